```python
import math
import jax
import jax.numpy as jnp
from jax import lax
import numpy as np

D_MODEL = 1024
BATCH = 4
SEQ = 8192
DEPTH = 2

GRID_W = 64
CTX_LEN = 256
HEAD_DIM = 64
NA_HEADS = 4
NA_WIN_H = 8
NA_WIN_W = 16
NA_WIDTH = NA_HEADS * HEAD_DIM
GM_GROUPS = 4
GM_CHUNK = 128
GM_WIDTH = 256
GM_GROUP_DIM = GM_WIDTH // GM_GROUPS
DA_HEADS = 4
DA_QK_DIM = 2 * HEAD_DIM
DA_V_DIM = 2 * HEAD_DIM
DA_WIDTH = DA_HEADS * DA_V_DIM
MIX_WIDTH = NA_WIDTH + GM_WIDTH + DA_WIDTH
QU_WIDTH = NA_WIDTH + DA_HEADS * DA_QK_DIM + 2 * GM_WIDTH
KV_WIDTH = 2 * NA_WIDTH + DA_HEADS * DA_QK_DIM + DA_HEADS * DA_V_DIM
IN_WIDTH = QU_WIDTH + KV_WIDTH
D_FF = -(-8 * D_MODEL // (3 * 256)) * 256
ROPE_BASE = 10000.0
EPS = 1e-6
BLOCK_Q = 128
NEG_INF = -1e30

kernel_name = 'hybrid_natten_gmlp_diffattn_dit'


def rms_norm(x, g):
    xf = x.astype(jnp.float32)
    y = xf * lax.rsqrt(jnp.mean(xf * xf, axis=-1, keepdims=True) + EPS)
    return (y * g.astype(jnp.float32)).astype(x.dtype)


def modulate(x, g, shift, scale):
    return rms_norm(x, g) * (1 + scale[:, None]) + shift[:, None]


def axial_rope(n):
    t = jnp.arange(n, dtype=jnp.int32)
    row = (t // GRID_W).astype(jnp.float32)
    col = (t % GRID_W).astype(jnp.float32)
    half = HEAD_DIM // 2
    inv = ROPE_BASE ** (-jnp.arange(0, half, 2, dtype=jnp.float32) / half)
    ar = row[:, None] * inv[None, :]
    ac = col[:, None] * inv[None, :]
    ang = jnp.concatenate([ar, ar, ac, ac], axis=-1)
    return jnp.cos(ang), jnp.sin(ang)


def apply_rope(x, cos, sin):
    xr = x.reshape(x.shape[:-1] + (2, 2, HEAD_DIM // 4))
    x1 = xr[..., 0, :]
    x2 = xr[..., 1, :]
    rot = jnp.stack([-x2, x1], axis=-2).reshape(x.shape)
    return x * cos.astype(x.dtype) + rot * sin.astype(x.dtype)


def split_qu(qu):
    lead = qu.shape[:-1]
    qa = qu[..., :NA_WIDTH].reshape(lead + (NA_HEADS, HEAD_DIM))
    o = NA_WIDTH
    qd = qu[..., o:o + DA_HEADS * DA_QK_DIM].reshape(lead + (DA_HEADS, 2, HEAD_DIM))
    o = o + DA_HEADS * DA_QK_DIM
    uv = qu[..., o:o + 2 * GM_WIDTH]
    return qa, qd, uv


def split_kv(kv):
    lead = kv.shape[:-1]
    ka = kv[..., :NA_WIDTH].reshape(lead + (NA_HEADS, HEAD_DIM))
    va = kv[..., NA_WIDTH:2 * NA_WIDTH].reshape(lead + (NA_HEADS, HEAD_DIM))
    o = 2 * NA_WIDTH
    kd = kv[..., o:o + DA_HEADS * DA_QK_DIM].reshape(lead + (DA_HEADS, 2, HEAD_DIM))
    o = o + DA_HEADS * DA_QK_DIM
    vd = kv[..., o:o + DA_HEADS * DA_V_DIM].reshape(lead + (DA_HEADS, DA_V_DIM))
    return ka, va, kd, vd


def neighborhood_attention(q, k, v, k_c, v_c, rpb):
    b, n, h, d = q.shape
    rows = n // GRID_W
    kh = min(NA_WIN_H, rows)
    kw = NA_WIN_W
    r = jnp.arange(rows)
    key_rows = jnp.clip(r - kh // 2, 0, rows - kh)[:, None] + jnp.arange(kh)[None, :]
    cq = jnp.arange(GRID_W)
    c0 = jnp.clip(cq - kw // 2, 0, GRID_W - kw)
    col_in = (cq[None, :] >= c0[:, None]) & (cq[None, :] < c0[:, None] + kw)
    qg = q.reshape(b, rows, GRID_W, h, d)
    kg = k.reshape(b, rows, GRID_W, h, d)[:, key_rows]
    vg = v.reshape(b, rows, GRID_W, h, d)[:, key_rows]
    scale = d ** -0.5
    s_win = jnp.einsum('brqhd,brjkhd->bhrqjk', qg, kg, preferred_element_type=jnp.float32) * scale
    roff = key_rows - r[:, None] + (NA_WIN_H - 1)
    coff = jnp.clip(cq[None, :] - cq[:, None], -(kw - 1), kw - 1) + (NA_WIN_W - 1)
    bias = rpb.astype(jnp.float32)[:, roff[:, None, :, None], coff[None, :, None, :]]
    s_win = jnp.where(col_in[:, None, :], s_win + bias[None], NEG_INF)
    s_ctx = jnp.einsum('brqhd,blhd->bhrql', qg, k_c, preferred_element_type=jnp.float32) * scale
    n_win = kh * GRID_W
    s = jnp.concatenate([s_win.reshape(b, h, rows, GRID_W, n_win), s_ctx], axis=-1)
    p = jax.nn.softmax(s, axis=-1).astype(v.dtype)
    p_win = p[..., :n_win].reshape(b, h, rows, GRID_W, kh, GRID_W)
    o = jnp.einsum('bhrqjk,brjkhd->brqhd', p_win, vg) + jnp.einsum('bhrql,blhd->brqhd', p[..., n_win:], v_c)
    return o.reshape(b, n, h * d)


def dense_attention(q, k, v):
    b, nq, h, d = q.shape
    s = jnp.einsum('bqhd,bkhd->bhqk', q, k, preferred_element_type=jnp.float32) * (d ** -0.5)
    p = jax.nn.softmax(s, axis=-1).astype(v.dtype)
    return jnp.einsum('bhqk,bkhd->bqhd', p, v).reshape(b, nq, h * d)


def chunk_spatial_gating(uv, v_g, ws, bs):
    b, n, _ = uv.shape
    z = jax.nn.gelu(uv)
    u = z[..., :GM_WIDTH]
    v = rms_norm(z[..., GM_WIDTH:], v_g)
    vc = v.reshape(b, n // GM_CHUNK, GM_CHUNK, GM_GROUPS, GM_GROUP_DIM)
    s = jnp.einsum('gts,bcsgd->bctgd', ws, vc) + bs.T[None, None, :, :, None]
    return u * s.reshape(b, n, GM_WIDTH)


def diff_attend(q, k, v, lam):
    s = jnp.einsum('bqhmd,bkhmd->bhmqk', q, k, preferred_element_type=jnp.float32) * (HEAD_DIM ** -0.5)
    p = jax.nn.softmax(s, axis=-1)
    a = p[:, :, 0] - lam * p[:, :, 1]
    return jnp.einsum('bhqk,bkhe->bqhe', a.astype(v.dtype), v)


def diff_attention_blocks(q, k, v, k_c, v_c, lam):
    b, n, h, _, d = q.shape
    k_all = jnp.concatenate([k, k_c], axis=1)
    v_all = jnp.concatenate([v, v_c], axis=1)
    nb = n // BLOCK_Q
    qb = jnp.moveaxis(q.reshape(b, nb, BLOCK_Q, h, 2, d), 1, 0)
    o = lax.map(lambda qblk: diff_attend(qblk, k_all, v_all, lam), qb)
    return jnp.moveaxis(o, 0, 1).reshape(b, n, h, DA_V_DIM)


def diff_head_out(o, sub_g, lam_init):
    b, n = o.shape[0], o.shape[1]
    return (rms_norm(o, sub_g) * (1 - lam_init)).reshape(b, n, DA_WIDTH)


def swiglu(h, w1, w3, w2):
    return (jax.nn.silu(h @ w1) * (h @ w3)) @ w2


def hybrid_layer(x, xc, c, c_ctx, lam_init, update_ctx,
                 w_mod, b_mod, norm1_g, w_in, na_q_g, na_k_g, na_rpb,
                 gm_v_g, gm_ws, gm_bs, da_q_g, da_k_g, da_lq1, da_lk1, da_lq2, da_lk2,
                 da_sub_g, w_out, norm2_g, ffn_w1, ffn_w3, ffn_w2):
    b, n, _ = x.shape
    mod = (jax.nn.silu(c) @ w_mod + b_mod).reshape(b, 6, D_MODEL)
    mod_c = (jax.nn.silu(c_ctx) @ w_mod + b_mod).reshape(1, 6, D_MODEL)
    h = modulate(x, norm1_g, mod[:, 0], mod[:, 1])
    hc = modulate(xc, norm1_g, mod_c[:, 0], mod_c[:, 1])
    proj = h @ w_in
    qa, qd, uv = split_qu(proj[..., :QU_WIDTH])
    ka, va, kd, vd = split_kv(proj[..., QU_WIDTH:])
    if update_ctx:
        proj_c = hc @ w_in
        kv_c = proj_c[..., QU_WIDTH:]
    else:
        kv_c = hc @ w_in[:, QU_WIDTH:]
    ka_c, va_c, kd_c, vd_c = split_kv(kv_c)
    qa = rms_norm(qa, na_q_g)
    ka = rms_norm(ka, na_k_g)
    ka_c = rms_norm(ka_c, na_k_g)
    cos, sin = axial_rope(n)
    cos = cos[:, None, None, :]
    sin = sin[:, None, None, :]
    qd = apply_rope(rms_norm(qd, da_q_g), cos, sin)
    kd = apply_rope(rms_norm(kd, da_k_g), cos, sin)
    kd_c = rms_norm(kd_c, da_k_g)
    f32 = jnp.float32
    lam = (jnp.exp(jnp.sum(da_lq1.astype(f32) * da_lk1.astype(f32)))
           - jnp.exp(jnp.sum(da_lq2.astype(f32) * da_lk2.astype(f32))) + lam_init)
    o_a = neighborhood_attention(qa, ka, va, ka_c, va_c, na_rpb)
    o_b = chunk_spatial_gating(uv, gm_v_g, gm_ws, gm_bs)
    o_c = diff_head_out(diff_attention_blocks(qd, kd, vd, kd_c, vd_c, lam), da_sub_g, lam_init)
    mixed = jnp.concatenate([o_a, o_b, o_c], axis=-1) @ w_out
    x = x + mod[:, 2][:, None] * mixed
    x = x + mod[:, 5][:, None] * swiglu(modulate(x, norm2_g, mod[:, 3], mod[:, 4]), ffn_w1, ffn_w3, ffn_w2)
    if update_ctx:
        qa_c, qd_c, uv_c = split_qu(proj_c[..., :QU_WIDTH])
        oa_c = dense_attention(rms_norm(qa_c, na_q_g), ka_c, va_c)
        ob_c = chunk_spatial_gating(uv_c, gm_v_g, gm_ws, gm_bs)
        oc_c = diff_head_out(diff_attend(rms_norm(qd_c, da_q_g), kd_c, vd_c, lam), da_sub_g, lam_init)
        mixed_c = jnp.concatenate([oa_c, ob_c, oc_c], axis=-1) @ w_out
        xc = xc + mod_c[:, 2][:, None] * mixed_c
        xc = xc + mod_c[:, 5][:, None] * swiglu(modulate(xc, norm2_g, mod_c[:, 3], mod_c[:, 4]), ffn_w1, ffn_w3, ffn_w2)
    return x, xc


def setup_inputs(seed: int = 0) -> dict:
    key = jax.random.key(seed)
    ks = jax.random.split(key, 32)

    def nrm(k, shape, s):
        return jax.random.normal(k, shape, jnp.float32) * s

    L = DEPTH
    return {
        'x': nrm(ks[0], (BATCH, SEQ, D_MODEL), 1.0),
        'c': nrm(ks[1], (BATCH, D_MODEL), 1.0),
        'ctx': nrm(ks[2], (BATCH, CTX_LEN, D_MODEL), 1.0),
        'c_ctx': nrm(ks[3], (D_MODEL,), 1.0),
        'w_mod': nrm(ks[4], (L, D_MODEL, 6 * D_MODEL), 0.5 * D_MODEL ** -0.5),
        'b_mod': nrm(ks[5], (L, 6 * D_MODEL), 0.02),
        'norm1_g': 1.0 + nrm(ks[6], (L, D_MODEL), 0.05),
        'w_in': nrm(ks[7], (L, D_MODEL, IN_WIDTH), D_MODEL ** -0.5),
        'na_q_g': 1.0 + nrm(ks[8], (L, HEAD_DIM), 0.05),
        'na_k_g': 1.0 + nrm(ks[9], (L, HEAD_DIM), 0.05),
        'na_rpb': nrm(ks[10], (L, NA_HEADS, 2 * NA_WIN_H - 1, 2 * NA_WIN_W - 1), 0.1),
        'gm_v_g': 1.0 + nrm(ks[11], (L, GM_WIDTH), 0.05),
        'gm_ws': nrm(ks[12], (L, GM_GROUPS, GM_CHUNK, GM_CHUNK), GM_CHUNK ** -0.5),
        'gm_bs': 1.0 + nrm(ks[13], (L, GM_GROUPS, GM_CHUNK), 0.02),
        'da_q_g': 1.0 + nrm(ks[14], (L, HEAD_DIM), 0.05),
        'da_k_g': 1.0 + nrm(ks[15], (L, HEAD_DIM), 0.05),
        'da_lq1': nrm(ks[16], (L, HEAD_DIM), 0.1),
        'da_lk1': nrm(ks[17], (L, HEAD_DIM), 0.1),
        'da_lq2': nrm(ks[18], (L, HEAD_DIM), 0.1),
        'da_lk2': nrm(ks[19], (L, HEAD_DIM), 0.1),
        'da_sub_g': 1.0 + nrm(ks[20], (L, DA_V_DIM), 0.05),
        'w_out': nrm(ks[21], (L, MIX_WIDTH, D_MODEL), MIX_WIDTH ** -0.5),
        'norm2_g': 1.0 + nrm(ks[22], (L, D_MODEL), 0.05),
        'ffn_w1': nrm(ks[23], (L, D_MODEL, D_FF), D_MODEL ** -0.5),
        'ffn_w3': nrm(ks[24], (L, D_MODEL, D_FF), D_MODEL ** -0.5),
        'ffn_w2': nrm(ks[25], (L, D_FF, D_MODEL), D_FF ** -0.5),
    }


def reference(x, c, ctx, c_ctx, w_mod, b_mod, norm1_g, w_in, na_q_g, na_k_g, na_rpb,
              gm_v_g, gm_ws, gm_bs, da_q_g, da_k_g, da_lq1, da_lk1, da_lq2, da_lk2,
              da_sub_g, w_out, norm2_g, ffn_w1, ffn_w3, ffn_w2):
    xc = ctx
    for i in range(DEPTH):
        lam_init = 0.8 - 0.6 * math.exp(-0.3 * i)
        x, xc = hybrid_layer(
            x, xc, c, c_ctx, lam_init, i < DEPTH - 1,
            w_mod[i], b_mod[i], norm1_g[i], w_in[i], na_q_g[i], na_k_g[i], na_rpb[i],
            gm_v_g[i], gm_ws[i], gm_bs[i], da_q_g[i], da_k_g[i], da_lq1[i], da_lk1[i],
            da_lq2[i], da_lk2[i], da_sub_g[i], w_out[i], norm2_g[i], ffn_w1[i], ffn_w3[i], ffn_w2[i])
    return x
```

```python
import functools
import math

import numpy as np
import jax
import jax.numpy as jnp
from jax import lax
from jax.experimental import pallas as pl
from jax.experimental.pallas import tpu as pltpu

D_MODEL = 1024
GRID_W = 64
CTX_LEN = 256
HEAD_DIM = 64
NA_HEADS = 4
NA_WIN_H = 8
NA_WIN_W = 16
NA_WIDTH = NA_HEADS * HEAD_DIM
GM_GROUPS = 4
GM_CHUNK = 128
GM_WIDTH = 256
DA_HEADS = 4
DA_QK_DIM = 2 * HEAD_DIM
DA_V_DIM = 2 * HEAD_DIM
DA_WIDTH = DA_HEADS * DA_V_DIM
QU_WIDTH = NA_WIDTH + DA_HEADS * DA_QK_DIM + 2 * GM_WIDTH
IN_WIDTH = QU_WIDTH + 2 * NA_WIDTH + DA_HEADS * DA_QK_DIM + DA_HEADS * DA_V_DIM
D_FF = -(-8 * D_MODEL // (3 * 256)) * 256
ROPE_BASE = 10000.0
EPS = 1e-6
NEG_INF = -1e30
LOG2E = math.log2(math.e)

COL_QA = 0
COL_QD = NA_WIDTH
COL_UV = COL_QD + DA_HEADS * DA_QK_DIM
COL_KA = QU_WIDTH
COL_VA = COL_KA + NA_WIDTH
COL_KD = COL_VA + NA_WIDTH
COL_VD = COL_KD + DA_HEADS * DA_QK_DIM

MOD_ROWS = 8
CTX_MOD_ROW = 4
NA_Q_ROWS = 4
NA_K_ROWS = 12
ONES_ROWS = 16
V7X_VMEM_LIMIT = 56 * 2 ** 20

F32 = jnp.float32
BF16 = jnp.bfloat16


def _dot(a, b):
    return jnp.dot(a, b, preferred_element_type=F32)


def _dot_nt(a, b):
    return lax.dot_general(a, b, (((1,), (1,)), ((), ())), preferred_element_type=F32)


def _resident(shape, index_map):
    return pl.BlockSpec(shape, index_map, pipeline_mode=pl.Buffered(1))


def _mod_kernel(c_ref, w_ref, b_ref, o_ref):
    c = c_ref[...]
    a = c * jax.nn.sigmoid(c)
    a_hi = a.astype(BF16)
    a_lo = (a - a_hi.astype(F32)).astype(BF16)
    w = w_ref[...]
    w_hi = w.astype(BF16)
    w_lo = (w - w_hi.astype(F32)).astype(BF16)
    o_ref[...] = _dot(a_hi, w_hi) + _dot(a_lo, w_hi) + _dot(a_hi, w_lo) + b_ref[...]


def _modulation(c8, w_mod, b_mod):
    depth = w_mod.shape[0]
    tn = 1024
    return pl.pallas_call(
        _mod_kernel,
        grid=(depth, 6 * D_MODEL // tn),
        in_specs=[
            pl.BlockSpec((MOD_ROWS, D_MODEL), lambda l, j: (0, 0)),
            pl.BlockSpec((None, D_MODEL, tn), lambda l, j: (l, 0, j)),
            pl.BlockSpec((None, 1, tn), lambda l, j: (l, 0, j)),
        ],
        out_specs=pl.BlockSpec((None, MOD_ROWS, tn), lambda l, j: (l, 0, j)),
        out_shape=jax.ShapeDtypeStruct((depth, MOD_ROWS, 6 * D_MODEL), F32),
        name="adaln_mod",
    )(c8, w_mod, b_mod.reshape(depth, 1, 6 * D_MODEL))


def _inproj_kernel(x_ref, shift_ref, scale_ref, g1_ref, w_ref, cos_ref, sa_ref, sb_ref,
                   gqa_ref, gka_ref, gqd_ref, gkd_ref, gmat_ref, gv_ref, ws_ref, bs_ref,
                   qa_ref, qd_ref, ka_ref, va_ref, kd_ref, vdt_ref, ob_ref, *, tm):
    x = x_ref[...]
    h = x * lax.rsqrt(jnp.mean(x * x, axis=-1, keepdims=True) + EPS) * g1_ref[...]
    h = h * (1.0 + scale_ref[0]) + shift_ref[0]
    hb = h.astype(BF16)
    gmat = gmat_ref[...]
    cos = cos_ref[...]
    sa = sa_ref[...]
    sb = sb_ref[...]

    def proj(c0, width):
        return _dot(hb, w_ref[:, c0:c0 + width])

    def head_norm(y, g):
        ss = _dot((y * y).astype(BF16), gmat)
        return y * lax.rsqrt(ss * (1.0 / HEAD_DIM) + EPS) * g

    def rope(z):
        return z * cos + pltpu.roll(z, 128 - 16, 1) * sa + pltpu.roll(z, 16, 1) * sb

    qa_ref[...] = head_norm(proj(COL_QA, 256), gqa_ref[...]).astype(BF16)
    ka_ref[...] = head_norm(proj(COL_KA, 256), gka_ref[...]).astype(BF16)
    va_ref[...] = proj(COL_VA, 256).astype(BF16)
    for c in range(2):
        yq = head_norm(proj(COL_QD + 256 * c, 256), gqd_ref[...])
        yk = head_norm(proj(COL_KD + 256 * c, 256), gkd_ref[...])
        for t in range(2):
            lo = 256 * c + 128 * t
            qd_ref[:, lo:lo + 128] = rope(yq[:, 128 * t:128 * t + 128]).astype(BF16)
            kd_ref[:, lo:lo + 128] = rope(yk[:, 128 * t:128 * t + 128]).astype(BF16)
    vdt_ref[...] = proj(COL_VD, 512).T.astype(BF16)

    z = jax.nn.gelu(proj(COL_UV, 2 * GM_WIDTH))
    u = z[:, :GM_WIDTH]
    v = z[:, GM_WIDTH:]
    v = v * lax.rsqrt(jnp.mean(v * v, axis=-1, keepdims=True) + EPS) * gv_ref[...]
    vb = v.astype(BF16)
    group = lax.broadcasted_iota(jnp.int32, (GM_CHUNK, GM_WIDTH), 1) // (GM_WIDTH // GM_GROUPS)
    ws = ws_ref[...]
    bs = bs_ref[...]
    for c in range(tm // GM_CHUNK):
        vc = vb[c * GM_CHUNK:(c + 1) * GM_CHUNK, :]
        vbd = jnp.concatenate([jnp.where(group == g, vc, jnp.zeros_like(vc)) for g in range(GM_GROUPS)], axis=0)
        s = _dot(ws, vbd) + bs
        ob_ref[c * GM_CHUNK:(c + 1) * GM_CHUNK, :] = (u[c * GM_CHUNK:(c + 1) * GM_CHUNK, :] * s).astype(BF16)


def _inproj(x2d, mod3, lw, rope_tabs, *, batch, n, tm, per_batch_mod):
    nt = n // tm
    t_tot = batch * n
    row = (lambda b: b) if per_batch_mod else (lambda b: CTX_MOD_ROW)
    const = lambda b, i: (0, 0)
    tok = lambda b, i: (b * nt + i, 0)
    in_specs = [
        pl.BlockSpec((tm, D_MODEL), tok),
        pl.BlockSpec((1, 1, D_MODEL), lambda b, i: (row(b), 0, 0)),
        pl.BlockSpec((1, 1, D_MODEL), lambda b, i: (row(b), 0, 1)),
        pl.BlockSpec((1, D_MODEL), const),
        _resident((D_MODEL, IN_WIDTH), const),
        pl.BlockSpec((tm, 128), lambda b, i: (i, 0)),
        pl.BlockSpec((tm, 128), lambda b, i: (i, 0)),
        pl.BlockSpec((tm, 128), lambda b, i: (i, 0)),
        pl.BlockSpec((1, 256), const),
        pl.BlockSpec((1, 256), const),
        pl.BlockSpec((1, 256), const),
        pl.BlockSpec((1, 256), const),
        pl.BlockSpec((256, 256), const),
        pl.BlockSpec((1, GM_WIDTH), const),
        pl.BlockSpec((GM_CHUNK, GM_GROUPS * GM_CHUNK), const),
        pl.BlockSpec((GM_CHUNK, GM_WIDTH), const),
    ]
    out_specs = [
        pl.BlockSpec((tm, 256), tok),
        pl.BlockSpec((tm, 512), tok),
        pl.BlockSpec((tm, 256), tok),
        pl.BlockSpec((tm, 256), tok),
        pl.BlockSpec((tm, 512), tok),
        pl.BlockSpec((None, 512, tm), lambda b, i: (b, 0, i)),
        pl.BlockSpec((tm, 256), tok),
    ]
    out_shape = [
        jax.ShapeDtypeStruct((t_tot, 256), BF16),
        jax.ShapeDtypeStruct((t_tot, 512), BF16),
        jax.ShapeDtypeStruct((t_tot, 256), BF16),
        jax.ShapeDtypeStruct((t_tot, 256), BF16),
        jax.ShapeDtypeStruct((t_tot, 512), BF16),
        jax.ShapeDtypeStruct((batch, 512, n), BF16),
        jax.ShapeDtypeStruct((t_tot, 256), BF16),
    ]
    return pl.pallas_call(
        functools.partial(_inproj_kernel, tm=tm),
        grid=(batch, nt),
        in_specs=in_specs,
        out_specs=out_specs,
        out_shape=out_shape,
        compiler_params=pltpu.CompilerParams(
            dimension_semantics=("arbitrary", "arbitrary"), vmem_limit_bytes=V7X_VMEM_LIMIT),
        name="inproj",
    )(x2d, mod3, mod3, lw["norm1_g"], lw["w_in"], *rope_tabs,
      lw["gqa"], lw["gka"], lw["gqd"], lw["gkd"], lw["gmat"], lw["gv"], lw["ws_cat"], lw["bs_tab"])


def _na_kernel(*refs, has_window):
    if has_window:
        q_ref, k_ref, v_ref, kc_ref, vc_ref, tab_ref, o_ref = refs
        i = pl.program_id(1)
        u0 = jnp.clip(NA_Q_ROWS * i - NA_WIN_H // 2, 0, k_ref.shape[0] // GRID_W - NA_K_ROWS)
        off = pl.multiple_of(u0 * GRID_W, GRID_W)
        kw = k_ref[pl.ds(off, NA_K_ROWS * GRID_W), :]
        vw = v_ref[pl.ds(off, NA_K_ROWS * GRID_W), :]
    else:
        q_ref, kc_ref, vc_ref, o_ref = refs
    lane_head = lax.broadcasted_iota(jnp.int32, (1, 128), 1) // HEAD_DIM
    for p in range(NA_HEADS // 2):
        cols = slice(128 * p, 128 * p + 128)
        qp = q_ref[:, cols]
        kcp = kc_ref[:, cols]
        vcp = vc_ref[:, cols]
        acc = jnp.zeros((qp.shape[0], 128), F32)
        for j in range(2):
            sel = (lane_head == j).astype(BF16)
            qm = qp * sel
            sc = _dot_nt(qm, kcp)
            m = jnp.max(sc, axis=1, keepdims=True)
            if has_window:
                sw = _dot_nt(qm, kw[:, cols]) + tab_ref[2 * p + j]
                m = jnp.maximum(m, jnp.max(sw, axis=1, keepdims=True))
            ec = jnp.exp(sc - m)
            l = jnp.sum(ec, axis=1, keepdims=True)
            o = _dot(ec.astype(BF16), vcp * sel)
            if has_window:
                ew = jnp.exp(sw - m)
                l = l + jnp.sum(ew, axis=1, keepdims=True)
                o = o + _dot(ew.astype(BF16), vw[:, cols] * sel)
            acc = acc + o / l
        o_ref[:, cols] = acc.astype(BF16)


def _na_window(qa, ka, va, kac, vac, tab, *, batch, n):
    tq = NA_Q_ROWS * GRID_W
    nt = n // tq
    return pl.pallas_call(
        functools.partial(_na_kernel, has_window=True),
        grid=(batch, nt),
        in_specs=[
            pl.BlockSpec((tq, 256), lambda b, i: (b * nt + i, 0)),
            pl.BlockSpec((n, 256), lambda b, i: (b, 0)),
            pl.BlockSpec((n, 256), lambda b, i: (b, 0)),
            pl.BlockSpec((CTX_LEN, 256), lambda b, i: (b, 0)),
            pl.BlockSpec((CTX_LEN, 256), lambda b, i: (b, 0)),
            pl.BlockSpec((None, NA_HEADS, tq, NA_K_ROWS * GRID_W),
                         lambda b, i: (jnp.where(i == 0, 0, jnp.where(i == nt - 1, 2, 1)), 0, 0, 0)),
        ],
        out_specs=pl.BlockSpec((tq, 256), lambda b, i: (b * nt + i, 0)),
        out_shape=jax.ShapeDtypeStruct((batch * n, 256), BF16),
        compiler_params=pltpu.CompilerParams(
            dimension_semantics=("arbitrary", "arbitrary"), vmem_limit_bytes=V7X_VMEM_LIMIT),
        name="na_window",
    )(qa, ka, va, kac, vac, tab)


def _na_dense(qac, kac, vac, *, batch):
    spec = pl.BlockSpec((CTX_LEN, 256), lambda b: (b, 0))
    return pl.pallas_call(
        functools.partial(_na_kernel, has_window=False),
        grid=(batch,),
        in_specs=[spec, spec, spec],
        out_specs=spec,
        out_shape=jax.ShapeDtypeStruct((batch * CTX_LEN, 256), BF16),
        name="na_dense",
    )(qac, kac, vac)


def _da_kernel(*refs, n_lat_blocks, kt, lam_init):
    if n_lat_blocks:
        q_ref, k_ref, vt_ref, kc_ref, vtc_ref, lq1, lk1, lq2, lk2, subg_ref, o_ref, acc_ref, m_ref = refs
    else:
        q_ref, kc_ref, vtc_ref, lq1, lk1, lq2, lk2, subg_ref, o_ref, acc_ref, m_ref = refs
    q = q_ref[...]
    lane_map = lax.broadcasted_iota(jnp.int32, (1, 128), 1) // HEAD_DIM
    qms = [q * (lane_map == mi).astype(BF16) for mi in range(2)]
    acc_ref[...] = jnp.zeros_like(acc_ref)
    m_ref[...] = jnp.full_like(m_ref, NEG_INF)

    def step(kblk, vtblk):
        vte = jnp.concatenate([vtblk, jnp.ones((ONES_ROWS, vtblk.shape[1]), BF16)], axis=0)
        for mi in range(2):
            s = _dot_nt(kblk, qms[mi])
            m_old = m_ref[mi]
            m_new = jnp.maximum(m_old, jnp.max(s, axis=0, keepdims=True))
            alpha = jnp.exp2(m_old - m_new)
            e = jnp.exp2(s - m_new).astype(BF16)
            acc_ref[mi] = acc_ref[mi] * alpha + _dot(vte, e)
            m_ref[mi] = m_new

    if n_lat_blocks:
        def body(j, carry):
            o = pl.multiple_of(j * kt, kt)
            step(k_ref[pl.ds(o, kt), :], vt_ref[:, pl.ds(o, kt)])
            return carry
        lax.fori_loop(0, n_lat_blocks, body, 0)
    step(kc_ref[...], vtc_ref[...])

    lam = (jnp.exp(jnp.sum(lq1[...] * lk1[...], keepdims=True))
           - jnp.exp(jnp.sum(lq2[...] * lk2[...], keepdims=True)) + lam_init)
    a0 = acc_ref[0]
    a1 = acc_ref[1]
    o = a0[:DA_V_DIM] / a0[DA_V_DIM:DA_V_DIM + 1] - lam * (a1[:DA_V_DIM] / a1[DA_V_DIM:DA_V_DIM + 1])
    y = o * lax.rsqrt(jnp.mean(o * o, axis=0, keepdims=True) + EPS) * subg_ref[...] * (1.0 - lam_init)
    o_ref[...] = y.T.astype(BF16)


def _diff_attention(qd, kd, vdt, kdc, vdtc, lw, *, batch, nq, nk, tq, kt, lam_init):
    nt = nq // tq
    lat = nk > 0
    in_specs = [pl.BlockSpec((tq, 128), lambda b, h, i: (b * nt + i, h))]
    args = [qd]
    if lat:
        in_specs += [pl.BlockSpec((nk, 128), lambda b, h, i: (b, h)),
                     pl.BlockSpec((None, 128, nk), lambda b, h, i: (b, h, 0))]
        args += [kd, vdt]
    in_specs += [pl.BlockSpec((CTX_LEN, 128), lambda b, h, i: (b, h)),
                 pl.BlockSpec((None, 128, CTX_LEN), lambda b, h, i: (b, h, 0))]
    args += [kdc, vdtc]
    in_specs += [pl.BlockSpec((1, HEAD_DIM), lambda b, h, i: (0, 0))] * 4
    args += [lw["lq1"], lw["lk1"], lw["lq2"], lw["lk2"]]
    in_specs += [pl.BlockSpec((DA_V_DIM, 1), lambda b, h, i: (0, 0))]
    args += [lw["sub_g"]]
    return pl.pallas_call(
        functools.partial(_da_kernel, n_lat_blocks=nk // kt if lat else 0, kt=kt, lam_init=lam_init),
        grid=(batch, DA_HEADS, nt),
        in_specs=in_specs,
        out_specs=pl.BlockSpec((tq, 128), lambda b, h, i: (b * nt + i, h)),
        out_shape=jax.ShapeDtypeStruct((batch * nq, DA_WIDTH), BF16),
        scratch_shapes=[pltpu.VMEM((2, DA_V_DIM + ONES_ROWS, tq), F32), pltpu.VMEM((2, 1, tq), F32)],
        compiler_params=pltpu.CompilerParams(
            dimension_semantics=("arbitrary", "arbitrary", "arbitrary"), vmem_limit_bytes=V7X_VMEM_LIMIT),
        name="diff_attn" if lat else "diff_attn_ctx",
    )(*args)


def _ffn_kernel(x_ref, oa_ref, ob_ref, oc_ref, gate1_ref, shift2_ref, scale2_ref, gate2_ref, g2_ref,
                wout_ref, w1_ref, w3_ref, w2_ref, o_ref):
    mixed = (_dot(oa_ref[...], wout_ref[0:NA_WIDTH, :])
             + _dot(ob_ref[...], wout_ref[NA_WIDTH:NA_WIDTH + GM_WIDTH, :])
             + _dot(oc_ref[...], wout_ref[NA_WIDTH + GM_WIDTH:, :]))
    x1 = x_ref[...] + gate1_ref[0] * mixed
    h = x1 * lax.rsqrt(jnp.mean(x1 * x1, axis=-1, keepdims=True) + EPS) * g2_ref[...]
    hb = (h * (1.0 + scale2_ref[0]) + shift2_ref[0]).astype(BF16)
    a = _dot(hb, w1_ref[...])
    b = _dot(hb, w3_ref[...])
    g = (a * jax.nn.sigmoid(a) * b).astype(BF16)
    o_ref[...] = x1 + gate2_ref[0] * _dot(g, w2_ref[...])


def _out_ffn(x2d, oa, ob, oc, mod3, lw, *, batch, n, tm, per_batch_mod):
    nt = n // tm
    row = (lambda b: b) if per_batch_mod else (lambda b: CTX_MOD_ROW)
    const = lambda b, i: (0, 0)
    tok = lambda b, i: (b * nt + i, 0)
    mod_spec = lambda k: pl.BlockSpec((1, 1, D_MODEL), lambda b, i: (row(b), 0, k))
    return pl.pallas_call(
        _ffn_kernel,
        grid=(batch, nt),
        in_specs=[
            pl.BlockSpec((tm, D_MODEL), tok),
            pl.BlockSpec((tm, NA_WIDTH), tok),
            pl.BlockSpec((tm, GM_WIDTH), tok),
            pl.BlockSpec((tm, DA_WIDTH), tok),
            mod_spec(2), mod_spec(3), mod_spec(4), mod_spec(5),
            pl.BlockSpec((1, D_MODEL), const),
            _resident((D_MODEL, D_MODEL), const),
            _resident((D_MODEL, D_FF), const),
            _resident((D_MODEL, D_FF), const),
            _resident((D_FF, D_MODEL), const),
        ],
        out_specs=pl.BlockSpec((tm, D_MODEL), tok),
        out_shape=jax.ShapeDtypeStruct((batch * n, D_MODEL), F32),
        compiler_params=pltpu.CompilerParams(
            dimension_semantics=("arbitrary", "arbitrary"), vmem_limit_bytes=V7X_VMEM_LIMIT),
        name="out_ffn",
    )(x2d, oa, ob, oc, mod3, mod3, mod3, mod3, lw["norm2_g"], lw["w_out"], lw["w1"], lw["w3"], lw["w2"])


def _rope_tables(n):
    t = jnp.arange(n, dtype=jnp.int32)
    row = (t // GRID_W).astype(F32)
    col = (t % GRID_W).astype(F32)
    half = HEAD_DIM // 2
    inv = ROPE_BASE ** (-jnp.arange(0, half, 2, dtype=F32) / half)
    ar = row[:, None] * inv[None, :]
    ac = col[:, None] * inv[None, :]
    ang = jnp.concatenate([ar, ar, ac, ac], axis=-1)
    cos = jnp.cos(ang)
    sin = jnp.sin(ang)
    first = (np.arange(HEAD_DIM) % 32) < 16
    sa = jnp.where(first, -sin, 0.0)
    sb = jnp.where(first, 0.0, sin)
    return tuple(jnp.tile(a, (1, 2)) for a in (cos, sa, sb))


def _identity_rope_tables(n):
    return (jnp.ones((n, 128), F32), jnp.zeros((n, 128), F32), jnp.zeros((n, 128), F32))


def _na_bias_tables(rpb, rows):
    tabs = []
    for r0 in (0, 2 * NA_Q_ROWS, rows - NA_Q_ROWS):
        u0 = min(max(r0 - NA_WIN_H // 2, 0), rows - NA_K_ROWS)
        r = r0 + np.arange(NA_Q_ROWS)
        key_row = u0 + np.arange(NA_K_ROWS)
        start = np.clip(r - NA_WIN_H // 2, 0, rows - NA_WIN_H)
        row_ok = (key_row[None, :] >= start[:, None]) & (key_row[None, :] < start[:, None] + NA_WIN_H)
        roff = np.clip(key_row[None, :] - r[:, None] + (NA_WIN_H - 1), 0, 2 * NA_WIN_H - 2)
        cq = np.arange(GRID_W)
        c0 = np.clip(cq - NA_WIN_W // 2, 0, GRID_W - NA_WIN_W)
        col_ok = (cq[None, :] >= c0[:, None]) & (cq[None, :] < c0[:, None] + NA_WIN_W)
        coff = np.clip(cq[None, :] - cq[:, None], -(NA_WIN_W - 1), NA_WIN_W - 1) + (NA_WIN_W - 1)
        bias = rpb.astype(F32)[:, roff[:, None, :, None], coff[None, :, None, :]]
        ok = row_ok[:, None, :, None] & col_ok[None, :, None, :]
        tab = jnp.where(ok[None], bias, NEG_INF)
        tabs.append(tab.reshape(NA_HEADS, NA_Q_ROWS * GRID_W, NA_K_ROWS * GRID_W))
    return jnp.stack(tabs)


def _layer_weights(i, p):
    tile4 = lambda g: jnp.tile(g.astype(F32), 256 // HEAD_DIM).reshape(1, 256)
    blk = np.arange(256) // HEAD_DIM
    return {
        "norm1_g": p["norm1_g"][i].reshape(1, D_MODEL),
        "w_in": p["w_in"][i].astype(BF16),
        "gqa": tile4(p["na_q_g"][i]) * (HEAD_DIM ** -0.5),
        "gka": tile4(p["na_k_g"][i]),
        "gqd": tile4(p["da_q_g"][i]) * (HEAD_DIM ** -0.5 * LOG2E),
        "gkd": tile4(p["da_k_g"][i]),
        "gmat": jnp.asarray(blk[:, None] == blk[None, :], BF16),
        "gv": p["gm_v_g"][i].reshape(1, GM_WIDTH),
        "ws_cat": p["gm_ws"][i].transpose(1, 0, 2).reshape(GM_CHUNK, GM_GROUPS * GM_CHUNK).astype(BF16),
        "bs_tab": jnp.repeat(p["gm_bs"][i].T, GM_WIDTH // GM_GROUPS, axis=1),
        "lq1": p["da_lq1"][i].reshape(1, HEAD_DIM),
        "lk1": p["da_lk1"][i].reshape(1, HEAD_DIM),
        "lq2": p["da_lq2"][i].reshape(1, HEAD_DIM),
        "lk2": p["da_lk2"][i].reshape(1, HEAD_DIM),
        "sub_g": p["da_sub_g"][i].reshape(DA_V_DIM, 1),
        "norm2_g": p["norm2_g"][i].reshape(1, D_MODEL),
        "w_out": p["w_out"][i].astype(BF16),
        "w1": p["ffn_w1"][i].astype(BF16),
        "w3": p["ffn_w3"][i].astype(BF16),
        "w2": p["ffn_w2"][i].astype(BF16),
    }


def kernel(x, c, ctx, c_ctx, w_mod, b_mod, norm1_g, w_in, na_q_g, na_k_g, na_rpb, gm_v_g, gm_ws, gm_bs,
           da_q_g, da_k_g, da_lq1, da_lk1, da_lq2, da_lk2, da_sub_g, w_out, norm2_g, ffn_w1, ffn_w3, ffn_w2):
    batch, n, _ = x.shape
    depth = w_mod.shape[0]
    assert n % (NA_Q_ROWS * GRID_W) == 0 and ctx.shape[1] == CTX_LEN and batch < CTX_MOD_ROW + 1
    params = dict(norm1_g=norm1_g, w_in=w_in, na_q_g=na_q_g, na_k_g=na_k_g, gm_v_g=gm_v_g, gm_ws=gm_ws,
                  gm_bs=gm_bs, da_q_g=da_q_g, da_k_g=da_k_g, da_lq1=da_lq1, da_lk1=da_lk1, da_lq2=da_lq2,
                  da_lk2=da_lk2, da_sub_g=da_sub_g, w_out=w_out, norm2_g=norm2_g, ffn_w1=ffn_w1,
                  ffn_w3=ffn_w3, ffn_w2=ffn_w2)

    c8 = jnp.zeros((MOD_ROWS, D_MODEL), F32).at[:batch].set(c).at[CTX_MOD_ROW].set(c_ctx)
    mod_all = _modulation(c8, w_mod, b_mod)

    rope_lat = _rope_tables(n)
    rope_ctx = _identity_rope_tables(CTX_LEN)
    x2d = x.reshape(batch * n, D_MODEL)
    xc2d = ctx.reshape(batch * CTX_LEN, D_MODEL)

    for i in range(depth):
        lam_init = 0.8 - 0.6 * math.exp(-0.3 * i)
        lw = _layer_weights(i, params)
        mod3 = mod_all[i].reshape(MOD_ROWS, 1, 6 * D_MODEL)
        tab = _na_bias_tables(na_rpb[i], n // GRID_W)

        qa, qd, ka, va, kd, vdt, ob = _inproj(x2d, mod3, lw, rope_lat, batch=batch, n=n, tm=512,
                                              per_batch_mod=True)
        qac, qdc, kac, vac, kdc, vdtc, obc = _inproj(xc2d, mod3, lw, rope_ctx, batch=batch, n=CTX_LEN,
                                                     tm=CTX_LEN, per_batch_mod=False)
        oa = _na_window(qa, ka, va, kac, vac, tab, batch=batch, n=n)
        oc = _diff_attention(qd, kd, vdt, kdc, vdtc, lw, batch=batch, nq=n, nk=n, tq=512, kt=512,
                             lam_init=lam_init)
        x2d = _out_ffn(x2d, oa, ob, oc, mod3, lw, batch=batch, n=n, tm=256, per_batch_mod=True)
        if i < depth - 1:
            oac = _na_dense(qac, kac, vac, batch=batch)
            occ = _diff_attention(qdc, None, None, kdc, vdtc, lw, batch=batch, nq=CTX_LEN, nk=0,
                                  tq=CTX_LEN, kt=CTX_LEN, lam_init=lam_init)
            xc2d = _out_ffn(xc2d, oac, obc, occ, mod3, lw, batch=batch, n=CTX_LEN, tm=CTX_LEN,
                            per_batch_mod=False)
    return x2d.reshape(batch, n, D_MODEL)
```

```python
import functools
import math

import numpy as np
import jax
import jax.numpy as jnp
from jax import lax
from jax.experimental import pallas as pl
from jax.experimental.pallas import tpu as pltpu

D_MODEL = 1024
GRID_W = 64
CTX_LEN = 256
HEAD_DIM = 64
NA_HEADS = 4
NA_WIN_H = 8
NA_WIN_W = 16
NA_WIDTH = NA_HEADS * HEAD_DIM
GM_GROUPS = 4
GM_CHUNK = 128
GM_WIDTH = 256
DA_HEADS = 4
DA_QK_DIM = 2 * HEAD_DIM
DA_V_DIM = 2 * HEAD_DIM
DA_WIDTH = DA_HEADS * DA_V_DIM
QU_WIDTH = NA_WIDTH + DA_HEADS * DA_QK_DIM + 2 * GM_WIDTH
IN_WIDTH = QU_WIDTH + 2 * NA_WIDTH + DA_HEADS * DA_QK_DIM + DA_HEADS * DA_V_DIM
D_FF = -(-8 * D_MODEL // (3 * 256)) * 256
ROPE_BASE = 10000.0
EPS = 1e-6
NEG_INF = -1e30
LOG2E = math.log2(math.e)

COL_QA = 0
COL_QD = NA_WIDTH
COL_UV = COL_QD + DA_HEADS * DA_QK_DIM
COL_KA = QU_WIDTH
COL_VA = COL_KA + NA_WIDTH
COL_KD = COL_VA + NA_WIDTH
COL_VD = COL_KD + DA_HEADS * DA_QK_DIM

MOD_ROWS = 8
CTX_MOD_ROW = 4
NA_Q_ROWS = 4
NA_K_ROWS = 12
ONES_ROWS = 16
V7X_VMEM_LIMIT = 56 * 2 ** 20

F32 = jnp.float32
BF16 = jnp.bfloat16


def _dot(a, b):
    return jnp.dot(a, b, preferred_element_type=F32)


def _dot_nt(a, b):
    return lax.dot_general(a, b, (((1,), (1,)), ((), ())), preferred_element_type=F32)


def _resident(shape, index_map):
    return pl.BlockSpec(shape, index_map, pipeline_mode=pl.Buffered(1))


def _mod_kernel(c_ref, w_ref, b_ref, o_ref):
    c = c_ref[...]
    a = c * jax.nn.sigmoid(c)
    a_hi = a.astype(BF16)
    a_lo = (a - a_hi.astype(F32)).astype(BF16)
    w = w_ref[...]
    w_hi = w.astype(BF16)
    w_lo = (w - w_hi.astype(F32)).astype(BF16)
    o_ref[...] = _dot(a_hi, w_hi) + _dot(a_lo, w_hi) + _dot(a_hi, w_lo) + b_ref[...]


def _modulation(c8, w_mod, b_mod):
    depth = w_mod.shape[0]
    tn = 1024
    return pl.pallas_call(
        _mod_kernel,
        grid=(depth, 6 * D_MODEL // tn),
        in_specs=[
            pl.BlockSpec((MOD_ROWS, D_MODEL), lambda l, j: (0, 0)),
            pl.BlockSpec((None, D_MODEL, tn), lambda l, j: (l, 0, j)),
            pl.BlockSpec((None, 1, tn), lambda l, j: (l, 0, j)),
        ],
        out_specs=pl.BlockSpec((None, MOD_ROWS, tn), lambda l, j: (l, 0, j)),
        out_shape=jax.ShapeDtypeStruct((depth, MOD_ROWS, 6 * D_MODEL), F32),
        name="adaln_mod",
    )(c8, w_mod, b_mod.reshape(depth, 1, 6 * D_MODEL))


def _inproj_kernel(x_ref, shift_ref, scale_ref, g1_ref, w_ref, cos_ref, sa_ref, sb_ref,
                   gqa_ref, gka_ref, gqd_ref, gkd_ref, gmat_ref, gv_ref, ws_ref, bs_ref,
                   qa_ref, qd_ref, ka_ref, va_ref, kd_ref, vdt_ref, ob_ref, *, tm):
    x = x_ref[...]
    h = x * lax.rsqrt(jnp.mean(x * x, axis=-1, keepdims=True) + EPS) * g1_ref[...]
    h = h * (1.0 + scale_ref[0]) + shift_ref[0]
    hb = h.astype(BF16)
    gmat = gmat_ref[...]
    cos = cos_ref[...]
    sa = sa_ref[...]
    sb = sb_ref[...]

    def proj(c0, width):
        return _dot(hb, w_ref[:, c0:c0 + width])

    def head_norm(y, g):
        ss = _dot((y * y).astype(BF16), gmat)
        return y * lax.rsqrt(ss * (1.0 / HEAD_DIM) + EPS) * g

    def rope(z):
        return z * cos + pltpu.roll(z, 128 - 16, 1) * sa + pltpu.roll(z, 16, 1) * sb

    qa_ref[...] = head_norm(proj(COL_QA, 256), gqa_ref[...]).astype(BF16)
    ka_ref[...] = head_norm(proj(COL_KA, 256), gka_ref[...]).astype(BF16)
    va_ref[...] = proj(COL_VA, 256).astype(BF16)
    for c in range(2):
        yq = head_norm(proj(COL_QD + 256 * c, 256), gqd_ref[...])
        yk = head_norm(proj(COL_KD + 256 * c, 256), gkd_ref[...])
        for t in range(2):
            lo = 256 * c + 128 * t
            qd_ref[:, lo:lo + 128] = rope(yq[:, 128 * t:128 * t + 128]).astype(BF16)
            kd_ref[:, lo:lo + 128] = rope(yk[:, 128 * t:128 * t + 128]).astype(BF16)
    vdt_ref[...] = proj(COL_VD, 512).T.astype(BF16)

    z = jax.nn.gelu(proj(COL_UV, 2 * GM_WIDTH))
    u = z[:, :GM_WIDTH]
    v = z[:, GM_WIDTH:]
    v = v * lax.rsqrt(jnp.mean(v * v, axis=-1, keepdims=True) + EPS) * gv_ref[...]
    vb = v.astype(BF16)
    group = lax.broadcasted_iota(jnp.int32, (GM_CHUNK, GM_WIDTH), 1) // (GM_WIDTH // GM_GROUPS)
    ws = ws_ref[...]
    bs = bs_ref[...]
    for c in range(tm // GM_CHUNK):
        vc = vb[c * GM_CHUNK:(c + 1) * GM_CHUNK, :]
        vbd = jnp.concatenate([jnp.where(group == g, vc, jnp.zeros_like(vc)) for g in range(GM_GROUPS)], axis=0)
        s = _dot(ws, vbd) + bs
        ob_ref[c * GM_CHUNK:(c + 1) * GM_CHUNK, :] = (u[c * GM_CHUNK:(c + 1) * GM_CHUNK, :] * s).astype(BF16)


def _inproj(x2d, mod3, lw, rope_tabs, *, batch, n, tm, per_batch_mod):
    nt = n // tm
    t_tot = batch * n
    row = (lambda b: b) if per_batch_mod else (lambda b: CTX_MOD_ROW)
    const = lambda b, i: (0, 0)
    tok = lambda b, i: (b * nt + i, 0)
    in_specs = [
        pl.BlockSpec((tm, D_MODEL), tok),
        pl.BlockSpec((1, 1, D_MODEL), lambda b, i: (row(b), 0, 0)),
        pl.BlockSpec((1, 1, D_MODEL), lambda b, i: (row(b), 0, 1)),
        pl.BlockSpec((1, D_MODEL), const),
        _resident((D_MODEL, IN_WIDTH), const),
        pl.BlockSpec((tm, 128), lambda b, i: (i, 0)),
        pl.BlockSpec((tm, 128), lambda b, i: (i, 0)),
        pl.BlockSpec((tm, 128), lambda b, i: (i, 0)),
        pl.BlockSpec((1, 256), const),
        pl.BlockSpec((1, 256), const),
        pl.BlockSpec((1, 256), const),
        pl.BlockSpec((1, 256), const),
        pl.BlockSpec((256, 256), const),
        pl.BlockSpec((1, GM_WIDTH), const),
        pl.BlockSpec((GM_CHUNK, GM_GROUPS * GM_CHUNK), const),
        pl.BlockSpec((GM_CHUNK, GM_WIDTH), const),
    ]
    out_specs = [
        pl.BlockSpec((tm, 256), tok),
        pl.BlockSpec((tm, 512), tok),
        pl.BlockSpec((tm, 256), tok),
        pl.BlockSpec((tm, 256), tok),
        pl.BlockSpec((tm, 512), tok),
        pl.BlockSpec((None, 512, tm), lambda b, i: (b, 0, i)),
        pl.BlockSpec((tm, 256), tok),
    ]
    out_shape = [
        jax.ShapeDtypeStruct((t_tot, 256), BF16),
        jax.ShapeDtypeStruct((t_tot, 512), BF16),
        jax.ShapeDtypeStruct((t_tot, 256), BF16),
        jax.ShapeDtypeStruct((t_tot, 256), BF16),
        jax.ShapeDtypeStruct((t_tot, 512), BF16),
        jax.ShapeDtypeStruct((batch, 512, n), BF16),
        jax.ShapeDtypeStruct((t_tot, 256), BF16),
    ]
    return pl.pallas_call(
        functools.partial(_inproj_kernel, tm=tm),
        grid=(batch, nt),
        in_specs=in_specs,
        out_specs=out_specs,
        out_shape=out_shape,
        compiler_params=pltpu.CompilerParams(
            dimension_semantics=("arbitrary", "arbitrary"), vmem_limit_bytes=V7X_VMEM_LIMIT),
        name="inproj",
    )(x2d, mod3, mod3, lw["norm1_g"], lw["w_in"], *rope_tabs,
      lw["gqa"], lw["gka"], lw["gqd"], lw["gkd"], lw["gmat"], lw["gv"], lw["ws_cat"], lw["bs_tab"])


def _na_kernel(*refs, has_window):
    if has_window:
        q_ref, k_ref, v_ref, kc_ref, vc_ref, tab_ref, o_ref = refs
        i = pl.program_id(1)
        u0 = jnp.clip(NA_Q_ROWS * i - NA_WIN_H // 2, 0, k_ref.shape[0] // GRID_W - NA_K_ROWS)
        off = pl.multiple_of(u0 * GRID_W, GRID_W)
        kw = k_ref[pl.ds(off, NA_K_ROWS * GRID_W), :]
        vw = v_ref[pl.ds(off, NA_K_ROWS * GRID_W), :]
    else:
        q_ref, kc_ref, vc_ref, o_ref = refs
    lane_head = lax.broadcasted_iota(jnp.int32, (1, 128), 1) // HEAD_DIM
    for p in range(NA_HEADS // 2):
        cols = slice(128 * p, 128 * p + 128)
        qp = q_ref[:, cols]
        kcp = kc_ref[:, cols]
        vcp = vc_ref[:, cols]
        acc = jnp.zeros((qp.shape[0], 128), F32)
        for j in range(2):
            sel = (lane_head == j).astype(BF16)
            qm = qp * sel
            sc = _dot_nt(qm, kcp)
            m = jnp.max(sc, axis=1, keepdims=True)
            if has_window:
                sw = _dot_nt(qm, kw[:, cols]) + tab_ref[2 * p + j]
                m = jnp.maximum(m, jnp.max(sw, axis=1, keepdims=True))
            ec = jnp.exp(sc - m)
            l = jnp.sum(ec, axis=1, keepdims=True)
            o = _dot(ec.astype(BF16), vcp * sel)
            if has_window:
                ew = jnp.exp(sw - m)
                l = l + jnp.sum(ew, axis=1, keepdims=True)
                o = o + _dot(ew.astype(BF16), vw[:, cols] * sel)
            acc = acc + o / l
        o_ref[:, cols] = acc.astype(BF16)


def _na_window(qa, ka, va, kac, vac, tab, *, batch, n):
    tq = NA_Q_ROWS * GRID_W
    nt = n // tq
    return pl.pallas_call(
        functools.partial(_na_kernel, has_window=True),
        grid=(batch, nt),
        in_specs=[
            pl.BlockSpec((tq, 256), lambda b, i: (b * nt + i, 0)),
            pl.BlockSpec((n, 256), lambda b, i: (b, 0)),
            pl.BlockSpec((n, 256), lambda b, i: (b, 0)),
            pl.BlockSpec((CTX_LEN, 256), lambda b, i: (b, 0)),
            pl.BlockSpec((CTX_LEN, 256), lambda b, i: (b, 0)),
            pl.BlockSpec((None, NA_HEADS, tq, NA_K_ROWS * GRID_W),
                         lambda b, i: (jnp.where(i == 0, 0, jnp.where(i == nt - 1, 2, 1)), 0, 0, 0)),
        ],
        out_specs=pl.BlockSpec((tq, 256), lambda b, i: (b * nt + i, 0)),
        out_shape=jax.ShapeDtypeStruct((batch * n, 256), BF16),
        compiler_params=pltpu.CompilerParams(
            dimension_semantics=("arbitrary", "arbitrary"), vmem_limit_bytes=V7X_VMEM_LIMIT),
        name="na_window",
    )(qa, ka, va, kac, vac, tab)


def _na_dense(qac, kac, vac, *, batch):
    spec = pl.BlockSpec((CTX_LEN, 256), lambda b: (b, 0))
    return pl.pallas_call(
        functools.partial(_na_kernel, has_window=False),
        grid=(batch,),
        in_specs=[spec, spec, spec],
        out_specs=spec,
        out_shape=jax.ShapeDtypeStruct((batch * CTX_LEN, 256), BF16),
        name="na_dense",
    )(qac, kac, vac)


def _da_kernel(*refs, nk, kt, lam_init):
    if nk:
        q_ref, k_ref, vt_ref, kc_ref, vtc_ref, lq1, lk1, lq2, lk2, subg_ref, o_ref = refs[:11]
    else:
        q_ref, kc_ref, vtc_ref, lq1, lk1, lq2, lk2, subg_ref, o_ref = refs[:9]
    kall, vtall, s_buf, acc_ref, m_ref = refs[-5:]
    tq = q_ref.shape[0]
    nblk = (nk + CTX_LEN) // kt

    @pl.when(pl.program_id(2) == 0)
    def _():
        if nk:
            kall[0:nk, :] = k_ref[...]
            vtall[0:DA_V_DIM, 0:nk] = vt_ref[...]
        kall[nk:nk + CTX_LEN, :] = kc_ref[...]
        vtall[0:DA_V_DIM, nk:nk + CTX_LEN] = vtc_ref[...]
        vtall[DA_V_DIM:, :] = jnp.ones((ONES_ROWS, nk + CTX_LEN), BF16)

    q = q_ref[...]
    lane_map = lax.broadcasted_iota(jnp.int32, (1, 128), 1) // HEAD_DIM
    qms = [q * (lane_map == mi).astype(BF16) for mi in range(2)]
    acc_ref[...] = jnp.zeros_like(acc_ref)
    m_ref[...] = jnp.full_like(m_ref, NEG_INF)

    def scores(j, slot):
        kblk = kall[pl.ds(pl.multiple_of(j * kt, kt), kt), :]
        for mi in range(2):
            s_buf[slot, mi] = _dot_nt(kblk, qms[mi])

    def softmax_pv(j, slot):
        vte = vtall[:, pl.ds(pl.multiple_of(j * kt, kt), kt)]
        for mi in range(2):
            for g in range(tq // 256):
                cols = slice(256 * g, 256 * g + 256)
                s = s_buf[slot, mi, :, cols]
                m_old = m_ref[mi, :, cols]
                m_new = jnp.maximum(m_old, jnp.max(s, axis=0, keepdims=True))
                alpha = jnp.exp2(m_old - m_new)
                e = jnp.exp2(s - m_new).astype(BF16)
                acc_ref[mi, :, cols] = acc_ref[mi, :, cols] * alpha + _dot(vte, e)
                m_ref[mi, :, cols] = m_new

    scores(0, 0)
    npairs = (nblk - 1) // 2

    def pair(p, carry):
        j = 2 * p
        scores(j + 1, 1)
        softmax_pv(j, 0)
        scores(j + 2, 0)
        softmax_pv(j + 1, 1)
        return carry

    if npairs:
        lax.fori_loop(0, npairs, pair, 0)
    j = 2 * npairs
    if j < nblk - 1:
        scores(j + 1, 1)
        softmax_pv(j, 0)
        softmax_pv(j + 1, 1)
    else:
        softmax_pv(j, 0)

    lam = (jnp.exp(jnp.sum(lq1[...] * lk1[...], keepdims=True))
           - jnp.exp(jnp.sum(lq2[...] * lk2[...], keepdims=True)) + lam_init)
    a0 = acc_ref[0]
    a1 = acc_ref[1]
    o = a0[:DA_V_DIM] / a0[DA_V_DIM:DA_V_DIM + 1] - lam * (a1[:DA_V_DIM] / a1[DA_V_DIM:DA_V_DIM + 1])
    y = o * lax.rsqrt(jnp.mean(o * o, axis=0, keepdims=True) + EPS) * subg_ref[...] * (1.0 - lam_init)
    o_ref[...] = y.T.astype(BF16)


def _diff_attention(qd, kd, vdt, kdc, vdtc, lw, *, batch, nq, nk, tq, kt, lam_init):
    nt = nq // tq
    nkeys = nk + CTX_LEN
    assert nkeys % kt == 0 and tq % 256 == 0
    in_specs = [pl.BlockSpec((tq, 128), lambda b, h, i: (b * nt + i, h))]
    args = [qd]
    if nk:
        in_specs += [pl.BlockSpec((nk, 128), lambda b, h, i: (b, h)),
                     pl.BlockSpec((None, 128, nk), lambda b, h, i: (b, h, 0))]
        args += [kd, vdt]
    in_specs += [pl.BlockSpec((CTX_LEN, 128), lambda b, h, i: (b, h)),
                 pl.BlockSpec((None, 128, CTX_LEN), lambda b, h, i: (b, h, 0))]
    args += [kdc, vdtc]
    in_specs += [pl.BlockSpec((1, HEAD_DIM), lambda b, h, i: (0, 0))] * 4
    args += [lw["lq1"], lw["lk1"], lw["lq2"], lw["lk2"]]
    in_specs += [pl.BlockSpec((DA_V_DIM, 1), lambda b, h, i: (0, 0))]
    args += [lw["sub_g"]]
    return pl.pallas_call(
        functools.partial(_da_kernel, nk=nk, kt=kt, lam_init=lam_init),
        grid=(batch, DA_HEADS, nt),
        in_specs=in_specs,
        out_specs=pl.BlockSpec((tq, 128), lambda b, h, i: (b * nt + i, h)),
        out_shape=jax.ShapeDtypeStruct((batch * nq, DA_WIDTH), BF16),
        scratch_shapes=[
            pltpu.VMEM((nkeys, 128), BF16),
            pltpu.VMEM((DA_V_DIM + ONES_ROWS, nkeys), BF16),
            pltpu.VMEM((2, 2, kt, tq), F32),
            pltpu.VMEM((2, DA_V_DIM + ONES_ROWS, tq), F32),
            pltpu.VMEM((2, 1, tq), F32),
        ],
        compiler_params=pltpu.CompilerParams(
            dimension_semantics=("arbitrary", "arbitrary", "arbitrary"), vmem_limit_bytes=V7X_VMEM_LIMIT),
        name="diff_attn" if nk else "diff_attn_ctx",
    )(*args)


def _ffn_kernel(x_ref, oa_ref, ob_ref, oc_ref, gate1_ref, shift2_ref, scale2_ref, gate2_ref, g2_ref,
                wout_ref, w1_ref, w3_ref, w2_ref, o_ref):
    mixed = (_dot(oa_ref[...], wout_ref[0:NA_WIDTH, :])
             + _dot(ob_ref[...], wout_ref[NA_WIDTH:NA_WIDTH + GM_WIDTH, :])
             + _dot(oc_ref[...], wout_ref[NA_WIDTH + GM_WIDTH:, :]))
    x1 = x_ref[...] + gate1_ref[0] * mixed
    h = x1 * lax.rsqrt(jnp.mean(x1 * x1, axis=-1, keepdims=True) + EPS) * g2_ref[...]
    hb = (h * (1.0 + scale2_ref[0]) + shift2_ref[0]).astype(BF16)
    a = _dot(hb, w1_ref[...])
    b = _dot(hb, w3_ref[...])
    g = (a * jax.nn.sigmoid(a) * b).astype(BF16)
    o_ref[...] = x1 + gate2_ref[0] * _dot(g, w2_ref[...])


def _out_ffn(x2d, oa, ob, oc, mod3, lw, *, batch, n, tm, per_batch_mod):
    nt = n // tm
    row = (lambda b: b) if per_batch_mod else (lambda b: CTX_MOD_ROW)
    const = lambda b, i: (0, 0)
    tok = lambda b, i: (b * nt + i, 0)
    mod_spec = lambda k: pl.BlockSpec((1, 1, D_MODEL), lambda b, i: (row(b), 0, k))
    return pl.pallas_call(
        _ffn_kernel,
        grid=(batch, nt),
        in_specs=[
            pl.BlockSpec((tm, D_MODEL), tok),
            pl.BlockSpec((tm, NA_WIDTH), tok),
            pl.BlockSpec((tm, GM_WIDTH), tok),
            pl.BlockSpec((tm, DA_WIDTH), tok),
            mod_spec(2), mod_spec(3), mod_spec(4), mod_spec(5),
            pl.BlockSpec((1, D_MODEL), const),
            _resident((D_MODEL, D_MODEL), const),
            _resident((D_MODEL, D_FF), const),
            _resident((D_MODEL, D_FF), const),
            _resident((D_FF, D_MODEL), const),
        ],
        out_specs=pl.BlockSpec((tm, D_MODEL), tok),
        out_shape=jax.ShapeDtypeStruct((batch * n, D_MODEL), F32),
        compiler_params=pltpu.CompilerParams(
            dimension_semantics=("arbitrary", "arbitrary"), vmem_limit_bytes=V7X_VMEM_LIMIT),
        name="out_ffn",
    )(x2d, oa, ob, oc, mod3, mod3, mod3, mod3, lw["norm2_g"], lw["w_out"], lw["w1"], lw["w3"], lw["w2"])


def _rope_tables(n):
    t = jnp.arange(n, dtype=jnp.int32)
    row = (t // GRID_W).astype(F32)
    col = (t % GRID_W).astype(F32)
    half = HEAD_DIM // 2
    inv = ROPE_BASE ** (-jnp.arange(0, half, 2, dtype=F32) / half)
    ar = row[:, None] * inv[None, :]
    ac = col[:, None] * inv[None, :]
    ang = jnp.concatenate([ar, ar, ac, ac], axis=-1)
    cos = jnp.cos(ang)
    sin = jnp.sin(ang)
    first = (np.arange(HEAD_DIM) % 32) < 16
    sa = jnp.where(first, -sin, 0.0)
    sb = jnp.where(first, 0.0, sin)
    return tuple(jnp.tile(a, (1, 2)) for a in (cos, sa, sb))


def _identity_rope_tables(n):
    return (jnp.ones((n, 128), F32), jnp.zeros((n, 128), F32), jnp.zeros((n, 128), F32))


def _na_bias_tables(rpb, rows):
    cq = np.arange(GRID_W)
    c0 = np.clip(cq - NA_WIN_W // 2, 0, GRID_W - NA_WIN_W)
    col_ok = (cq[None, :] >= c0[:, None]) & (cq[None, :] < c0[:, None] + NA_WIN_W)
    coff = np.clip(cq[None, :] - cq[:, None], -(NA_WIN_W - 1), NA_WIN_W - 1) + (NA_WIN_W - 1)
    col_sel = (coff[None] == np.arange(2 * NA_WIN_W - 1)[:, None, None]).astype(np.float32)
    row_sel, ok = [], []
    for r0 in (0, 2 * NA_Q_ROWS, rows - NA_Q_ROWS):
        u0 = min(max(r0 - NA_WIN_H // 2, 0), rows - NA_K_ROWS)
        r = r0 + np.arange(NA_Q_ROWS)
        key_row = u0 + np.arange(NA_K_ROWS)
        start = np.clip(r - NA_WIN_H // 2, 0, rows - NA_WIN_H)
        row_ok = (key_row[None, :] >= start[:, None]) & (key_row[None, :] < start[:, None] + NA_WIN_H)
        roff = key_row[None, :] - r[:, None] + (NA_WIN_H - 1)
        row_sel.append((roff[None] == np.arange(2 * NA_WIN_H - 1)[:, None, None]).astype(np.float32))
        ok.append(row_ok[:, None, :, None] & col_ok[None, :, None, :])
    toeplitz = jnp.einsum("hrc,cqk->hrqk", rpb.astype(F32), jnp.asarray(col_sel),
                          precision=lax.Precision.HIGHEST)
    bias = jnp.einsum("traj,hrqk->thaqjk", jnp.asarray(np.stack(row_sel)), toeplitz,
                      precision=lax.Precision.HIGHEST)
    tab = jnp.where(jnp.asarray(np.stack(ok))[:, None], bias, NEG_INF)
    return tab.reshape(3, NA_HEADS, NA_Q_ROWS * GRID_W, NA_K_ROWS * GRID_W)


def _layer_weights(i, p):
    tile4 = lambda g: jnp.tile(g.astype(F32), 256 // HEAD_DIM).reshape(1, 256)
    blk = np.arange(256) // HEAD_DIM
    return {
        "norm1_g": p["norm1_g"][i].reshape(1, D_MODEL),
        "w_in": p["w_in"][i].astype(BF16),
        "gqa": tile4(p["na_q_g"][i]) * (HEAD_DIM ** -0.5),
        "gka": tile4(p["na_k_g"][i]),
        "gqd": tile4(p["da_q_g"][i]) * (HEAD_DIM ** -0.5 * LOG2E),
        "gkd": tile4(p["da_k_g"][i]),
        "gmat": jnp.asarray(blk[:, None] == blk[None, :], BF16),
        "gv": p["gm_v_g"][i].reshape(1, GM_WIDTH),
        "ws_cat": p["gm_ws"][i].transpose(1, 0, 2).reshape(GM_CHUNK, GM_GROUPS * GM_CHUNK).astype(BF16),
        "bs_tab": jnp.repeat(p["gm_bs"][i].T, GM_WIDTH // GM_GROUPS, axis=1),
        "lq1": p["da_lq1"][i].reshape(1, HEAD_DIM),
        "lk1": p["da_lk1"][i].reshape(1, HEAD_DIM),
        "lq2": p["da_lq2"][i].reshape(1, HEAD_DIM),
        "lk2": p["da_lk2"][i].reshape(1, HEAD_DIM),
        "sub_g": p["da_sub_g"][i].reshape(DA_V_DIM, 1),
        "norm2_g": p["norm2_g"][i].reshape(1, D_MODEL),
        "w_out": p["w_out"][i].astype(BF16),
        "w1": p["ffn_w1"][i].astype(BF16),
        "w3": p["ffn_w3"][i].astype(BF16),
        "w2": p["ffn_w2"][i].astype(BF16),
    }


def kernel(x, c, ctx, c_ctx, w_mod, b_mod, norm1_g, w_in, na_q_g, na_k_g, na_rpb, gm_v_g, gm_ws, gm_bs,
           da_q_g, da_k_g, da_lq1, da_lk1, da_lq2, da_lk2, da_sub_g, w_out, norm2_g, ffn_w1, ffn_w3, ffn_w2):
    batch, n, _ = x.shape
    depth = w_mod.shape[0]
    assert n % (NA_Q_ROWS * GRID_W) == 0 and ctx.shape[1] == CTX_LEN and batch < CTX_MOD_ROW + 1
    params = dict(norm1_g=norm1_g, w_in=w_in, na_q_g=na_q_g, na_k_g=na_k_g, gm_v_g=gm_v_g, gm_ws=gm_ws,
                  gm_bs=gm_bs, da_q_g=da_q_g, da_k_g=da_k_g, da_lq1=da_lq1, da_lk1=da_lk1, da_lq2=da_lq2,
                  da_lk2=da_lk2, da_sub_g=da_sub_g, w_out=w_out, norm2_g=norm2_g, ffn_w1=ffn_w1,
                  ffn_w3=ffn_w3, ffn_w2=ffn_w2)

    c8 = jnp.zeros((MOD_ROWS, D_MODEL), F32).at[:batch].set(c).at[CTX_MOD_ROW].set(c_ctx)
    mod_all = _modulation(c8, w_mod, b_mod)

    rope_lat = _rope_tables(n)
    rope_ctx = _identity_rope_tables(CTX_LEN)
    x2d = x.reshape(batch * n, D_MODEL)
    xc2d = ctx.reshape(batch * CTX_LEN, D_MODEL)

    for i in range(depth):
        lam_init = 0.8 - 0.6 * math.exp(-0.3 * i)
        lw = _layer_weights(i, params)
        mod3 = mod_all[i].reshape(MOD_ROWS, 1, 6 * D_MODEL)
        tab = _na_bias_tables(na_rpb[i], n // GRID_W)

        qa, qd, ka, va, kd, vdt, ob = _inproj(x2d, mod3, lw, rope_lat, batch=batch, n=n, tm=512,
                                              per_batch_mod=True)
        qac, qdc, kac, vac, kdc, vdtc, obc = _inproj(xc2d, mod3, lw, rope_ctx, batch=batch, n=CTX_LEN,
                                                     tm=CTX_LEN, per_batch_mod=False)
        oa = _na_window(qa, ka, va, kac, vac, tab, batch=batch, n=n)
        oc = _diff_attention(qd, kd, vdt, kdc, vdtc, lw, batch=batch, nq=n, nk=n, tq=512, kt=768,
                             lam_init=lam_init)
        x2d = _out_ffn(x2d, oa, ob, oc, mod3, lw, batch=batch, n=n, tm=256, per_batch_mod=True)
        if i < depth - 1:
            oac = _na_dense(qac, kac, vac, batch=batch)
            occ = _diff_attention(qdc, None, None, kdc, vdtc, lw, batch=batch, nq=CTX_LEN, nk=0,
                                  tq=CTX_LEN, kt=CTX_LEN, lam_init=lam_init)
            xc2d = _out_ffn(xc2d, oac, obc, occ, mod3, lw, batch=batch, n=CTX_LEN, tm=CTX_LEN,
                            per_batch_mod=False)
    return x2d.reshape(batch, n, D_MODEL)
```

```python
import functools
import math

import numpy as np
import jax
import jax.numpy as jnp
from jax import lax
from jax.experimental import pallas as pl
from jax.experimental.pallas import tpu as pltpu

D_MODEL = 1024
GRID_W = 64
CTX_LEN = 256
HEAD_DIM = 64
NA_HEADS = 4
NA_WIN_H = 8
NA_WIN_W = 16
NA_WIDTH = NA_HEADS * HEAD_DIM
GM_GROUPS = 4
GM_CHUNK = 128
GM_WIDTH = 256
DA_HEADS = 4
DA_QK_DIM = 2 * HEAD_DIM
DA_V_DIM = 2 * HEAD_DIM
DA_WIDTH = DA_HEADS * DA_V_DIM
QU_WIDTH = NA_WIDTH + DA_HEADS * DA_QK_DIM + 2 * GM_WIDTH
IN_WIDTH = QU_WIDTH + 2 * NA_WIDTH + DA_HEADS * DA_QK_DIM + DA_HEADS * DA_V_DIM
D_FF = -(-8 * D_MODEL // (3 * 256)) * 256
ROPE_BASE = 10000.0
EPS = 1e-6
NEG_INF = -1e30
LOG2E = math.log2(math.e)

COL_QA = 0
COL_QD = NA_WIDTH
COL_UV = COL_QD + DA_HEADS * DA_QK_DIM
COL_KA = QU_WIDTH
COL_VA = COL_KA + NA_WIDTH
COL_KD = COL_VA + NA_WIDTH
COL_VD = COL_KD + DA_HEADS * DA_QK_DIM

MOD_ROWS = 8
CTX_MOD_ROW = 4
NA_Q_ROWS = 4
NA_K_ROWS = 12
ONES_ROWS = 16
V7X_VMEM_LIMIT = 56 * 2 ** 20

F32 = jnp.float32
BF16 = jnp.bfloat16


def _dot(a, b):
    return jnp.dot(a, b, preferred_element_type=F32)


def _dot_nt(a, b):
    return lax.dot_general(a, b, (((1,), (1,)), ((), ())), preferred_element_type=F32)


def _resident(shape, index_map):
    return pl.BlockSpec(shape, index_map, pipeline_mode=pl.Buffered(1))


def _mod_kernel(c_ref, w_ref, b_ref, o_ref):
    c = c_ref[...]
    a = c * jax.nn.sigmoid(c)
    a_hi = a.astype(BF16)
    a_lo = (a - a_hi.astype(F32)).astype(BF16)
    w = w_ref[...]
    w_hi = w.astype(BF16)
    w_lo = (w - w_hi.astype(F32)).astype(BF16)
    o_ref[...] = _dot(a_hi, w_hi) + _dot(a_lo, w_hi) + _dot(a_hi, w_lo) + b_ref[...]


def _modulation(c8, w_mod, b_mod):
    depth = w_mod.shape[0]
    tn = 1024
    return pl.pallas_call(
        _mod_kernel,
        grid=(depth, 6 * D_MODEL // tn),
        in_specs=[
            pl.BlockSpec((MOD_ROWS, D_MODEL), lambda l, j: (0, 0)),
            pl.BlockSpec((None, D_MODEL, tn), lambda l, j: (l, 0, j)),
            pl.BlockSpec((None, 1, tn), lambda l, j: (l, 0, j)),
        ],
        out_specs=pl.BlockSpec((None, MOD_ROWS, tn), lambda l, j: (l, 0, j)),
        out_shape=jax.ShapeDtypeStruct((depth, MOD_ROWS, 6 * D_MODEL), F32),
        name="adaln_mod",
    )(c8, w_mod, b_mod.reshape(depth, 1, 6 * D_MODEL))


def _inproj_kernel(x_ref, shift_ref, scale_ref, g1_ref, w_ref, cos_ref, sa_ref, sb_ref,
                   gqa_ref, gka_ref, gqd_ref, gkd_ref, gmat_ref, gv_ref, ws_ref, bs_ref,
                   qa_ref, qd_ref, ka_ref, va_ref, kd_ref, vdt_ref, ob_ref, *, tm):
    x = x_ref[...]
    h = x * lax.rsqrt(jnp.mean(x * x, axis=-1, keepdims=True) + EPS) * g1_ref[...]
    h = h * (1.0 + scale_ref[0]) + shift_ref[0]
    hb = h.astype(BF16)
    gmat = gmat_ref[...]
    cos = cos_ref[...]
    sa = sa_ref[...]
    sb = sb_ref[...]

    def proj(c0, width):
        return _dot(hb, w_ref[:, c0:c0 + width])

    def head_norm(y, g):
        ss = _dot((y * y).astype(BF16), gmat)
        return y * lax.rsqrt(ss * (1.0 / HEAD_DIM) + EPS) * g

    def rope(z):
        return z * cos + pltpu.roll(z, 128 - 16, 1) * sa + pltpu.roll(z, 16, 1) * sb

    qa_ref[...] = head_norm(proj(COL_QA, 256), gqa_ref[...]).astype(BF16)
    ka_ref[...] = head_norm(proj(COL_KA, 256), gka_ref[...]).astype(BF16)
    va_ref[...] = proj(COL_VA, 256).astype(BF16)
    for c in range(2):
        yq = head_norm(proj(COL_QD + 256 * c, 256), gqd_ref[...])
        yk = head_norm(proj(COL_KD + 256 * c, 256), gkd_ref[...])
        for t in range(2):
            lo = 256 * c + 128 * t
            qd_ref[:, lo:lo + 128] = rope(yq[:, 128 * t:128 * t + 128]).astype(BF16)
            kd_ref[:, lo:lo + 128] = rope(yk[:, 128 * t:128 * t + 128]).astype(BF16)
    vdt_ref[...] = proj(COL_VD, 512).T.astype(BF16)

    z = jax.nn.gelu(proj(COL_UV, 2 * GM_WIDTH))
    u = z[:, :GM_WIDTH]
    v = z[:, GM_WIDTH:]
    v = v * lax.rsqrt(jnp.mean(v * v, axis=-1, keepdims=True) + EPS) * gv_ref[...]
    vb = v.astype(BF16)
    group = lax.broadcasted_iota(jnp.int32, (GM_CHUNK, GM_WIDTH), 1) // (GM_WIDTH // GM_GROUPS)
    ws = ws_ref[...]
    bs = bs_ref[...]
    for c in range(tm // GM_CHUNK):
        vc = vb[c * GM_CHUNK:(c + 1) * GM_CHUNK, :]
        vbd = jnp.concatenate([jnp.where(group == g, vc, jnp.zeros_like(vc)) for g in range(GM_GROUPS)], axis=0)
        s = _dot(ws, vbd) + bs
        ob_ref[c * GM_CHUNK:(c + 1) * GM_CHUNK, :] = (u[c * GM_CHUNK:(c + 1) * GM_CHUNK, :] * s).astype(BF16)


def _inproj(x2d, mod3, lw, rope_tabs, *, batch, n, tm, per_batch_mod):
    nt = n // tm
    t_tot = batch * n
    row = (lambda b: b) if per_batch_mod else (lambda b: CTX_MOD_ROW)
    const = lambda b, i: (0, 0)
    tok = lambda b, i: (b * nt + i, 0)
    in_specs = [
        pl.BlockSpec((tm, D_MODEL), tok),
        pl.BlockSpec((1, 1, D_MODEL), lambda b, i: (row(b), 0, 0)),
        pl.BlockSpec((1, 1, D_MODEL), lambda b, i: (row(b), 0, 1)),
        pl.BlockSpec((1, D_MODEL), const),
        _resident((D_MODEL, IN_WIDTH), const),
        pl.BlockSpec((tm, 128), lambda b, i: (i, 0)),
        pl.BlockSpec((tm, 128), lambda b, i: (i, 0)),
        pl.BlockSpec((tm, 128), lambda b, i: (i, 0)),
        pl.BlockSpec((1, 256), const),
        pl.BlockSpec((1, 256), const),
        pl.BlockSpec((1, 256), const),
        pl.BlockSpec((1, 256), const),
        pl.BlockSpec((256, 256), const),
        pl.BlockSpec((1, GM_WIDTH), const),
        pl.BlockSpec((GM_CHUNK, GM_GROUPS * GM_CHUNK), const),
        pl.BlockSpec((GM_CHUNK, GM_WIDTH), const),
    ]
    out_specs = [
        pl.BlockSpec((tm, 256), tok),
        pl.BlockSpec((tm, 512), tok),
        pl.BlockSpec((tm, 256), tok),
        pl.BlockSpec((tm, 256), tok),
        pl.BlockSpec((tm, 512), tok),
        pl.BlockSpec((None, 512, tm), lambda b, i: (b, 0, i)),
        pl.BlockSpec((tm, 256), tok),
    ]
    out_shape = [
        jax.ShapeDtypeStruct((t_tot, 256), BF16),
        jax.ShapeDtypeStruct((t_tot, 512), BF16),
        jax.ShapeDtypeStruct((t_tot, 256), BF16),
        jax.ShapeDtypeStruct((t_tot, 256), BF16),
        jax.ShapeDtypeStruct((t_tot, 512), BF16),
        jax.ShapeDtypeStruct((batch, 512, n), BF16),
        jax.ShapeDtypeStruct((t_tot, 256), BF16),
    ]
    return pl.pallas_call(
        functools.partial(_inproj_kernel, tm=tm),
        grid=(batch, nt),
        in_specs=in_specs,
        out_specs=out_specs,
        out_shape=out_shape,
        compiler_params=pltpu.CompilerParams(
            dimension_semantics=("arbitrary", "arbitrary"), vmem_limit_bytes=V7X_VMEM_LIMIT),
        name="inproj",
    )(x2d, mod3, mod3, lw["norm1_g"], lw["w_in"], *rope_tabs,
      lw["gqa"], lw["gka"], lw["gqd"], lw["gkd"], lw["gmat"], lw["gv"], lw["ws_cat"], lw["bs_tab"])


def _na_kernel(*refs, has_window):
    if has_window:
        q_ref, k_ref, v_ref, kc_ref, vc_ref, tab_ref, o_ref = refs
        i = pl.program_id(1)
        u0 = jnp.clip(NA_Q_ROWS * i - NA_WIN_H // 2, 0, k_ref.shape[0] // GRID_W - NA_K_ROWS)
        off = pl.multiple_of(u0 * GRID_W, GRID_W)
        kw = k_ref[pl.ds(off, NA_K_ROWS * GRID_W), :]
        vw = v_ref[pl.ds(off, NA_K_ROWS * GRID_W), :]
    else:
        q_ref, kc_ref, vc_ref, o_ref = refs
    lane_head = lax.broadcasted_iota(jnp.int32, (1, 128), 1) // HEAD_DIM
    for p in range(NA_HEADS // 2):
        cols = slice(128 * p, 128 * p + 128)
        qp = q_ref[:, cols]
        kcp = kc_ref[:, cols]
        vcp = vc_ref[:, cols]
        acc = jnp.zeros((qp.shape[0], 128), F32)
        for j in range(2):
            sel = (lane_head == j).astype(BF16)
            qm = qp * sel
            sc = _dot_nt(qm, kcp)
            m = jnp.max(sc, axis=1, keepdims=True)
            if has_window:
                sw = _dot_nt(qm, kw[:, cols]) + tab_ref[2 * p + j]
                m = jnp.maximum(m, jnp.max(sw, axis=1, keepdims=True))
            ec = jnp.exp(sc - m)
            l = jnp.sum(ec, axis=1, keepdims=True)
            o = _dot(ec.astype(BF16), vcp * sel)
            if has_window:
                ew = jnp.exp(sw - m)
                l = l + jnp.sum(ew, axis=1, keepdims=True)
                o = o + _dot(ew.astype(BF16), vw[:, cols] * sel)
            acc = acc + o / l
        o_ref[:, cols] = acc.astype(BF16)


def _na_window(qa, ka, va, kac, vac, tab, *, batch, n):
    tq = NA_Q_ROWS * GRID_W
    nt = n // tq
    return pl.pallas_call(
        functools.partial(_na_kernel, has_window=True),
        grid=(batch, nt),
        in_specs=[
            pl.BlockSpec((tq, 256), lambda b, i: (b * nt + i, 0)),
            pl.BlockSpec((n, 256), lambda b, i: (b, 0)),
            pl.BlockSpec((n, 256), lambda b, i: (b, 0)),
            pl.BlockSpec((CTX_LEN, 256), lambda b, i: (b, 0)),
            pl.BlockSpec((CTX_LEN, 256), lambda b, i: (b, 0)),
            pl.BlockSpec((None, NA_HEADS, tq, NA_K_ROWS * GRID_W),
                         lambda b, i: (jnp.where(i == 0, 0, jnp.where(i == nt - 1, 2, 1)), 0, 0, 0)),
        ],
        out_specs=pl.BlockSpec((tq, 256), lambda b, i: (b * nt + i, 0)),
        out_shape=jax.ShapeDtypeStruct((batch * n, 256), BF16),
        compiler_params=pltpu.CompilerParams(
            dimension_semantics=("arbitrary", "arbitrary"), vmem_limit_bytes=V7X_VMEM_LIMIT),
        name="na_window",
    )(qa, ka, va, kac, vac, tab)


def _na_dense(qac, kac, vac, *, batch):
    spec = pl.BlockSpec((CTX_LEN, 256), lambda b: (b, 0))
    return pl.pallas_call(
        functools.partial(_na_kernel, has_window=False),
        grid=(batch,),
        in_specs=[spec, spec, spec],
        out_specs=spec,
        out_shape=jax.ShapeDtypeStruct((batch * CTX_LEN, 256), BF16),
        name="na_dense",
    )(qac, kac, vac)


def _da_kernel(*refs, nk, kt, lam_init):
    if nk:
        q_ref, k_ref, vt_ref, kc_ref, vtc_ref, lq1, lk1, lq2, lk2, subg_ref, o_ref = refs[:11]
    else:
        q_ref, kc_ref, vtc_ref, lq1, lk1, lq2, lk2, subg_ref, o_ref = refs[:9]
    kall, vtall, s_buf, bm_buf, acc_ref, m_ref = refs[-6:]
    tq = o_ref.shape[0]
    nt = q_ref.shape[0] // tq
    nblk = (nk + CTX_LEN) // kt
    i = pl.program_id(2)
    lane_map = lax.broadcasted_iota(jnp.int32, (1, 128), 1) // HEAD_DIM
    sel = [(lane_map == mi).astype(BF16) for mi in range(2)]

    def scores(tile, blk, slot):
        q = q_ref[pl.ds(pl.multiple_of(tile * tq, tq), tq), :]
        kblk = kall[pl.ds(pl.multiple_of(blk * kt, kt), kt), :]
        for mi in range(2):
            s = _dot_nt(kblk, q * sel[mi])
            s_buf[slot, mi] = s
            bm_buf[slot, mi] = jnp.max(s, axis=0, keepdims=True)

    @pl.when(i == 0)
    def _():
        if nk:
            kall[0:nk, :] = k_ref[...]
            vtall[0:DA_V_DIM, 0:nk] = vt_ref[...]
        kall[nk:nk + CTX_LEN, :] = kc_ref[...]
        vtall[0:DA_V_DIM, nk:nk + CTX_LEN] = vtc_ref[...]
        vtall[DA_V_DIM:, :] = jnp.ones((ONES_ROWS, nk + CTX_LEN), BF16)
        scores(0, 0, 0)

    acc_ref[...] = jnp.zeros_like(acc_ref)
    m_ref[...] = jnp.full_like(m_ref, NEG_INF)

    def softmax_pv(blk, slot):
        vte = vtall[:, pl.ds(pl.multiple_of(blk * kt, kt), kt)]
        for mi in range(2):
            for g in range(tq // 256):
                cols = slice(256 * g, 256 * g + 256)
                m_old = m_ref[mi, :, cols]
                m_new = jnp.maximum(m_old, bm_buf[slot, mi, :, cols])
                alpha = jnp.exp2(m_old - m_new)
                e = jnp.exp2(s_buf[slot, mi, :, cols] - m_new).astype(BF16)
                acc_ref[mi, :, cols] = acc_ref[mi, :, cols] * alpha + _dot(vte, e)
                m_ref[mi, :, cols] = m_new

    if nblk == 1:
        softmax_pv(0, 0)
    else:
        def pair(p, carry):
            blk = 2 * p
            scores(i, blk + 1, 1)
            softmax_pv(blk, 0)
            wrap = blk + 2 == nblk
            scores(jnp.where(wrap, jnp.minimum(i + 1, nt - 1), i), jnp.where(wrap, 0, blk + 2), 0)
            softmax_pv(blk + 1, 1)
            return carry

        lax.fori_loop(0, nblk // 2, pair, 0)

    lam = (jnp.exp(jnp.sum(lq1[...] * lk1[...], keepdims=True))
           - jnp.exp(jnp.sum(lq2[...] * lk2[...], keepdims=True)) + lam_init)
    a0 = acc_ref[0]
    a1 = acc_ref[1]
    o = a0[:DA_V_DIM] / a0[DA_V_DIM:DA_V_DIM + 1] - lam * (a1[:DA_V_DIM] / a1[DA_V_DIM:DA_V_DIM + 1])
    y = o * lax.rsqrt(jnp.mean(o * o, axis=0, keepdims=True) + EPS) * subg_ref[...] * (1.0 - lam_init)
    o_ref[...] = y.T.astype(BF16)


def _diff_attention(qd, kd, vdt, kdc, vdtc, lw, *, batch, nq, nk, tq, kt, lam_init):
    nt = nq // tq
    nkeys = nk + CTX_LEN
    nblk = nkeys // kt
    assert nkeys % kt == 0 and tq % 256 == 0 and nq % tq == 0 and (nblk == 1 and nt == 1 or nblk % 2 == 0)
    in_specs = [pl.BlockSpec((nq, 128), lambda b, h, i: (b, h))]
    args = [qd]
    if nk:
        in_specs += [pl.BlockSpec((nk, 128), lambda b, h, i: (b, h)),
                     pl.BlockSpec((None, 128, nk), lambda b, h, i: (b, h, 0))]
        args += [kd, vdt]
    in_specs += [pl.BlockSpec((CTX_LEN, 128), lambda b, h, i: (b, h)),
                 pl.BlockSpec((None, 128, CTX_LEN), lambda b, h, i: (b, h, 0))]
    args += [kdc, vdtc]
    in_specs += [pl.BlockSpec((1, HEAD_DIM), lambda b, h, i: (0, 0))] * 4
    args += [lw["lq1"], lw["lk1"], lw["lq2"], lw["lk2"]]
    in_specs += [pl.BlockSpec((DA_V_DIM, 1), lambda b, h, i: (0, 0))]
    args += [lw["sub_g"]]
    return pl.pallas_call(
        functools.partial(_da_kernel, nk=nk, kt=kt, lam_init=lam_init),
        grid=(batch, DA_HEADS, nt),
        in_specs=in_specs,
        out_specs=pl.BlockSpec((tq, 128), lambda b, h, i: (b * nt + i, h)),
        out_shape=jax.ShapeDtypeStruct((batch * nq, DA_WIDTH), BF16),
        scratch_shapes=[
            pltpu.VMEM((nkeys, 128), BF16),
            pltpu.VMEM((DA_V_DIM + ONES_ROWS, nkeys), BF16),
            pltpu.VMEM((2, 2, kt, tq), F32),
            pltpu.VMEM((2, 2, 1, tq), F32),
            pltpu.VMEM((2, DA_V_DIM + ONES_ROWS, tq), F32),
            pltpu.VMEM((2, 1, tq), F32),
        ],
        compiler_params=pltpu.CompilerParams(
            dimension_semantics=("arbitrary", "arbitrary", "arbitrary"), vmem_limit_bytes=V7X_VMEM_LIMIT),
        name="diff_attn" if nk else "diff_attn_ctx",
    )(*args)


def _ffn_kernel(x_ref, oa_ref, ob_ref, oc_ref, gate1_ref, shift2_ref, scale2_ref, gate2_ref, g2_ref,
                wout_ref, w1_ref, w3_ref, w2_ref, o_ref):
    mixed = (_dot(oa_ref[...], wout_ref[0:NA_WIDTH, :])
             + _dot(ob_ref[...], wout_ref[NA_WIDTH:NA_WIDTH + GM_WIDTH, :])
             + _dot(oc_ref[...], wout_ref[NA_WIDTH + GM_WIDTH:, :]))
    x1 = x_ref[...] + gate1_ref[0] * mixed
    h = x1 * lax.rsqrt(jnp.mean(x1 * x1, axis=-1, keepdims=True) + EPS) * g2_ref[...]
    hb = (h * (1.0 + scale2_ref[0]) + shift2_ref[0]).astype(BF16)
    a = _dot(hb, w1_ref[...])
    b = _dot(hb, w3_ref[...])
    g = (a * jax.nn.sigmoid(a) * b).astype(BF16)
    o_ref[...] = x1 + gate2_ref[0] * _dot(g, w2_ref[...])


def _out_ffn(x2d, oa, ob, oc, mod3, lw, *, batch, n, tm, per_batch_mod):
    nt = n // tm
    row = (lambda b: b) if per_batch_mod else (lambda b: CTX_MOD_ROW)
    const = lambda b, i: (0, 0)
    tok = lambda b, i: (b * nt + i, 0)
    mod_spec = lambda k: pl.BlockSpec((1, 1, D_MODEL), lambda b, i: (row(b), 0, k))
    return pl.pallas_call(
        _ffn_kernel,
        grid=(batch, nt),
        in_specs=[
            pl.BlockSpec((tm, D_MODEL), tok),
            pl.BlockSpec((tm, NA_WIDTH), tok),
            pl.BlockSpec((tm, GM_WIDTH), tok),
            pl.BlockSpec((tm, DA_WIDTH), tok),
            mod_spec(2), mod_spec(3), mod_spec(4), mod_spec(5),
            pl.BlockSpec((1, D_MODEL), const),
            _resident((D_MODEL, D_MODEL), const),
            _resident((D_MODEL, D_FF), const),
            _resident((D_MODEL, D_FF), const),
            _resident((D_FF, D_MODEL), const),
        ],
        out_specs=pl.BlockSpec((tm, D_MODEL), tok),
        out_shape=jax.ShapeDtypeStruct((batch * n, D_MODEL), F32),
        compiler_params=pltpu.CompilerParams(
            dimension_semantics=("arbitrary", "arbitrary"), vmem_limit_bytes=V7X_VMEM_LIMIT),
        name="out_ffn",
    )(x2d, oa, ob, oc, mod3, mod3, mod3, mod3, lw["norm2_g"], lw["w_out"], lw["w1"], lw["w3"], lw["w2"])


def _rope_tables(n):
    t = jnp.arange(n, dtype=jnp.int32)
    row = (t // GRID_W).astype(F32)
    col = (t % GRID_W).astype(F32)
    half = HEAD_DIM // 2
    inv = ROPE_BASE ** (-jnp.arange(0, half, 2, dtype=F32) / half)
    ar = row[:, None] * inv[None, :]
    ac = col[:, None] * inv[None, :]
    ang = jnp.concatenate([ar, ar, ac, ac], axis=-1)
    cos = jnp.cos(ang)
    sin = jnp.sin(ang)
    first = (np.arange(HEAD_DIM) % 32) < 16
    sa = jnp.where(first, -sin, 0.0)
    sb = jnp.where(first, 0.0, sin)
    return tuple(jnp.tile(a, (1, 2)) for a in (cos, sa, sb))


def _identity_rope_tables(n):
    return (jnp.ones((n, 128), F32), jnp.zeros((n, 128), F32), jnp.zeros((n, 128), F32))


def _na_bias_tables(rpb, rows):
    cq = np.arange(GRID_W)
    c0 = np.clip(cq - NA_WIN_W // 2, 0, GRID_W - NA_WIN_W)
    col_ok = (cq[None, :] >= c0[:, None]) & (cq[None, :] < c0[:, None] + NA_WIN_W)
    coff = np.clip(cq[None, :] - cq[:, None], -(NA_WIN_W - 1), NA_WIN_W - 1) + (NA_WIN_W - 1)
    col_sel = (coff[None] == np.arange(2 * NA_WIN_W - 1)[:, None, None]).astype(np.float32)
    row_sel, ok = [], []
    for r0 in (0, 2 * NA_Q_ROWS, rows - NA_Q_ROWS):
        u0 = min(max(r0 - NA_WIN_H // 2, 0), rows - NA_K_ROWS)
        r = r0 + np.arange(NA_Q_ROWS)
        key_row = u0 + np.arange(NA_K_ROWS)
        start = np.clip(r - NA_WIN_H // 2, 0, rows - NA_WIN_H)
        row_ok = (key_row[None, :] >= start[:, None]) & (key_row[None, :] < start[:, None] + NA_WIN_H)
        roff = key_row[None, :] - r[:, None] + (NA_WIN_H - 1)
        row_sel.append((roff[None] == np.arange(2 * NA_WIN_H - 1)[:, None, None]).astype(np.float32))
        ok.append(row_ok[:, None, :, None] & col_ok[None, :, None, :])
    toeplitz = jnp.einsum("hrc,cqk->hrqk", rpb.astype(F32), jnp.asarray(col_sel),
                          precision=lax.Precision.HIGHEST)
    bias = jnp.einsum("traj,hrqk->thaqjk", jnp.asarray(np.stack(row_sel)), toeplitz,
                      precision=lax.Precision.HIGHEST)
    tab = jnp.where(jnp.asarray(np.stack(ok))[:, None], bias, NEG_INF)
    return tab.reshape(3, NA_HEADS, NA_Q_ROWS * GRID_W, NA_K_ROWS * GRID_W)


def _layer_weights(i, p):
    tile4 = lambda g: jnp.tile(g.astype(F32), 256 // HEAD_DIM).reshape(1, 256)
    blk = np.arange(256) // HEAD_DIM
    return {
        "norm1_g": p["norm1_g"][i].reshape(1, D_MODEL),
        "w_in": p["w_in"][i].astype(BF16),
        "gqa": tile4(p["na_q_g"][i]) * (HEAD_DIM ** -0.5),
        "gka": tile4(p["na_k_g"][i]),
        "gqd": tile4(p["da_q_g"][i]) * (HEAD_DIM ** -0.5 * LOG2E),
        "gkd": tile4(p["da_k_g"][i]),
        "gmat": jnp.asarray(blk[:, None] == blk[None, :], BF16),
        "gv": p["gm_v_g"][i].reshape(1, GM_WIDTH),
        "ws_cat": p["gm_ws"][i].transpose(1, 0, 2).reshape(GM_CHUNK, GM_GROUPS * GM_CHUNK).astype(BF16),
        "bs_tab": jnp.repeat(p["gm_bs"][i].T, GM_WIDTH // GM_GROUPS, axis=1),
        "lq1": p["da_lq1"][i].reshape(1, HEAD_DIM),
        "lk1": p["da_lk1"][i].reshape(1, HEAD_DIM),
        "lq2": p["da_lq2"][i].reshape(1, HEAD_DIM),
        "lk2": p["da_lk2"][i].reshape(1, HEAD_DIM),
        "sub_g": p["da_sub_g"][i].reshape(DA_V_DIM, 1),
        "norm2_g": p["norm2_g"][i].reshape(1, D_MODEL),
        "w_out": p["w_out"][i].astype(BF16),
        "w1": p["ffn_w1"][i].astype(BF16),
        "w3": p["ffn_w3"][i].astype(BF16),
        "w2": p["ffn_w2"][i].astype(BF16),
    }


def kernel(x, c, ctx, c_ctx, w_mod, b_mod, norm1_g, w_in, na_q_g, na_k_g, na_rpb, gm_v_g, gm_ws, gm_bs,
           da_q_g, da_k_g, da_lq1, da_lk1, da_lq2, da_lk2, da_sub_g, w_out, norm2_g, ffn_w1, ffn_w3, ffn_w2):
    batch, n, _ = x.shape
    depth = w_mod.shape[0]
    assert n % (NA_Q_ROWS * GRID_W) == 0 and ctx.shape[1] == CTX_LEN and batch < CTX_MOD_ROW + 1
    params = dict(norm1_g=norm1_g, w_in=w_in, na_q_g=na_q_g, na_k_g=na_k_g, gm_v_g=gm_v_g, gm_ws=gm_ws,
                  gm_bs=gm_bs, da_q_g=da_q_g, da_k_g=da_k_g, da_lq1=da_lq1, da_lk1=da_lk1, da_lq2=da_lq2,
                  da_lk2=da_lk2, da_sub_g=da_sub_g, w_out=w_out, norm2_g=norm2_g, ffn_w1=ffn_w1,
                  ffn_w3=ffn_w3, ffn_w2=ffn_w2)

    c8 = jnp.zeros((MOD_ROWS, D_MODEL), F32).at[:batch].set(c).at[CTX_MOD_ROW].set(c_ctx)
    mod_all = _modulation(c8, w_mod, b_mod)

    rope_lat = _rope_tables(n)
    rope_ctx = _identity_rope_tables(CTX_LEN)
    x2d = x.reshape(batch * n, D_MODEL)
    xc2d = ctx.reshape(batch * CTX_LEN, D_MODEL)

    for i in range(depth):
        lam_init = 0.8 - 0.6 * math.exp(-0.3 * i)
        lw = _layer_weights(i, params)
        mod3 = mod_all[i].reshape(MOD_ROWS, 1, 6 * D_MODEL)
        tab = _na_bias_tables(na_rpb[i], n // GRID_W)

        qa, qd, ka, va, kd, vdt, ob = _inproj(x2d, mod3, lw, rope_lat, batch=batch, n=n, tm=512,
                                              per_batch_mod=True)
        qac, qdc, kac, vac, kdc, vdtc, obc = _inproj(xc2d, mod3, lw, rope_ctx, batch=batch, n=CTX_LEN,
                                                     tm=CTX_LEN, per_batch_mod=False)
        oa = _na_window(qa, ka, va, kac, vac, tab, batch=batch, n=n)
        oc = _diff_attention(qd, kd, vdt, kdc, vdtc, lw, batch=batch, nq=n, nk=n, tq=512, kt=1408,
                             lam_init=lam_init)
        x2d = _out_ffn(x2d, oa, ob, oc, mod3, lw, batch=batch, n=n, tm=512, per_batch_mod=True)
        if i < depth - 1:
            oac = _na_dense(qac, kac, vac, batch=batch)
            occ = _diff_attention(qdc, None, None, kdc, vdtc, lw, batch=batch, nq=CTX_LEN, nk=0,
                                  tq=CTX_LEN, kt=CTX_LEN, lam_init=lam_init)
            xc2d = _out_ffn(xc2d, oac, obc, occ, mod3, lw, batch=batch, n=CTX_LEN, tm=CTX_LEN,
                            per_batch_mod=False)
    return x2d.reshape(batch, n, D_MODEL)
```

```python
import functools
import math

import numpy as np
import jax
import jax.numpy as jnp
from jax import lax
from jax.experimental import pallas as pl
from jax.experimental.pallas import tpu as pltpu

D_MODEL = 1024
GRID_W = 64
CTX_LEN = 256
HEAD_DIM = 64
NA_HEADS = 4
NA_WIN_H = 8
NA_WIN_W = 16
NA_WIDTH = NA_HEADS * HEAD_DIM
GM_GROUPS = 4
GM_CHUNK = 128
GM_WIDTH = 256
DA_HEADS = 4
DA_QK_DIM = 2 * HEAD_DIM
DA_V_DIM = 2 * HEAD_DIM
DA_WIDTH = DA_HEADS * DA_V_DIM
QU_WIDTH = NA_WIDTH + DA_HEADS * DA_QK_DIM + 2 * GM_WIDTH
IN_WIDTH = QU_WIDTH + 2 * NA_WIDTH + DA_HEADS * DA_QK_DIM + DA_HEADS * DA_V_DIM
D_FF = -(-8 * D_MODEL // (3 * 256)) * 256
ROPE_BASE = 10000.0
EPS = 1e-6
NEG_INF = -1e30
LOG2E = math.log2(math.e)

COL_QA = 0
COL_QD = NA_WIDTH
COL_UV = COL_QD + DA_HEADS * DA_QK_DIM
COL_KA = QU_WIDTH
COL_VA = COL_KA + NA_WIDTH
COL_KD = COL_VA + NA_WIDTH
COL_VD = COL_KD + DA_HEADS * DA_QK_DIM

MOD_ROWS = 8
CTX_MOD_ROW = 4
NA_Q_ROWS = 4
NA_K_ROWS = 12
ONES_ROWS = 16
V7X_VMEM_LIMIT = 56 * 2 ** 20

F32 = jnp.float32
BF16 = jnp.bfloat16


def _dot(a, b):
    return jnp.dot(a, b, preferred_element_type=F32)


def _dot_nt(a, b):
    return lax.dot_general(a, b, (((1,), (1,)), ((), ())), preferred_element_type=F32)


def _resident(shape, index_map):
    return pl.BlockSpec(shape, index_map, pipeline_mode=pl.Buffered(1))


def _mod_kernel(c_ref, w_ref, b_ref, o_ref):
    c = c_ref[...]
    a = c * jax.nn.sigmoid(c)
    a_hi = a.astype(BF16)
    a_lo = (a - a_hi.astype(F32)).astype(BF16)
    w = w_ref[...]
    w_hi = w.astype(BF16)
    w_lo = (w - w_hi.astype(F32)).astype(BF16)
    o_ref[...] = _dot(a_hi, w_hi) + _dot(a_lo, w_hi) + _dot(a_hi, w_lo) + b_ref[...]


def _modulation(c8, w_mod, b_mod):
    depth = w_mod.shape[0]
    tn = 1024
    return pl.pallas_call(
        _mod_kernel,
        grid=(depth, 6 * D_MODEL // tn),
        in_specs=[
            pl.BlockSpec((MOD_ROWS, D_MODEL), lambda l, j: (0, 0)),
            pl.BlockSpec((None, D_MODEL, tn), lambda l, j: (l, 0, j)),
            pl.BlockSpec((None, 1, tn), lambda l, j: (l, 0, j)),
        ],
        out_specs=pl.BlockSpec((None, MOD_ROWS, tn), lambda l, j: (l, 0, j)),
        out_shape=jax.ShapeDtypeStruct((depth, MOD_ROWS, 6 * D_MODEL), F32),
        name="adaln_mod",
    )(c8, w_mod, b_mod.reshape(depth, 1, 6 * D_MODEL))


def _inproj_kernel(x_ref, shift_ref, scale_ref, g1_ref, w_ref, cos_ref, sa_ref, sb_ref,
                   gqa_ref, gka_ref, gqd_ref, gkd_ref, gmat_ref, gv_ref, ws_ref, bs_ref,
                   qa_ref, qd_ref, ka_ref, va_ref, kd_ref, vdt_ref, ob_ref, *, tm, sub):
    gmat = gmat_ref[...]
    group = lax.broadcasted_iota(jnp.int32, (GM_CHUNK, GM_WIDTH), 1) // (GM_WIDTH // GM_GROUPS)
    ws = ws_ref[...]
    bs = bs_ref[...]

    def head_norm(y, g):
        ss = _dot((y * y).astype(BF16), gmat)
        return y * lax.rsqrt(ss * (1.0 / HEAD_DIM) + EPS) * g

    for r0 in range(0, tm, sub):
        rows = slice(r0, r0 + sub)
        x = x_ref[rows, :]
        h = x * lax.rsqrt(jnp.mean(x * x, axis=-1, keepdims=True) + EPS) * g1_ref[...]
        hb = (h * (1.0 + scale_ref[0]) + shift_ref[0]).astype(BF16)
        p = _dot(hb, w_ref[...])
        cos = cos_ref[rows, :]
        sa = sa_ref[rows, :]
        sb = sb_ref[rows, :]

        def rope(z):
            return z * cos + pltpu.roll(z, 128 - 16, 1) * sa + pltpu.roll(z, 16, 1) * sb

        qa_ref[rows, :] = head_norm(p[:, COL_QA:COL_QA + 256], gqa_ref[...]).astype(BF16)
        ka_ref[rows, :] = head_norm(p[:, COL_KA:COL_KA + 256], gka_ref[...]).astype(BF16)
        va_ref[rows, :] = p[:, COL_VA:COL_VA + 256].astype(BF16)
        for c in range(2):
            yq = head_norm(p[:, COL_QD + 256 * c:COL_QD + 256 * c + 256], gqd_ref[...])
            yk = head_norm(p[:, COL_KD + 256 * c:COL_KD + 256 * c + 256], gkd_ref[...])
            for t in range(2):
                lo = 256 * c + 128 * t
                qd_ref[rows, lo:lo + 128] = rope(yq[:, 128 * t:128 * t + 128]).astype(BF16)
                kd_ref[rows, lo:lo + 128] = rope(yk[:, 128 * t:128 * t + 128]).astype(BF16)
        vdt_ref[:, rows] = p[:, COL_VD:COL_VD + 512].T.astype(BF16)

        z = jax.nn.gelu(p[:, COL_UV:COL_UV + 2 * GM_WIDTH])
        u = z[:, :GM_WIDTH]
        v = z[:, GM_WIDTH:]
        v = v * lax.rsqrt(jnp.mean(v * v, axis=-1, keepdims=True) + EPS) * gv_ref[...]
        vb = v.astype(BF16)
        for c in range(sub // GM_CHUNK):
            vc = vb[c * GM_CHUNK:(c + 1) * GM_CHUNK, :]
            vbd = jnp.concatenate([jnp.where(group == g, vc, jnp.zeros_like(vc)) for g in range(GM_GROUPS)],
                                  axis=0)
            s = _dot(ws, vbd) + bs
            lo = r0 + c * GM_CHUNK
            ob_ref[lo:lo + GM_CHUNK, :] = (u[c * GM_CHUNK:(c + 1) * GM_CHUNK, :] * s).astype(BF16)


def _inproj(x2d, mod3, lw, rope_tabs, *, batch, n, tm, per_batch_mod):
    nt = n // tm
    t_tot = batch * n
    row = (lambda b: b) if per_batch_mod else (lambda b: CTX_MOD_ROW)
    const = lambda b, i: (0, 0)
    tok = lambda b, i: (b * nt + i, 0)
    in_specs = [
        pl.BlockSpec((tm, D_MODEL), tok),
        pl.BlockSpec((1, 1, D_MODEL), lambda b, i: (row(b), 0, 0)),
        pl.BlockSpec((1, 1, D_MODEL), lambda b, i: (row(b), 0, 1)),
        pl.BlockSpec((1, D_MODEL), const),
        _resident((D_MODEL, IN_WIDTH), const),
        pl.BlockSpec((tm, 128), lambda b, i: (i, 0)),
        pl.BlockSpec((tm, 128), lambda b, i: (i, 0)),
        pl.BlockSpec((tm, 128), lambda b, i: (i, 0)),
        pl.BlockSpec((1, 256), const),
        pl.BlockSpec((1, 256), const),
        pl.BlockSpec((1, 256), const),
        pl.BlockSpec((1, 256), const),
        pl.BlockSpec((256, 256), const),
        pl.BlockSpec((1, GM_WIDTH), const),
        pl.BlockSpec((GM_CHUNK, GM_GROUPS * GM_CHUNK), const),
        pl.BlockSpec((GM_CHUNK, GM_WIDTH), const),
    ]
    out_specs = [
        pl.BlockSpec((tm, 256), tok),
        pl.BlockSpec((tm, 512), tok),
        pl.BlockSpec((tm, 256), tok),
        pl.BlockSpec((tm, 256), tok),
        pl.BlockSpec((tm, 512), tok),
        pl.BlockSpec((None, 512, tm), lambda b, i: (b, 0, i)),
        pl.BlockSpec((tm, 256), tok),
    ]
    out_shape = [
        jax.ShapeDtypeStruct((t_tot, 256), BF16),
        jax.ShapeDtypeStruct((t_tot, 512), BF16),
        jax.ShapeDtypeStruct((t_tot, 256), BF16),
        jax.ShapeDtypeStruct((t_tot, 256), BF16),
        jax.ShapeDtypeStruct((t_tot, 512), BF16),
        jax.ShapeDtypeStruct((batch, 512, n), BF16),
        jax.ShapeDtypeStruct((t_tot, 256), BF16),
    ]
    return pl.pallas_call(
        functools.partial(_inproj_kernel, tm=tm, sub=min(tm, 256)),
        grid=(batch, nt),
        in_specs=in_specs,
        out_specs=out_specs,
        out_shape=out_shape,
        compiler_params=pltpu.CompilerParams(
            dimension_semantics=("arbitrary", "arbitrary"), vmem_limit_bytes=V7X_VMEM_LIMIT),
        name="inproj",
    )(x2d, mod3, mod3, lw["norm1_g"], lw["w_in"], *rope_tabs,
      lw["gqa"], lw["gka"], lw["gqd"], lw["gkd"], lw["gmat"], lw["gv"], lw["ws_cat"], lw["bs_tab"])


def _na_kernel(*refs, has_window):
    if has_window:
        q_ref, k_ref, v_ref, kc_ref, vc_ref, tab_ref, o_ref = refs
        i = pl.program_id(1)
        u0 = jnp.clip(NA_Q_ROWS * i - NA_WIN_H // 2, 0, k_ref.shape[0] // GRID_W - NA_K_ROWS)
        off = pl.multiple_of(u0 * GRID_W, GRID_W)
        kw = k_ref[pl.ds(off, NA_K_ROWS * GRID_W), :]
        vw = v_ref[pl.ds(off, NA_K_ROWS * GRID_W), :]
    else:
        q_ref, kc_ref, vc_ref, o_ref = refs
    lane_head = lax.broadcasted_iota(jnp.int32, (1, 128), 1) // HEAD_DIM
    for p in range(NA_HEADS // 2):
        cols = slice(128 * p, 128 * p + 128)
        qp = q_ref[:, cols]
        kcp = kc_ref[:, cols]
        vcp = vc_ref[:, cols]
        acc = jnp.zeros((qp.shape[0], 128), F32)
        for j in range(2):
            sel = (lane_head == j).astype(BF16)
            qm = qp * sel
            sc = _dot_nt(qm, kcp)
            m = jnp.max(sc, axis=1, keepdims=True)
            if has_window:
                sw = _dot_nt(qm, kw[:, cols]) + tab_ref[2 * p + j]
                m = jnp.maximum(m, jnp.max(sw, axis=1, keepdims=True))
            ec = jnp.exp2(sc - m)
            l = jnp.sum(ec, axis=1, keepdims=True)
            o = _dot(ec.astype(BF16), vcp * sel)
            if has_window:
                ew = jnp.exp2(sw - m)
                l = l + jnp.sum(ew, axis=1, keepdims=True)
                o = o + _dot(ew.astype(BF16), vw[:, cols] * sel)
            acc = acc + o / l
        o_ref[:, cols] = acc.astype(BF16)


def _na_window(qa, ka, va, kac, vac, tab, *, batch, n):
    tq = NA_Q_ROWS * GRID_W
    nt = n // tq
    return pl.pallas_call(
        functools.partial(_na_kernel, has_window=True),
        grid=(batch, nt),
        in_specs=[
            pl.BlockSpec((tq, 256), lambda b, i: (b * nt + i, 0)),
            pl.BlockSpec((n, 256), lambda b, i: (b, 0)),
            pl.BlockSpec((n, 256), lambda b, i: (b, 0)),
            pl.BlockSpec((CTX_LEN, 256), lambda b, i: (b, 0)),
            pl.BlockSpec((CTX_LEN, 256), lambda b, i: (b, 0)),
            pl.BlockSpec((None, NA_HEADS, tq, NA_K_ROWS * GRID_W),
                         lambda b, i: (jnp.where(i == 0, 0, jnp.where(i == nt - 1, 2, 1)), 0, 0, 0)),
        ],
        out_specs=pl.BlockSpec((tq, 256), lambda b, i: (b * nt + i, 0)),
        out_shape=jax.ShapeDtypeStruct((batch * n, 256), BF16),
        compiler_params=pltpu.CompilerParams(
            dimension_semantics=("arbitrary", "arbitrary"), vmem_limit_bytes=V7X_VMEM_LIMIT),
        name="na_window",
    )(qa, ka, va, kac, vac, tab)


def _na_dense(qac, kac, vac, *, batch):
    spec = pl.BlockSpec((CTX_LEN, 256), lambda b: (b, 0))
    return pl.pallas_call(
        functools.partial(_na_kernel, has_window=False),
        grid=(batch,),
        in_specs=[spec, spec, spec],
        out_specs=spec,
        out_shape=jax.ShapeDtypeStruct((batch * CTX_LEN, 256), BF16),
        name="na_dense",
    )(qac, kac, vac)


def _da_kernel(*refs, nk, kt, lam_init):
    if nk:
        q_ref, k_ref, vt_ref, kc_ref, vtc_ref, lq1, lk1, lq2, lk2, subg_ref, o_ref = refs[:11]
    else:
        q_ref, kc_ref, vtc_ref, lq1, lk1, lq2, lk2, subg_ref, o_ref = refs[:9]
    kall, vtall, s_buf, bm_buf, acc_ref, m_ref = refs[-6:]
    tq = o_ref.shape[0]
    nt = q_ref.shape[0] // tq
    nblk = (nk + CTX_LEN) // kt
    i = pl.program_id(2)
    lane_map = lax.broadcasted_iota(jnp.int32, (1, 128), 1) // HEAD_DIM
    sel = [(lane_map == mi).astype(BF16) for mi in range(2)]

    def scores(tile, blk, slot):
        q = q_ref[pl.ds(pl.multiple_of(tile * tq, tq), tq), :]
        kblk = kall[pl.ds(pl.multiple_of(blk * kt, kt), kt), :]
        for mi in range(2):
            s = _dot_nt(kblk, q * sel[mi])
            s_buf[slot, mi] = s
            bm_buf[slot, mi] = jnp.max(s, axis=0, keepdims=True)

    @pl.when(i == 0)
    def _():
        if nk:
            kall[0:nk, :] = k_ref[...]
            vtall[0:DA_V_DIM, 0:nk] = vt_ref[...]
        kall[nk:nk + CTX_LEN, :] = kc_ref[...]
        vtall[0:DA_V_DIM, nk:nk + CTX_LEN] = vtc_ref[...]
        vtall[DA_V_DIM:, :] = jnp.ones((ONES_ROWS, nk + CTX_LEN), BF16)
        scores(0, 0, 0)

    acc_ref[...] = jnp.zeros_like(acc_ref)
    m_ref[...] = jnp.full_like(m_ref, NEG_INF)

    def softmax_pv(blk, slot):
        vte = vtall[:, pl.ds(pl.multiple_of(blk * kt, kt), kt)]
        for mi in range(2):
            for g in range(tq // 256):
                cols = slice(256 * g, 256 * g + 256)
                m_old = m_ref[mi, :, cols]
                m_new = jnp.maximum(m_old, bm_buf[slot, mi, :, cols])
                alpha = jnp.exp2(m_old - m_new)
                e = jnp.exp2(s_buf[slot, mi, :, cols] - m_new).astype(BF16)
                acc_ref[mi, :, cols] = acc_ref[mi, :, cols] * alpha + _dot(vte, e)
                m_ref[mi, :, cols] = m_new

    if nblk == 1:
        softmax_pv(0, 0)
    else:
        for blk in range(nblk):
            if blk + 1 < nblk:
                scores(i, blk + 1, (blk + 1) % 2)
            else:
                scores(jnp.minimum(i + 1, nt - 1), 0, 0)
            softmax_pv(blk, blk % 2)

    lam = (jnp.exp(jnp.sum(lq1[...] * lk1[...], keepdims=True))
           - jnp.exp(jnp.sum(lq2[...] * lk2[...], keepdims=True)) + lam_init)
    a0 = acc_ref[0]
    a1 = acc_ref[1]
    o = a0[:DA_V_DIM] / a0[DA_V_DIM:DA_V_DIM + 1] - lam * (a1[:DA_V_DIM] / a1[DA_V_DIM:DA_V_DIM + 1])
    y = o * lax.rsqrt(jnp.mean(o * o, axis=0, keepdims=True) + EPS) * subg_ref[...] * (1.0 - lam_init)
    o_ref[...] = y.T.astype(BF16)


def _diff_attention(qd, kd, vdt, kdc, vdtc, lw, *, batch, nq, nk, tq, kt, lam_init):
    nt = nq // tq
    nkeys = nk + CTX_LEN
    nblk = nkeys // kt
    assert nkeys % kt == 0 and tq % 256 == 0 and nq % tq == 0 and (nblk == 1 and nt == 1 or nblk % 2 == 0)
    in_specs = [pl.BlockSpec((nq, 128), lambda b, h, i: (b, h))]
    args = [qd]
    if nk:
        in_specs += [pl.BlockSpec((nk, 128), lambda b, h, i: (b, h)),
                     pl.BlockSpec((None, 128, nk), lambda b, h, i: (b, h, 0))]
        args += [kd, vdt]
    in_specs += [pl.BlockSpec((CTX_LEN, 128), lambda b, h, i: (b, h)),
                 pl.BlockSpec((None, 128, CTX_LEN), lambda b, h, i: (b, h, 0))]
    args += [kdc, vdtc]
    in_specs += [pl.BlockSpec((1, HEAD_DIM), lambda b, h, i: (0, 0))] * 4
    args += [lw["lq1"], lw["lk1"], lw["lq2"], lw["lk2"]]
    in_specs += [pl.BlockSpec((DA_V_DIM, 1), lambda b, h, i: (0, 0))]
    args += [lw["sub_g"]]
    return pl.pallas_call(
        functools.partial(_da_kernel, nk=nk, kt=kt, lam_init=lam_init),
        grid=(batch, DA_HEADS, nt),
        in_specs=in_specs,
        out_specs=pl.BlockSpec((tq, 128), lambda b, h, i: (b * nt + i, h)),
        out_shape=jax.ShapeDtypeStruct((batch * nq, DA_WIDTH), BF16),
        scratch_shapes=[
            pltpu.VMEM((nkeys, 128), BF16),
            pltpu.VMEM((DA_V_DIM + ONES_ROWS, nkeys), BF16),
            pltpu.VMEM((2, 2, kt, tq), F32),
            pltpu.VMEM((2, 2, 1, tq), F32),
            pltpu.VMEM((2, DA_V_DIM + ONES_ROWS, tq), F32),
            pltpu.VMEM((2, 1, tq), F32),
        ],
        compiler_params=pltpu.CompilerParams(
            dimension_semantics=("arbitrary", "arbitrary", "arbitrary"), vmem_limit_bytes=V7X_VMEM_LIMIT),
        name="diff_attn" if nk else "diff_attn_ctx",
    )(*args)


def _ffn_kernel(x_ref, oa_ref, ob_ref, oc_ref, gate1_ref, shift2_ref, scale2_ref, gate2_ref, g2_ref,
                wout_ref, w1_ref, w3_ref, w2_ref, o_ref):
    mixed = (_dot(oa_ref[...], wout_ref[0:NA_WIDTH, :])
             + _dot(ob_ref[...], wout_ref[NA_WIDTH:NA_WIDTH + GM_WIDTH, :])
             + _dot(oc_ref[...], wout_ref[NA_WIDTH + GM_WIDTH:, :]))
    x1 = x_ref[...] + gate1_ref[0] * mixed
    h = x1 * lax.rsqrt(jnp.mean(x1 * x1, axis=-1, keepdims=True) + EPS) * g2_ref[...]
    hb = (h * (1.0 + scale2_ref[0]) + shift2_ref[0]).astype(BF16)
    a = _dot(hb, w1_ref[...])
    b = _dot(hb, w3_ref[...])
    g = (a * jax.nn.sigmoid(a) * b).astype(BF16)
    o_ref[...] = x1 + gate2_ref[0] * _dot(g, w2_ref[...])


def _out_ffn(x2d, oa, ob, oc, mod3, lw, *, batch, n, tm, per_batch_mod):
    nt = n // tm
    row = (lambda b: b) if per_batch_mod else (lambda b: CTX_MOD_ROW)
    const = lambda b, i: (0, 0)
    tok = lambda b, i: (b * nt + i, 0)
    mod_spec = lambda k: pl.BlockSpec((1, 1, D_MODEL), lambda b, i: (row(b), 0, k))
    return pl.pallas_call(
        _ffn_kernel,
        grid=(batch, nt),
        in_specs=[
            pl.BlockSpec((tm, D_MODEL), tok),
            pl.BlockSpec((tm, NA_WIDTH), tok),
            pl.BlockSpec((tm, GM_WIDTH), tok),
            pl.BlockSpec((tm, DA_WIDTH), tok),
            mod_spec(2), mod_spec(3), mod_spec(4), mod_spec(5),
            pl.BlockSpec((1, D_MODEL), const),
            _resident((D_MODEL, D_MODEL), const),
            _resident((D_MODEL, D_FF), const),
            _resident((D_MODEL, D_FF), const),
            _resident((D_FF, D_MODEL), const),
        ],
        out_specs=pl.BlockSpec((tm, D_MODEL), tok),
        out_shape=jax.ShapeDtypeStruct((batch * n, D_MODEL), F32),
        compiler_params=pltpu.CompilerParams(
            dimension_semantics=("arbitrary", "arbitrary"), vmem_limit_bytes=V7X_VMEM_LIMIT),
        name="out_ffn",
    )(x2d, oa, ob, oc, mod3, mod3, mod3, mod3, lw["norm2_g"], lw["w_out"], lw["w1"], lw["w3"], lw["w2"])


def _rope_tables(n):
    t = jnp.arange(n, dtype=jnp.int32)
    row = (t // GRID_W).astype(F32)
    col = (t % GRID_W).astype(F32)
    half = HEAD_DIM // 2
    inv = ROPE_BASE ** (-jnp.arange(0, half, 2, dtype=F32) / half)
    ar = row[:, None] * inv[None, :]
    ac = col[:, None] * inv[None, :]
    ang = jnp.concatenate([ar, ar, ac, ac], axis=-1)
    cos = jnp.cos(ang)
    sin = jnp.sin(ang)
    first = (np.arange(HEAD_DIM) % 32) < 16
    sa = jnp.where(first, -sin, 0.0)
    sb = jnp.where(first, 0.0, sin)
    return tuple(jnp.tile(a, (1, 2)) for a in (cos, sa, sb))


def _identity_rope_tables(n):
    return (jnp.ones((n, 128), F32), jnp.zeros((n, 128), F32), jnp.zeros((n, 128), F32))


def _na_bias_tables(rpb, rows):
    cq = np.arange(GRID_W)
    c0 = np.clip(cq - NA_WIN_W // 2, 0, GRID_W - NA_WIN_W)
    col_ok = (cq[None, :] >= c0[:, None]) & (cq[None, :] < c0[:, None] + NA_WIN_W)
    coff = np.clip(cq[None, :] - cq[:, None], -(NA_WIN_W - 1), NA_WIN_W - 1) + (NA_WIN_W - 1)
    col_sel = (coff[None] == np.arange(2 * NA_WIN_W - 1)[:, None, None]).astype(np.float32)
    row_sel, ok = [], []
    for r0 in (0, 2 * NA_Q_ROWS, rows - NA_Q_ROWS):
        u0 = min(max(r0 - NA_WIN_H // 2, 0), rows - NA_K_ROWS)
        r = r0 + np.arange(NA_Q_ROWS)
        key_row = u0 + np.arange(NA_K_ROWS)
        start = np.clip(r - NA_WIN_H // 2, 0, rows - NA_WIN_H)
        row_ok = (key_row[None, :] >= start[:, None]) & (key_row[None, :] < start[:, None] + NA_WIN_H)
        roff = key_row[None, :] - r[:, None] + (NA_WIN_H - 1)
        row_sel.append((roff[None] == np.arange(2 * NA_WIN_H - 1)[:, None, None]).astype(np.float32))
        ok.append(row_ok[:, None, :, None] & col_ok[None, :, None, :])
    toeplitz = jnp.einsum("hrc,cqk->hrqk", rpb.astype(F32), jnp.asarray(col_sel),
                          precision=lax.Precision.HIGHEST)
    bias = jnp.einsum("traj,hrqk->thaqjk", jnp.asarray(np.stack(row_sel)), toeplitz,
                      precision=lax.Precision.HIGHEST)
    tab = jnp.where(jnp.asarray(np.stack(ok))[:, None], bias * LOG2E, NEG_INF)
    return tab.reshape(3, NA_HEADS, NA_Q_ROWS * GRID_W, NA_K_ROWS * GRID_W)


def _layer_weights(i, p):
    tile4 = lambda g: jnp.tile(g.astype(F32), 256 // HEAD_DIM).reshape(1, 256)
    blk = np.arange(256) // HEAD_DIM
    return {
        "norm1_g": p["norm1_g"][i].reshape(1, D_MODEL),
        "w_in": p["w_in"][i].astype(BF16),
        "gqa": tile4(p["na_q_g"][i]) * (HEAD_DIM ** -0.5 * LOG2E),
        "gka": tile4(p["na_k_g"][i]),
        "gqd": tile4(p["da_q_g"][i]) * (HEAD_DIM ** -0.5 * LOG2E),
        "gkd": tile4(p["da_k_g"][i]),
        "gmat": jnp.asarray(blk[:, None] == blk[None, :], BF16),
        "gv": p["gm_v_g"][i].reshape(1, GM_WIDTH),
        "ws_cat": p["gm_ws"][i].transpose(1, 0, 2).reshape(GM_CHUNK, GM_GROUPS * GM_CHUNK).astype(BF16),
        "bs_tab": jnp.repeat(p["gm_bs"][i].T, GM_WIDTH // GM_GROUPS, axis=1),
        "lq1": p["da_lq1"][i].reshape(1, HEAD_DIM),
        "lk1": p["da_lk1"][i].reshape(1, HEAD_DIM),
        "lq2": p["da_lq2"][i].reshape(1, HEAD_DIM),
        "lk2": p["da_lk2"][i].reshape(1, HEAD_DIM),
        "sub_g": p["da_sub_g"][i].reshape(DA_V_DIM, 1),
        "norm2_g": p["norm2_g"][i].reshape(1, D_MODEL),
        "w_out": p["w_out"][i].astype(BF16),
        "w1": p["ffn_w1"][i].astype(BF16),
        "w3": p["ffn_w3"][i].astype(BF16),
        "w2": p["ffn_w2"][i].astype(BF16),
    }


def kernel(x, c, ctx, c_ctx, w_mod, b_mod, norm1_g, w_in, na_q_g, na_k_g, na_rpb, gm_v_g, gm_ws, gm_bs,
           da_q_g, da_k_g, da_lq1, da_lk1, da_lq2, da_lk2, da_sub_g, w_out, norm2_g, ffn_w1, ffn_w3, ffn_w2):
    batch, n, _ = x.shape
    depth = w_mod.shape[0]
    assert n % (NA_Q_ROWS * GRID_W) == 0 and ctx.shape[1] == CTX_LEN and batch < CTX_MOD_ROW + 1
    params = dict(norm1_g=norm1_g, w_in=w_in, na_q_g=na_q_g, na_k_g=na_k_g, gm_v_g=gm_v_g, gm_ws=gm_ws,
                  gm_bs=gm_bs, da_q_g=da_q_g, da_k_g=da_k_g, da_lq1=da_lq1, da_lk1=da_lk1, da_lq2=da_lq2,
                  da_lk2=da_lk2, da_sub_g=da_sub_g, w_out=w_out, norm2_g=norm2_g, ffn_w1=ffn_w1,
                  ffn_w3=ffn_w3, ffn_w2=ffn_w2)

    c8 = jnp.zeros((MOD_ROWS, D_MODEL), F32).at[:batch].set(c).at[CTX_MOD_ROW].set(c_ctx)
    mod_all = _modulation(c8, w_mod, b_mod)

    rope_lat = _rope_tables(n)
    rope_ctx = _identity_rope_tables(CTX_LEN)
    x2d = x.reshape(batch * n, D_MODEL)
    xc2d = ctx.reshape(batch * CTX_LEN, D_MODEL)

    for i in range(depth):
        lam_init = 0.8 - 0.6 * math.exp(-0.3 * i)
        lw = _layer_weights(i, params)
        mod3 = mod_all[i].reshape(MOD_ROWS, 1, 6 * D_MODEL)
        tab = _na_bias_tables(na_rpb[i], n // GRID_W)

        qa, qd, ka, va, kd, vdt, ob = _inproj(x2d, mod3, lw, rope_lat, batch=batch, n=n, tm=1024,
                                              per_batch_mod=True)
        qac, qdc, kac, vac, kdc, vdtc, obc = _inproj(xc2d, mod3, lw, rope_ctx, batch=batch, n=CTX_LEN,
                                                     tm=CTX_LEN, per_batch_mod=False)
        oa = _na_window(qa, ka, va, kac, vac, tab, batch=batch, n=n)
        oc = _diff_attention(qd, kd, vdt, kdc, vdtc, lw, batch=batch, nq=n, nk=n, tq=512, kt=1408,
                             lam_init=lam_init)
        x2d = _out_ffn(x2d, oa, ob, oc, mod3, lw, batch=batch, n=n, tm=512, per_batch_mod=True)
        if i < depth - 1:
            oac = _na_dense(qac, kac, vac, batch=batch)
            occ = _diff_attention(qdc, None, None, kdc, vdtc, lw, batch=batch, nq=CTX_LEN, nk=0,
                                  tq=CTX_LEN, kt=CTX_LEN, lam_init=lam_init)
            xc2d = _out_ffn(xc2d, oac, obc, occ, mod3, lw, batch=batch, n=CTX_LEN, tm=CTX_LEN,
                            per_batch_mod=False)
    return x2d.reshape(batch, n, D_MODEL)
```

```python
import functools
import math

import numpy as np
import jax
import jax.numpy as jnp
from jax import lax
from jax.experimental import pallas as pl
from jax.experimental.pallas import tpu as pltpu

D_MODEL = 1024
GRID_W = 64
CTX_LEN = 256
HEAD_DIM = 64
NA_HEADS = 4
NA_WIN_H = 8
NA_WIN_W = 16
NA_WIDTH = NA_HEADS * HEAD_DIM
GM_GROUPS = 4
GM_CHUNK = 128
GM_WIDTH = 256
DA_HEADS = 4
DA_QK_DIM = 2 * HEAD_DIM
DA_V_DIM = 2 * HEAD_DIM
DA_WIDTH = DA_HEADS * DA_V_DIM
QU_WIDTH = NA_WIDTH + DA_HEADS * DA_QK_DIM + 2 * GM_WIDTH
IN_WIDTH = QU_WIDTH + 2 * NA_WIDTH + DA_HEADS * DA_QK_DIM + DA_HEADS * DA_V_DIM
D_FF = -(-8 * D_MODEL // (3 * 256)) * 256
ROPE_BASE = 10000.0
EPS = 1e-6
NEG_INF = -1e30
LOG2E = math.log2(math.e)

COL_QA = 0
COL_QD = NA_WIDTH
COL_UV = COL_QD + DA_HEADS * DA_QK_DIM
COL_KA = QU_WIDTH
COL_VA = COL_KA + NA_WIDTH
COL_KD = COL_VA + NA_WIDTH
COL_VD = COL_KD + DA_HEADS * DA_QK_DIM

MOD_ROWS = 8
CTX_MOD_ROW = 4
NA_Q_ROWS = 4
NA_K_ROWS = 12
ONES_ROWS = 16
DA_MAX_CONSTANT_SHIFT = 48.0
V7X_VMEM_LIMIT = 56 * 2 ** 20

F32 = jnp.float32
BF16 = jnp.bfloat16


def _dot(a, b):
    return jnp.dot(a, b, preferred_element_type=F32)


def _dot_nt(a, b):
    return lax.dot_general(a, b, (((1,), (1,)), ((), ())), preferred_element_type=F32)


def _resident(shape, index_map):
    return pl.BlockSpec(shape, index_map, pipeline_mode=pl.Buffered(1))


def _mod_kernel(c_ref, w_ref, b_ref, o_ref):
    c = c_ref[...]
    a = c * jax.nn.sigmoid(c)
    a_hi = a.astype(BF16)
    a_lo = (a - a_hi.astype(F32)).astype(BF16)
    w = w_ref[...]
    w_hi = w.astype(BF16)
    w_lo = (w - w_hi.astype(F32)).astype(BF16)
    o_ref[...] = _dot(a_hi, w_hi) + _dot(a_lo, w_hi) + _dot(a_hi, w_lo) + b_ref[...]


def _modulation(c8, w_mod, b_mod):
    depth = w_mod.shape[0]
    tn = 1024
    return pl.pallas_call(
        _mod_kernel,
        grid=(depth, 6 * D_MODEL // tn),
        in_specs=[
            pl.BlockSpec((MOD_ROWS, D_MODEL), lambda l, j: (0, 0)),
            pl.BlockSpec((None, D_MODEL, tn), lambda l, j: (l, 0, j)),
            pl.BlockSpec((None, 1, tn), lambda l, j: (l, 0, j)),
        ],
        out_specs=pl.BlockSpec((None, MOD_ROWS, tn), lambda l, j: (l, 0, j)),
        out_shape=jax.ShapeDtypeStruct((depth, MOD_ROWS, 6 * D_MODEL), F32),
        name="adaln_mod",
    )(c8, w_mod, b_mod.reshape(depth, 1, 6 * D_MODEL))


def _inproj_kernel(x_ref, shift_ref, scale_ref, g1_ref, w_ref, cos_ref, sa_ref, sb_ref,
                   gqa_ref, gka_ref, gqd_ref, gkd_ref, gmat_ref, gv_ref, ws_ref, bs_ref,
                   qa_ref, qd_ref, ka_ref, va_ref, kd_ref, vdt_ref, ob_ref, *, tm, sub):
    gmat = gmat_ref[...]
    group = lax.broadcasted_iota(jnp.int32, (GM_CHUNK, GM_WIDTH), 1) // (GM_WIDTH // GM_GROUPS)
    ws = ws_ref[...]
    bs = bs_ref[...]

    def head_norm(y, g):
        ss = _dot((y * y).astype(BF16), gmat)
        return y * lax.rsqrt(ss * (1.0 / HEAD_DIM) + EPS) * g

    for r0 in range(0, tm, sub):
        rows = slice(r0, r0 + sub)
        x = x_ref[rows, :]
        h = x * lax.rsqrt(jnp.mean(x * x, axis=-1, keepdims=True) + EPS) * g1_ref[...]
        hb = (h * (1.0 + scale_ref[0]) + shift_ref[0]).astype(BF16)
        p = _dot(hb, w_ref[...])
        cos = cos_ref[rows, :]
        sa = sa_ref[rows, :]
        sb = sb_ref[rows, :]

        def rope(z):
            return z * cos + pltpu.roll(z, 128 - 16, 1) * sa + pltpu.roll(z, 16, 1) * sb

        qa_ref[rows, :] = head_norm(p[:, COL_QA:COL_QA + 256], gqa_ref[...]).astype(BF16)
        ka_ref[rows, :] = head_norm(p[:, COL_KA:COL_KA + 256], gka_ref[...]).astype(BF16)
        va_ref[rows, :] = p[:, COL_VA:COL_VA + 256].astype(BF16)
        for c in range(2):
            yq = head_norm(p[:, COL_QD + 256 * c:COL_QD + 256 * c + 256], gqd_ref[...])
            yk = head_norm(p[:, COL_KD + 256 * c:COL_KD + 256 * c + 256], gkd_ref[...])
            for t in range(2):
                lo = 256 * c + 128 * t
                qd_ref[rows, lo:lo + 128] = rope(yq[:, 128 * t:128 * t + 128]).astype(BF16)
                kd_ref[rows, lo:lo + 128] = rope(yk[:, 128 * t:128 * t + 128]).astype(BF16)
        vdt_ref[:, rows] = p[:, COL_VD:COL_VD + 512].T.astype(BF16)

        z = jax.nn.gelu(p[:, COL_UV:COL_UV + 2 * GM_WIDTH])
        u = z[:, :GM_WIDTH]
        v = z[:, GM_WIDTH:]
        v = v * lax.rsqrt(jnp.mean(v * v, axis=-1, keepdims=True) + EPS) * gv_ref[...]
        vb = v.astype(BF16)
        for c in range(sub // GM_CHUNK):
            vc = vb[c * GM_CHUNK:(c + 1) * GM_CHUNK, :]
            vbd = jnp.concatenate([jnp.where(group == g, vc, jnp.zeros_like(vc)) for g in range(GM_GROUPS)],
                                  axis=0)
            s = _dot(ws, vbd) + bs
            lo = r0 + c * GM_CHUNK
            ob_ref[lo:lo + GM_CHUNK, :] = (u[c * GM_CHUNK:(c + 1) * GM_CHUNK, :] * s).astype(BF16)


def _inproj(x2d, mod3, lw, rope_tabs, *, batch, n, tm, per_batch_mod):
    nt = n // tm
    t_tot = batch * n
    row = (lambda b: b) if per_batch_mod else (lambda b: CTX_MOD_ROW)
    const = lambda b, i: (0, 0)
    tok = lambda b, i: (b * nt + i, 0)
    in_specs = [
        pl.BlockSpec((tm, D_MODEL), tok),
        pl.BlockSpec((1, 1, D_MODEL), lambda b, i: (row(b), 0, 0)),
        pl.BlockSpec((1, 1, D_MODEL), lambda b, i: (row(b), 0, 1)),
        pl.BlockSpec((1, D_MODEL), const),
        _resident((D_MODEL, IN_WIDTH), const),
        pl.BlockSpec((tm, 128), lambda b, i: (i, 0)),
        pl.BlockSpec((tm, 128), lambda b, i: (i, 0)),
        pl.BlockSpec((tm, 128), lambda b, i: (i, 0)),
        pl.BlockSpec((1, 256), const),
        pl.BlockSpec((1, 256), const),
        pl.BlockSpec((1, 256), const),
        pl.BlockSpec((1, 256), const),
        pl.BlockSpec((256, 256), const),
        pl.BlockSpec((1, GM_WIDTH), const),
        pl.BlockSpec((GM_CHUNK, GM_GROUPS * GM_CHUNK), const),
        pl.BlockSpec((GM_CHUNK, GM_WIDTH), const),
    ]
    out_specs = [
        pl.BlockSpec((tm, 256), tok),
        pl.BlockSpec((tm, 512), tok),
        pl.BlockSpec((tm, 256), tok),
        pl.BlockSpec((tm, 256), tok),
        pl.BlockSpec((tm, 512), tok),
        pl.BlockSpec((None, 512, tm), lambda b, i: (b, 0, i)),
        pl.BlockSpec((tm, 256), tok),
    ]
    out_shape = [
        jax.ShapeDtypeStruct((t_tot, 256), BF16),
        jax.ShapeDtypeStruct((t_tot, 512), BF16),
        jax.ShapeDtypeStruct((t_tot, 256), BF16),
        jax.ShapeDtypeStruct((t_tot, 256), BF16),
        jax.ShapeDtypeStruct((t_tot, 512), BF16),
        jax.ShapeDtypeStruct((batch, 512, n), BF16),
        jax.ShapeDtypeStruct((t_tot, 256), BF16),
    ]
    return pl.pallas_call(
        functools.partial(_inproj_kernel, tm=tm, sub=min(tm, 256)),
        grid=(batch, nt),
        in_specs=in_specs,
        out_specs=out_specs,
        out_shape=out_shape,
        compiler_params=pltpu.CompilerParams(
            dimension_semantics=("arbitrary", "arbitrary"), vmem_limit_bytes=V7X_VMEM_LIMIT),
        name="inproj",
    )(x2d, mod3, mod3, lw["norm1_g"], lw["w_in"], *rope_tabs,
      lw["gqa"], lw["gka"], lw["gqd"], lw["gkd"], lw["gmat"], lw["gv"], lw["ws_cat"], lw["bs_tab"])


def _na_kernel(*refs, has_window):
    if has_window:
        q_ref, k_ref, v_ref, kc_ref, vc_ref, tab_ref, o_ref = refs
        i = pl.program_id(1)
        u0 = jnp.clip(NA_Q_ROWS * i - NA_WIN_H // 2, 0, k_ref.shape[0] // GRID_W - NA_K_ROWS)
        off = pl.multiple_of(u0 * GRID_W, GRID_W)
        kw = k_ref[pl.ds(off, NA_K_ROWS * GRID_W), :]
        vw = v_ref[pl.ds(off, NA_K_ROWS * GRID_W), :]
    else:
        q_ref, kc_ref, vc_ref, o_ref = refs
    lane_head = lax.broadcasted_iota(jnp.int32, (1, 128), 1) // HEAD_DIM
    for p in range(NA_HEADS // 2):
        cols = slice(128 * p, 128 * p + 128)
        qp = q_ref[:, cols]
        kcp = kc_ref[:, cols]
        vcp = vc_ref[:, cols]
        acc = jnp.zeros((qp.shape[0], 128), F32)
        for j in range(2):
            sel = (lane_head == j).astype(BF16)
            qm = qp * sel
            sc = _dot_nt(qm, kcp)
            m = jnp.max(sc, axis=1, keepdims=True)
            if has_window:
                sw = _dot_nt(qm, kw[:, cols]) + tab_ref[2 * p + j]
                m = jnp.maximum(m, jnp.max(sw, axis=1, keepdims=True))
            ec = jnp.exp2(sc - m)
            l = jnp.sum(ec, axis=1, keepdims=True)
            o = _dot(ec.astype(BF16), vcp * sel)
            if has_window:
                ew = jnp.exp2(sw - m)
                l = l + jnp.sum(ew, axis=1, keepdims=True)
                o = o + _dot(ew.astype(BF16), vw[:, cols] * sel)
            acc = acc + o / l
        o_ref[:, cols] = acc.astype(BF16)


def _na_window(qa, ka, va, kac, vac, tab, *, batch, n):
    tq = NA_Q_ROWS * GRID_W
    nt = n // tq
    return pl.pallas_call(
        functools.partial(_na_kernel, has_window=True),
        grid=(batch, nt),
        in_specs=[
            pl.BlockSpec((tq, 256), lambda b, i: (b * nt + i, 0)),
            pl.BlockSpec((n, 256), lambda b, i: (b, 0)),
            pl.BlockSpec((n, 256), lambda b, i: (b, 0)),
            pl.BlockSpec((CTX_LEN, 256), lambda b, i: (b, 0)),
            pl.BlockSpec((CTX_LEN, 256), lambda b, i: (b, 0)),
            pl.BlockSpec((None, NA_HEADS, tq, NA_K_ROWS * GRID_W),
                         lambda b, i: (jnp.where(i == 0, 0, jnp.where(i == nt - 1, 2, 1)), 0, 0, 0)),
        ],
        out_specs=pl.BlockSpec((tq, 256), lambda b, i: (b * nt + i, 0)),
        out_shape=jax.ShapeDtypeStruct((batch * n, 256), BF16),
        compiler_params=pltpu.CompilerParams(
            dimension_semantics=("arbitrary", "arbitrary"), vmem_limit_bytes=V7X_VMEM_LIMIT),
        name="na_window",
    )(qa, ka, va, kac, vac, tab)


def _na_dense(qac, kac, vac, *, batch):
    spec = pl.BlockSpec((CTX_LEN, 256), lambda b: (b, 0))
    return pl.pallas_call(
        functools.partial(_na_kernel, has_window=False),
        grid=(batch,),
        in_specs=[spec, spec, spec],
        out_specs=spec,
        out_shape=jax.ShapeDtypeStruct((batch * CTX_LEN, 256), BF16),
        name="na_dense",
    )(qac, kac, vac)


def _da_kernel(*refs, nk, kt, lam_init):
    if nk:
        q_ref, k_ref, vt_ref, kc_ref, vtc_ref, lq1, lk1, lq2, lk2, subg_ref, o_ref = refs[:11]
    else:
        q_ref, kc_ref, vtc_ref, lq1, lk1, lq2, lk2, subg_ref, o_ref = refs[:9]
    kall, vtall, s_buf, bm_buf, acc_ref, m_ref = refs[-6:]
    tq = o_ref.shape[0]
    nt = q_ref.shape[0] // tq
    nblk = (nk + CTX_LEN) // kt
    i = pl.program_id(2)
    lane_map = lax.broadcasted_iota(jnp.int32, (1, 128), 1) // HEAD_DIM
    sel = [(lane_map == mi).astype(BF16) for mi in range(2)]

    def scores(tile, blk, slot):
        q = q_ref[pl.ds(pl.multiple_of(tile * tq, tq), tq), :]
        kblk = kall[pl.ds(pl.multiple_of(blk * kt, kt), kt), :]
        for mi in range(2):
            s = _dot_nt(kblk, q * sel[mi])
            s_buf[slot, mi] = s
            bm_buf[slot, mi] = jnp.max(s, axis=0, keepdims=True)

    @pl.when(i == 0)
    def _():
        if nk:
            kall[0:nk, :] = k_ref[...]
            vtall[0:DA_V_DIM, 0:nk] = vt_ref[...]
        kall[nk:nk + CTX_LEN, :] = kc_ref[...]
        vtall[0:DA_V_DIM, nk:nk + CTX_LEN] = vtc_ref[...]
        vtall[DA_V_DIM:, :] = jnp.ones((ONES_ROWS, nk + CTX_LEN), BF16)
        scores(0, 0, 0)

    acc_ref[...] = jnp.zeros_like(acc_ref)
    m_ref[...] = jnp.full_like(m_ref, NEG_INF)

    def softmax_pv(blk, slot):
        vte = vtall[:, pl.ds(pl.multiple_of(blk * kt, kt), kt)]
        for mi in range(2):
            for g in range(tq // 256):
                cols = slice(256 * g, 256 * g + 256)
                m_old = m_ref[mi, :, cols]
                m_new = jnp.maximum(m_old, bm_buf[slot, mi, :, cols])
                alpha = jnp.exp2(m_old - m_new)
                e = jnp.exp2(s_buf[slot, mi, :, cols] - m_new).astype(BF16)
                acc_ref[mi, :, cols] = acc_ref[mi, :, cols] * alpha + _dot(vte, e)
                m_ref[mi, :, cols] = m_new

    if nblk == 1:
        softmax_pv(0, 0)
    else:
        for blk in range(nblk):
            if blk + 1 < nblk:
                scores(i, blk + 1, (blk + 1) % 2)
            else:
                scores(jnp.minimum(i + 1, nt - 1), 0, 0)
            softmax_pv(blk, blk % 2)

    _da_finalize(acc_ref[0], acc_ref[1], lq1, lk1, lq2, lk2, subg_ref, o_ref, lam_init)


def _da_finalize(a0, a1, lq1, lk1, lq2, lk2, subg_ref, o_ref, lam_init):
    lam = (jnp.exp(jnp.sum(lq1[...] * lk1[...], keepdims=True))
           - jnp.exp(jnp.sum(lq2[...] * lk2[...], keepdims=True)) + lam_init)
    o = a0[:DA_V_DIM] / a0[DA_V_DIM:DA_V_DIM + 1] - lam * (a1[:DA_V_DIM] / a1[DA_V_DIM:DA_V_DIM + 1])
    y = o * lax.rsqrt(jnp.mean(o * o, axis=0, keepdims=True) + EPS) * subg_ref[...] * (1.0 - lam_init)
    o_ref[...] = y.T.astype(BF16)


def _da_bounded_kernel(q_ref, k_ref, vt_ref, kc_ref, vtc_ref, bound_ref, lq1, lk1, lq2, lk2, subg_ref, o_ref,
                       *, kt, lam_init):
    q = q_ref[...]
    lane_map = lax.broadcasted_iota(jnp.int32, (1, 128), 1) // HEAD_DIM
    qms = [q * (lane_map == mi).astype(BF16) for mi in range(2)]
    shift = bound_ref[...]
    acc = [None, None]

    def block(kblk, vtblk):
        vte = jnp.concatenate([vtblk, jnp.ones((ONES_ROWS, vtblk.shape[1]), BF16)], axis=0)
        for mi in range(2):
            e = jnp.exp2(_dot_nt(kblk, qms[mi]) - shift).astype(BF16)
            d = _dot(vte, e)
            acc[mi] = d if acc[mi] is None else acc[mi] + d

    for j in range(k_ref.shape[0] // kt):
        block(k_ref[j * kt:(j + 1) * kt, :], vt_ref[:, j * kt:(j + 1) * kt])
    block(kc_ref[...], vtc_ref[...])
    _da_finalize(acc[0], acc[1], lq1, lk1, lq2, lk2, subg_ref, o_ref, lam_init)


def _diff_attention_bounded(qd, kd, vdt, kdc, vdtc, bound, lw, *, batch, nq, nk, tq, kt, lam_init):
    nt = nq // tq
    assert nk % kt == 0 and nq % tq == 0
    small = lambda shape: pl.BlockSpec(shape, lambda b, h, i: (0, 0))
    return pl.pallas_call(
        functools.partial(_da_bounded_kernel, kt=kt, lam_init=lam_init),
        grid=(batch, DA_HEADS, nt),
        in_specs=[
            pl.BlockSpec((tq, 128), lambda b, h, i: (b * nt + i, h)),
            pl.BlockSpec((nk, 128), lambda b, h, i: (b, h)),
            pl.BlockSpec((None, 128, nk), lambda b, h, i: (b, h, 0)),
            pl.BlockSpec((CTX_LEN, 128), lambda b, h, i: (b, h)),
            pl.BlockSpec((None, 128, CTX_LEN), lambda b, h, i: (b, h, 0)),
            small((1, 1)), small((1, HEAD_DIM)), small((1, HEAD_DIM)), small((1, HEAD_DIM)), small((1, HEAD_DIM)),
            small((DA_V_DIM, 1)),
        ],
        out_specs=pl.BlockSpec((tq, 128), lambda b, h, i: (b * nt + i, h)),
        out_shape=jax.ShapeDtypeStruct((batch * nq, DA_WIDTH), BF16),
        compiler_params=pltpu.CompilerParams(
            dimension_semantics=("arbitrary", "arbitrary", "arbitrary"), vmem_limit_bytes=V7X_VMEM_LIMIT),
        name="diff_attn_bounded",
    )(qd, kd, vdt, kdc, vdtc, bound, lw["lq1"], lw["lk1"], lw["lq2"], lw["lk2"], lw["sub_g"])


def _diff_attention(qd, kd, vdt, kdc, vdtc, lw, *, batch, nq, nk, tq, kt, lam_init):
    nt = nq // tq
    nkeys = nk + CTX_LEN
    nblk = nkeys // kt
    assert nkeys % kt == 0 and tq % 256 == 0 and nq % tq == 0 and (nblk == 1 and nt == 1 or nblk % 2 == 0)
    in_specs = [pl.BlockSpec((nq, 128), lambda b, h, i: (b, h))]
    args = [qd]
    if nk:
        in_specs += [pl.BlockSpec((nk, 128), lambda b, h, i: (b, h)),
                     pl.BlockSpec((None, 128, nk), lambda b, h, i: (b, h, 0))]
        args += [kd, vdt]
    in_specs += [pl.BlockSpec((CTX_LEN, 128), lambda b, h, i: (b, h)),
                 pl.BlockSpec((None, 128, CTX_LEN), lambda b, h, i: (b, h, 0))]
    args += [kdc, vdtc]
    in_specs += [pl.BlockSpec((1, HEAD_DIM), lambda b, h, i: (0, 0))] * 4
    args += [lw["lq1"], lw["lk1"], lw["lq2"], lw["lk2"]]
    in_specs += [pl.BlockSpec((DA_V_DIM, 1), lambda b, h, i: (0, 0))]
    args += [lw["sub_g"]]
    return pl.pallas_call(
        functools.partial(_da_kernel, nk=nk, kt=kt, lam_init=lam_init),
        grid=(batch, DA_HEADS, nt),
        in_specs=in_specs,
        out_specs=pl.BlockSpec((tq, 128), lambda b, h, i: (b * nt + i, h)),
        out_shape=jax.ShapeDtypeStruct((batch * nq, DA_WIDTH), BF16),
        scratch_shapes=[
            pltpu.VMEM((nkeys, 128), BF16),
            pltpu.VMEM((DA_V_DIM + ONES_ROWS, nkeys), BF16),
            pltpu.VMEM((2, 2, kt, tq), F32),
            pltpu.VMEM((2, 2, 1, tq), F32),
            pltpu.VMEM((2, DA_V_DIM + ONES_ROWS, tq), F32),
            pltpu.VMEM((2, 1, tq), F32),
        ],
        compiler_params=pltpu.CompilerParams(
            dimension_semantics=("arbitrary", "arbitrary", "arbitrary"), vmem_limit_bytes=V7X_VMEM_LIMIT),
        name="diff_attn" if nk else "diff_attn_ctx",
    )(*args)


def _ffn_kernel(x_ref, oa_ref, ob_ref, oc_ref, gate1_ref, shift2_ref, scale2_ref, gate2_ref, g2_ref,
                wout_ref, w1_ref, w3_ref, w2_ref, o_ref):
    mixed = (_dot(oa_ref[...], wout_ref[0:NA_WIDTH, :])
             + _dot(ob_ref[...], wout_ref[NA_WIDTH:NA_WIDTH + GM_WIDTH, :])
             + _dot(oc_ref[...], wout_ref[NA_WIDTH + GM_WIDTH:, :]))
    x1 = x_ref[...] + gate1_ref[0] * mixed
    h = x1 * lax.rsqrt(jnp.mean(x1 * x1, axis=-1, keepdims=True) + EPS) * g2_ref[...]
    hb = (h * (1.0 + scale2_ref[0]) + shift2_ref[0]).astype(BF16)
    a = _dot(hb, w1_ref[...])
    b = _dot(hb, w3_ref[...])
    g = (a * jax.nn.sigmoid(a) * b).astype(BF16)
    o_ref[...] = x1 + gate2_ref[0] * _dot(g, w2_ref[...])


def _out_ffn(x2d, oa, ob, oc, mod3, lw, *, batch, n, tm, per_batch_mod):
    nt = n // tm
    row = (lambda b: b) if per_batch_mod else (lambda b: CTX_MOD_ROW)
    const = lambda b, i: (0, 0)
    tok = lambda b, i: (b * nt + i, 0)
    mod_spec = lambda k: pl.BlockSpec((1, 1, D_MODEL), lambda b, i: (row(b), 0, k))
    return pl.pallas_call(
        _ffn_kernel,
        grid=(batch, nt),
        in_specs=[
            pl.BlockSpec((tm, D_MODEL), tok),
            pl.BlockSpec((tm, NA_WIDTH), tok),
            pl.BlockSpec((tm, GM_WIDTH), tok),
            pl.BlockSpec((tm, DA_WIDTH), tok),
            mod_spec(2), mod_spec(3), mod_spec(4), mod_spec(5),
            pl.BlockSpec((1, D_MODEL), const),
            _resident((D_MODEL, D_MODEL), const),
            _resident((D_MODEL, D_FF), const),
            _resident((D_MODEL, D_FF), const),
            _resident((D_FF, D_MODEL), const),
        ],
        out_specs=pl.BlockSpec((tm, D_MODEL), tok),
        out_shape=jax.ShapeDtypeStruct((batch * n, D_MODEL), F32),
        compiler_params=pltpu.CompilerParams(
            dimension_semantics=("arbitrary", "arbitrary"), vmem_limit_bytes=V7X_VMEM_LIMIT),
        name="out_ffn",
    )(x2d, oa, ob, oc, mod3, mod3, mod3, mod3, lw["norm2_g"], lw["w_out"], lw["w1"], lw["w3"], lw["w2"])


def _rope_tables(n):
    t = jnp.arange(n, dtype=jnp.int32)
    row = (t // GRID_W).astype(F32)
    col = (t % GRID_W).astype(F32)
    half = HEAD_DIM // 2
    inv = ROPE_BASE ** (-jnp.arange(0, half, 2, dtype=F32) / half)
    ar = row[:, None] * inv[None, :]
    ac = col[:, None] * inv[None, :]
    ang = jnp.concatenate([ar, ar, ac, ac], axis=-1)
    cos = jnp.cos(ang)
    sin = jnp.sin(ang)
    first = (np.arange(HEAD_DIM) % 32) < 16
    sa = jnp.where(first, -sin, 0.0)
    sb = jnp.where(first, 0.0, sin)
    return tuple(jnp.tile(a, (1, 2)) for a in (cos, sa, sb))


def _identity_rope_tables(n):
    return (jnp.ones((n, 128), F32), jnp.zeros((n, 128), F32), jnp.zeros((n, 128), F32))


def _na_bias_tables(rpb, rows):
    cq = np.arange(GRID_W)
    c0 = np.clip(cq - NA_WIN_W // 2, 0, GRID_W - NA_WIN_W)
    col_ok = (cq[None, :] >= c0[:, None]) & (cq[None, :] < c0[:, None] + NA_WIN_W)
    coff = np.clip(cq[None, :] - cq[:, None], -(NA_WIN_W - 1), NA_WIN_W - 1) + (NA_WIN_W - 1)
    col_sel = (coff[None] == np.arange(2 * NA_WIN_W - 1)[:, None, None]).astype(np.float32)
    row_sel, ok = [], []
    for r0 in (0, 2 * NA_Q_ROWS, rows - NA_Q_ROWS):
        u0 = min(max(r0 - NA_WIN_H // 2, 0), rows - NA_K_ROWS)
        r = r0 + np.arange(NA_Q_ROWS)
        key_row = u0 + np.arange(NA_K_ROWS)
        start = np.clip(r - NA_WIN_H // 2, 0, rows - NA_WIN_H)
        row_ok = (key_row[None, :] >= start[:, None]) & (key_row[None, :] < start[:, None] + NA_WIN_H)
        roff = key_row[None, :] - r[:, None] + (NA_WIN_H - 1)
        row_sel.append((roff[None] == np.arange(2 * NA_WIN_H - 1)[:, None, None]).astype(np.float32))
        ok.append(row_ok[:, None, :, None] & col_ok[None, :, None, :])
    toeplitz = jnp.einsum("hrc,cqk->hrqk", rpb.astype(F32), jnp.asarray(col_sel),
                          precision=lax.Precision.HIGHEST)
    bias = jnp.einsum("traj,hrqk->thaqjk", jnp.asarray(np.stack(row_sel)), toeplitz,
                      precision=lax.Precision.HIGHEST)
    tab = jnp.where(jnp.asarray(np.stack(ok))[:, None], bias * LOG2E, NEG_INF)
    return tab.reshape(3, NA_HEADS, NA_Q_ROWS * GRID_W, NA_K_ROWS * GRID_W)


def _layer_weights(i, p):
    tile4 = lambda g: jnp.tile(g.astype(F32), 256 // HEAD_DIM).reshape(1, 256)
    blk = np.arange(256) // HEAD_DIM
    return {
        "norm1_g": p["norm1_g"][i].reshape(1, D_MODEL),
        "w_in": p["w_in"][i].astype(BF16),
        "gqa": tile4(p["na_q_g"][i]) * (HEAD_DIM ** -0.5 * LOG2E),
        "gka": tile4(p["na_k_g"][i]),
        "gqd": tile4(p["da_q_g"][i]) * (HEAD_DIM ** -0.5 * LOG2E),
        "gkd": tile4(p["da_k_g"][i]),
        "gmat": jnp.asarray(blk[:, None] == blk[None, :], BF16),
        "gv": p["gm_v_g"][i].reshape(1, GM_WIDTH),
        "ws_cat": p["gm_ws"][i].transpose(1, 0, 2).reshape(GM_CHUNK, GM_GROUPS * GM_CHUNK).astype(BF16),
        "bs_tab": jnp.repeat(p["gm_bs"][i].T, GM_WIDTH // GM_GROUPS, axis=1),
        "lq1": p["da_lq1"][i].reshape(1, HEAD_DIM),
        "lk1": p["da_lk1"][i].reshape(1, HEAD_DIM),
        "lq2": p["da_lq2"][i].reshape(1, HEAD_DIM),
        "lk2": p["da_lk2"][i].reshape(1, HEAD_DIM),
        "sub_g": p["da_sub_g"][i].reshape(DA_V_DIM, 1),
        "norm2_g": p["norm2_g"][i].reshape(1, D_MODEL),
        "w_out": p["w_out"][i].astype(BF16),
        "w1": p["ffn_w1"][i].astype(BF16),
        "w3": p["ffn_w3"][i].astype(BF16),
        "w2": p["ffn_w2"][i].astype(BF16),
    }


def kernel(x, c, ctx, c_ctx, w_mod, b_mod, norm1_g, w_in, na_q_g, na_k_g, na_rpb, gm_v_g, gm_ws, gm_bs,
           da_q_g, da_k_g, da_lq1, da_lk1, da_lq2, da_lk2, da_sub_g, w_out, norm2_g, ffn_w1, ffn_w3, ffn_w2):
    batch, n, _ = x.shape
    depth = w_mod.shape[0]
    assert n % (NA_Q_ROWS * GRID_W) == 0 and ctx.shape[1] == CTX_LEN and batch < CTX_MOD_ROW + 1
    params = dict(norm1_g=norm1_g, w_in=w_in, na_q_g=na_q_g, na_k_g=na_k_g, gm_v_g=gm_v_g, gm_ws=gm_ws,
                  gm_bs=gm_bs, da_q_g=da_q_g, da_k_g=da_k_g, da_lq1=da_lq1, da_lk1=da_lk1, da_lq2=da_lq2,
                  da_lk2=da_lk2, da_sub_g=da_sub_g, w_out=w_out, norm2_g=norm2_g, ffn_w1=ffn_w1,
                  ffn_w3=ffn_w3, ffn_w2=ffn_w2)

    c8 = jnp.zeros((MOD_ROWS, D_MODEL), F32).at[:batch].set(c).at[CTX_MOD_ROW].set(c_ctx)
    mod_all = _modulation(c8, w_mod, b_mod)

    rope_lat = _rope_tables(n)
    rope_ctx = _identity_rope_tables(CTX_LEN)
    x2d = x.reshape(batch * n, D_MODEL)
    xc2d = ctx.reshape(batch * CTX_LEN, D_MODEL)

    for i in range(depth):
        lam_init = 0.8 - 0.6 * math.exp(-0.3 * i)
        lw = _layer_weights(i, params)
        mod3 = mod_all[i].reshape(MOD_ROWS, 1, 6 * D_MODEL)
        tab = _na_bias_tables(na_rpb[i], n // GRID_W)

        qa, qd, ka, va, kd, vdt, ob = _inproj(x2d, mod3, lw, rope_lat, batch=batch, n=n, tm=1024,
                                              per_batch_mod=True)
        qac, qdc, kac, vac, kdc, vdtc, obc = _inproj(xc2d, mod3, lw, rope_ctx, batch=batch, n=CTX_LEN,
                                                     tm=CTX_LEN, per_batch_mod=False)
        oa = _na_window(qa, ka, va, kac, vac, tab, batch=batch, n=n)
        score_bound = (1.02 * HEAD_DIM * (HEAD_DIM ** -0.5 * LOG2E)
                       * jnp.max(jnp.abs(da_q_g[i])) * jnp.max(jnp.abs(da_k_g[i]))).astype(F32)
        oc = lax.cond(
            score_bound <= DA_MAX_CONSTANT_SHIFT,
            lambda *a: _diff_attention_bounded(*a, lw, batch=batch, nq=n, nk=n, tq=512, kt=1024,
                                               lam_init=lam_init),
            lambda *a: _diff_attention(*a[:5], lw, batch=batch, nq=n, nk=n, tq=512, kt=1408,
                                       lam_init=lam_init),
            qd, kd, vdt, kdc, vdtc, score_bound.reshape(1, 1))
        x2d = _out_ffn(x2d, oa, ob, oc, mod3, lw, batch=batch, n=n, tm=512, per_batch_mod=True)
        if i < depth - 1:
            oac = _na_dense(qac, kac, vac, batch=batch)
            occ = _diff_attention(qdc, None, None, kdc, vdtc, lw, batch=batch, nq=CTX_LEN, nk=0,
                                  tq=CTX_LEN, kt=CTX_LEN, lam_init=lam_init)
            xc2d = _out_ffn(xc2d, oac, obc, occ, mod3, lw, batch=batch, n=CTX_LEN, tm=CTX_LEN,
                            per_batch_mod=False)
    return x2d.reshape(batch, n, D_MODEL)
```

```python
import functools
import math

import numpy as np
import jax
import jax.numpy as jnp
from jax import lax
from jax.experimental import pallas as pl
from jax.experimental.pallas import tpu as pltpu

D_MODEL = 1024
GRID_W = 64
CTX_LEN = 256
HEAD_DIM = 64
NA_HEADS = 4
NA_WIN_H = 8
NA_WIN_W = 16
NA_WIDTH = NA_HEADS * HEAD_DIM
GM_GROUPS = 4
GM_CHUNK = 128
GM_WIDTH = 256
DA_HEADS = 4
DA_QK_DIM = 2 * HEAD_DIM
DA_V_DIM = 2 * HEAD_DIM
DA_WIDTH = DA_HEADS * DA_V_DIM
QU_WIDTH = NA_WIDTH + DA_HEADS * DA_QK_DIM + 2 * GM_WIDTH
IN_WIDTH = QU_WIDTH + 2 * NA_WIDTH + DA_HEADS * DA_QK_DIM + DA_HEADS * DA_V_DIM
D_FF = -(-8 * D_MODEL // (3 * 256)) * 256
ROPE_BASE = 10000.0
EPS = 1e-6
NEG_INF = -1e30
LOG2E = math.log2(math.e)

COL_QA = 0
COL_QD = NA_WIDTH
COL_UV = COL_QD + DA_HEADS * DA_QK_DIM
COL_KA = QU_WIDTH
COL_VA = COL_KA + NA_WIDTH
COL_KD = COL_VA + NA_WIDTH
COL_VD = COL_KD + DA_HEADS * DA_QK_DIM

MOD_ROWS = 8
CTX_MOD_ROW = 4
NA_Q_ROWS = 4
NA_K_ROWS = 12
ONES_ROWS = 16
MAX_CONSTANT_SHIFT = 48.0
V7X_VMEM_LIMIT = 56 * 2 ** 20

F32 = jnp.float32
BF16 = jnp.bfloat16


def _dot(a, b):
    return jnp.dot(a, b, preferred_element_type=F32)


def _dot_nt(a, b):
    return lax.dot_general(a, b, (((1,), (1,)), ((), ())), preferred_element_type=F32)


def _resident(shape, index_map):
    return pl.BlockSpec(shape, index_map, pipeline_mode=pl.Buffered(1))


def _mod_kernel(c_ref, w_ref, b_ref, o_ref):
    c = c_ref[...]
    a = c * jax.nn.sigmoid(c)
    a_hi = a.astype(BF16)
    a_lo = (a - a_hi.astype(F32)).astype(BF16)
    w = w_ref[...]
    w_hi = w.astype(BF16)
    w_lo = (w - w_hi.astype(F32)).astype(BF16)
    o_ref[...] = _dot(a_hi, w_hi) + _dot(a_lo, w_hi) + _dot(a_hi, w_lo) + b_ref[...]


def _modulation(c8, w_mod, b_mod):
    depth = w_mod.shape[0]
    tn = 1024
    return pl.pallas_call(
        _mod_kernel,
        grid=(depth, 6 * D_MODEL // tn),
        in_specs=[
            pl.BlockSpec((MOD_ROWS, D_MODEL), lambda l, j: (0, 0)),
            pl.BlockSpec((None, D_MODEL, tn), lambda l, j: (l, 0, j)),
            pl.BlockSpec((None, 1, tn), lambda l, j: (l, 0, j)),
        ],
        out_specs=pl.BlockSpec((None, MOD_ROWS, tn), lambda l, j: (l, 0, j)),
        out_shape=jax.ShapeDtypeStruct((depth, MOD_ROWS, 6 * D_MODEL), F32),
        name="adaln_mod",
    )(c8, w_mod, b_mod.reshape(depth, 1, 6 * D_MODEL))


def _inproj_kernel(x_ref, shift_ref, scale_ref, g1_ref, w_ref, cos_ref, sa_ref, sb_ref,
                   gqa_ref, gka_ref, gqd_ref, gkd_ref, gmat_ref, gv_ref, ws_ref, bs_ref,
                   qa_ref, qd_ref, ka_ref, va_ref, kd_ref, vdt_ref, ob_ref, *, tm, sub):
    gmat = gmat_ref[...]
    group = lax.broadcasted_iota(jnp.int32, (GM_CHUNK, GM_WIDTH), 1) // (GM_WIDTH // GM_GROUPS)
    ws = ws_ref[...]
    bs = bs_ref[...]

    def head_norm(y, g):
        ss = _dot((y * y).astype(BF16), gmat)
        return y * lax.rsqrt(ss * (1.0 / HEAD_DIM) + EPS) * g

    for r0 in range(0, tm, sub):
        rows = slice(r0, r0 + sub)
        x = x_ref[rows, :]
        h = x * lax.rsqrt(jnp.mean(x * x, axis=-1, keepdims=True) + EPS) * g1_ref[...]
        hb = (h * (1.0 + scale_ref[0]) + shift_ref[0]).astype(BF16)
        p = _dot(hb, w_ref[...])
        cos = cos_ref[rows, :]
        sa = sa_ref[rows, :]
        sb = sb_ref[rows, :]

        def rope(z):
            return z * cos + pltpu.roll(z, 128 - 16, 1) * sa + pltpu.roll(z, 16, 1) * sb

        qa_ref[rows, :] = head_norm(p[:, COL_QA:COL_QA + 256], gqa_ref[...]).astype(BF16)
        ka_ref[rows, :] = head_norm(p[:, COL_KA:COL_KA + 256], gka_ref[...]).astype(BF16)
        va_ref[rows, :] = p[:, COL_VA:COL_VA + 256].astype(BF16)
        for c in range(2):
            yq = head_norm(p[:, COL_QD + 256 * c:COL_QD + 256 * c + 256], gqd_ref[...])
            yk = head_norm(p[:, COL_KD + 256 * c:COL_KD + 256 * c + 256], gkd_ref[...])
            for t in range(2):
                lo = 256 * c + 128 * t
                qd_ref[rows, lo:lo + 128] = rope(yq[:, 128 * t:128 * t + 128]).astype(BF16)
                kd_ref[rows, lo:lo + 128] = rope(yk[:, 128 * t:128 * t + 128]).astype(BF16)
        vdt_ref[:, rows] = p[:, COL_VD:COL_VD + 512].T.astype(BF16)

        z = jax.nn.gelu(p[:, COL_UV:COL_UV + 2 * GM_WIDTH])
        u = z[:, :GM_WIDTH]
        v = z[:, GM_WIDTH:]
        v = v * lax.rsqrt(jnp.mean(v * v, axis=-1, keepdims=True) + EPS) * gv_ref[...]
        vb = v.astype(BF16)
        for c in range(sub // GM_CHUNK):
            vc = vb[c * GM_CHUNK:(c + 1) * GM_CHUNK, :]
            vbd = jnp.concatenate([jnp.where(group == g, vc, jnp.zeros_like(vc)) for g in range(GM_GROUPS)],
                                  axis=0)
            s = _dot(ws, vbd) + bs
            lo = r0 + c * GM_CHUNK
            ob_ref[lo:lo + GM_CHUNK, :] = (u[c * GM_CHUNK:(c + 1) * GM_CHUNK, :] * s).astype(BF16)


def _inproj(x2d, mod3, lw, rope_tabs, *, batch, n, tm, per_batch_mod):
    nt = n // tm
    t_tot = batch * n
    row = (lambda b: b) if per_batch_mod else (lambda b: CTX_MOD_ROW)
    const = lambda b, i: (0, 0)
    tok = lambda b, i: (b * nt + i, 0)
    in_specs = [
        pl.BlockSpec((tm, D_MODEL), tok),
        pl.BlockSpec((1, 1, D_MODEL), lambda b, i: (row(b), 0, 0)),
        pl.BlockSpec((1, 1, D_MODEL), lambda b, i: (row(b), 0, 1)),
        pl.BlockSpec((1, D_MODEL), const),
        _resident((D_MODEL, IN_WIDTH), const),
        pl.BlockSpec((tm, 128), lambda b, i: (i, 0)),
        pl.BlockSpec((tm, 128), lambda b, i: (i, 0)),
        pl.BlockSpec((tm, 128), lambda b, i: (i, 0)),
        pl.BlockSpec((1, 256), const),
        pl.BlockSpec((1, 256), const),
        pl.BlockSpec((1, 256), const),
        pl.BlockSpec((1, 256), const),
        pl.BlockSpec((256, 256), const),
        pl.BlockSpec((1, GM_WIDTH), const),
        pl.BlockSpec((GM_CHUNK, GM_GROUPS * GM_CHUNK), const),
        pl.BlockSpec((GM_CHUNK, GM_WIDTH), const),
    ]
    out_specs = [
        pl.BlockSpec((tm, 256), tok),
        pl.BlockSpec((tm, 512), tok),
        pl.BlockSpec((tm, 256), tok),
        pl.BlockSpec((tm, 256), tok),
        pl.BlockSpec((tm, 512), tok),
        pl.BlockSpec((None, 512, tm), lambda b, i: (b, 0, i)),
        pl.BlockSpec((tm, 256), tok),
    ]
    out_shape = [
        jax.ShapeDtypeStruct((t_tot, 256), BF16),
        jax.ShapeDtypeStruct((t_tot, 512), BF16),
        jax.ShapeDtypeStruct((t_tot, 256), BF16),
        jax.ShapeDtypeStruct((t_tot, 256), BF16),
        jax.ShapeDtypeStruct((t_tot, 512), BF16),
        jax.ShapeDtypeStruct((batch, 512, n), BF16),
        jax.ShapeDtypeStruct((t_tot, 256), BF16),
    ]
    return pl.pallas_call(
        functools.partial(_inproj_kernel, tm=tm, sub=min(tm, 256)),
        grid=(batch, nt),
        in_specs=in_specs,
        out_specs=out_specs,
        out_shape=out_shape,
        compiler_params=pltpu.CompilerParams(
            dimension_semantics=("arbitrary", "arbitrary"), vmem_limit_bytes=V7X_VMEM_LIMIT),
        name="inproj",
    )(x2d, mod3, mod3, lw["norm1_g"], lw["w_in"], *rope_tabs,
      lw["gqa"], lw["gka"], lw["gqd"], lw["gkd"], lw["gmat"], lw["gv"], lw["ws_cat"], lw["bs_tab"])


def _na_kernel(*refs, has_window, bounded=False):
    if bounded:
        q_ref, k_ref, v_ref, kc_ref, vc_ref, tab_ref, bound_ref, o_ref = refs
        refs = refs[:6] + refs[7:]
    if has_window:
        q_ref, k_ref, v_ref, kc_ref, vc_ref, tab_ref, o_ref = refs
        i = pl.program_id(1)
        u0 = jnp.clip(NA_Q_ROWS * i - NA_WIN_H // 2, 0, k_ref.shape[0] // GRID_W - NA_K_ROWS)
        off = pl.multiple_of(u0 * GRID_W, GRID_W)
        kw = k_ref[pl.ds(off, NA_K_ROWS * GRID_W), :]
        vw = v_ref[pl.ds(off, NA_K_ROWS * GRID_W), :]
    else:
        q_ref, kc_ref, vc_ref, o_ref = refs
    lane = lax.broadcasted_iota(jnp.int32, (1, 128), 1)
    lane_head = lane // HEAD_DIM
    for p in range(NA_HEADS // 2):
        cols = slice(128 * p, 128 * p + 128)
        qp = q_ref[:, cols]
        kcp = kc_ref[:, cols]
        vcp = vc_ref[:, cols]
        acc = jnp.zeros((qp.shape[0], 128), F32)
        for j in range(2):
            sel = (lane_head == j).astype(BF16)
            qm = qp * sel
            sc = _dot_nt(qm, kcp)
            if has_window:
                sw = _dot_nt(qm, kw[:, cols]) + tab_ref[2 * p + j]
            if bounded:
                ec = jnp.exp2(sc - bound_ref[...])
                ew = jnp.exp2(sw)
                l = jnp.sum(ec, axis=1, keepdims=True) + jnp.sum(ew, axis=1, keepdims=True)
                o = _dot(ec.astype(BF16), vcp * sel) + _dot(ew.astype(BF16), vw[:, cols] * sel)
                acc = acc + o / l
                continue
            m = jnp.max(sc, axis=1, keepdims=True)
            if has_window:
                m = jnp.maximum(m, jnp.max(sw, axis=1, keepdims=True))
            ec = jnp.exp2(sc - m)
            l = jnp.sum(ec, axis=1, keepdims=True)
            o = _dot(ec.astype(BF16), vcp * sel)
            if has_window:
                ew = jnp.exp2(sw - m)
                l = l + jnp.sum(ew, axis=1, keepdims=True)
                o = o + _dot(ew.astype(BF16), vw[:, cols] * sel)
            acc = acc + o / l
        o_ref[:, cols] = acc.astype(BF16)


def _na_window(qa, ka, va, kac, vac, tab, bound=None, *, batch, n):
    tq = NA_Q_ROWS * GRID_W
    nt = n // tq
    in_specs = [
        pl.BlockSpec((tq, 256), lambda b, i: (b * nt + i, 0)),
        pl.BlockSpec((n, 256), lambda b, i: (b, 0)),
        pl.BlockSpec((n, 256), lambda b, i: (b, 0)),
        pl.BlockSpec((CTX_LEN, 256), lambda b, i: (b, 0)),
        pl.BlockSpec((CTX_LEN, 256), lambda b, i: (b, 0)),
        pl.BlockSpec((None, NA_HEADS, tq, NA_K_ROWS * GRID_W),
                     lambda b, i: (jnp.where(i == 0, 0, jnp.where(i == nt - 1, 2, 1)), 0, 0, 0)),
    ]
    args = [qa, ka, va, kac, vac, tab]
    if bound is not None:
        in_specs.append(pl.BlockSpec((1, 1), lambda b, i: (0, 0)))
        args.append(bound)
    return pl.pallas_call(
        functools.partial(_na_kernel, has_window=True, bounded=bound is not None),
        grid=(batch, nt),
        in_specs=in_specs,
        out_specs=pl.BlockSpec((tq, 256), lambda b, i: (b * nt + i, 0)),
        out_shape=jax.ShapeDtypeStruct((batch * n, 256), BF16),
        compiler_params=pltpu.CompilerParams(
            dimension_semantics=("arbitrary", "arbitrary"), vmem_limit_bytes=V7X_VMEM_LIMIT),
        name="na_window_bounded" if bound is not None else "na_window",
    )(*args)


def _na_dense(qac, kac, vac, *, batch):
    spec = pl.BlockSpec((CTX_LEN, 256), lambda b: (b, 0))
    return pl.pallas_call(
        functools.partial(_na_kernel, has_window=False),
        grid=(batch,),
        in_specs=[spec, spec, spec],
        out_specs=spec,
        out_shape=jax.ShapeDtypeStruct((batch * CTX_LEN, 256), BF16),
        name="na_dense",
    )(qac, kac, vac)


def _da_kernel(*refs, nk, kt, lam_init):
    if nk:
        q_ref, k_ref, vt_ref, kc_ref, vtc_ref, lq1, lk1, lq2, lk2, subg_ref, o_ref = refs[:11]
    else:
        q_ref, kc_ref, vtc_ref, lq1, lk1, lq2, lk2, subg_ref, o_ref = refs[:9]
    kall, vtall, s_buf, bm_buf, acc_ref, m_ref = refs[-6:]
    tq = o_ref.shape[0]
    nt = q_ref.shape[0] // tq
    nblk = (nk + CTX_LEN) // kt
    i = pl.program_id(2)
    lane_map = lax.broadcasted_iota(jnp.int32, (1, 128), 1) // HEAD_DIM
    sel = [(lane_map == mi).astype(BF16) for mi in range(2)]

    def scores(tile, blk, slot):
        q = q_ref[pl.ds(pl.multiple_of(tile * tq, tq), tq), :]
        kblk = kall[pl.ds(pl.multiple_of(blk * kt, kt), kt), :]
        for mi in range(2):
            s = _dot_nt(kblk, q * sel[mi])
            s_buf[slot, mi] = s
            bm_buf[slot, mi] = jnp.max(s, axis=0, keepdims=True)

    @pl.when(i == 0)
    def _():
        if nk:
            kall[0:nk, :] = k_ref[...]
            vtall[0:DA_V_DIM, 0:nk] = vt_ref[...]
        kall[nk:nk + CTX_LEN, :] = kc_ref[...]
        vtall[0:DA_V_DIM, nk:nk + CTX_LEN] = vtc_ref[...]
        vtall[DA_V_DIM:, :] = jnp.ones((ONES_ROWS, nk + CTX_LEN), BF16)
        scores(0, 0, 0)

    acc_ref[...] = jnp.zeros_like(acc_ref)
    m_ref[...] = jnp.full_like(m_ref, NEG_INF)

    def softmax_pv(blk, slot):
        vte = vtall[:, pl.ds(pl.multiple_of(blk * kt, kt), kt)]
        for mi in range(2):
            for g in range(tq // 256):
                cols = slice(256 * g, 256 * g + 256)
                m_old = m_ref[mi, :, cols]
                m_new = jnp.maximum(m_old, bm_buf[slot, mi, :, cols])
                alpha = jnp.exp2(m_old - m_new)
                e = jnp.exp2(s_buf[slot, mi, :, cols] - m_new).astype(BF16)
                acc_ref[mi, :, cols] = acc_ref[mi, :, cols] * alpha + _dot(vte, e)
                m_ref[mi, :, cols] = m_new

    if nblk == 1:
        softmax_pv(0, 0)
    else:
        for blk in range(nblk):
            if blk + 1 < nblk:
                scores(i, blk + 1, (blk + 1) % 2)
            else:
                scores(jnp.minimum(i + 1, nt - 1), 0, 0)
            softmax_pv(blk, blk % 2)

    _da_finalize(acc_ref[0], acc_ref[1], lq1, lk1, lq2, lk2, subg_ref, o_ref, lam_init)


def _da_finalize(a0, a1, lq1, lk1, lq2, lk2, subg_ref, o_ref, lam_init):
    lam = (jnp.exp(jnp.sum(lq1[...] * lk1[...], keepdims=True))
           - jnp.exp(jnp.sum(lq2[...] * lk2[...], keepdims=True)) + lam_init)
    o = a0[:DA_V_DIM] / a0[DA_V_DIM:DA_V_DIM + 1] - lam * (a1[:DA_V_DIM] / a1[DA_V_DIM:DA_V_DIM + 1])
    y = o * lax.rsqrt(jnp.mean(o * o, axis=0, keepdims=True) + EPS) * subg_ref[...] * (1.0 - lam_init)
    o_ref[...] = y.T.astype(BF16)


def _da_bounded_kernel(q_ref, k_ref, vt_ref, kc_ref, vtc_ref, bound_ref, lq1, lk1, lq2, lk2, subg_ref, o_ref,
                       *, kt, lam_init):
    q = q_ref[...]
    lane_map = lax.broadcasted_iota(jnp.int32, (1, 128), 1) // HEAD_DIM
    qms = [q * (lane_map == mi).astype(BF16) for mi in range(2)]
    shift = bound_ref[...]
    acc = [None, None]

    def block(kblk, vtblk):
        vte = jnp.concatenate([vtblk, jnp.ones((ONES_ROWS, vtblk.shape[1]), BF16)], axis=0)
        for mi in range(2):
            e = jnp.exp2(_dot_nt(kblk, qms[mi]) - shift).astype(BF16)
            d = _dot(vte, e)
            acc[mi] = d if acc[mi] is None else acc[mi] + d

    for j in range(k_ref.shape[0] // kt):
        block(k_ref[j * kt:(j + 1) * kt, :], vt_ref[:, j * kt:(j + 1) * kt])
    block(kc_ref[...], vtc_ref[...])
    _da_finalize(acc[0], acc[1], lq1, lk1, lq2, lk2, subg_ref, o_ref, lam_init)


def _diff_attention_bounded(qd, kd, vdt, kdc, vdtc, bound, lw, *, batch, nq, nk, tq, kt, lam_init):
    nt = nq // tq
    assert nk % kt == 0 and nq % tq == 0
    small = lambda shape: pl.BlockSpec(shape, lambda b, h, i: (0, 0))
    return pl.pallas_call(
        functools.partial(_da_bounded_kernel, kt=kt, lam_init=lam_init),
        grid=(batch, DA_HEADS, nt),
        in_specs=[
            pl.BlockSpec((tq, 128), lambda b, h, i: (b * nt + i, h)),
            pl.BlockSpec((nk, 128), lambda b, h, i: (b, h)),
            pl.BlockSpec((None, 128, nk), lambda b, h, i: (b, h, 0)),
            pl.BlockSpec((CTX_LEN, 128), lambda b, h, i: (b, h)),
            pl.BlockSpec((None, 128, CTX_LEN), lambda b, h, i: (b, h, 0)),
            small((1, 1)), small((1, HEAD_DIM)), small((1, HEAD_DIM)), small((1, HEAD_DIM)), small((1, HEAD_DIM)),
            small((DA_V_DIM, 1)),
        ],
        out_specs=pl.BlockSpec((tq, 128), lambda b, h, i: (b * nt + i, h)),
        out_shape=jax.ShapeDtypeStruct((batch * nq, DA_WIDTH), BF16),
        compiler_params=pltpu.CompilerParams(
            dimension_semantics=("arbitrary", "arbitrary", "arbitrary"), vmem_limit_bytes=V7X_VMEM_LIMIT),
        name="diff_attn_bounded",
    )(qd, kd, vdt, kdc, vdtc, bound, lw["lq1"], lw["lk1"], lw["lq2"], lw["lk2"], lw["sub_g"])


def _diff_attention(qd, kd, vdt, kdc, vdtc, lw, *, batch, nq, nk, tq, kt, lam_init):
    nt = nq // tq
    nkeys = nk + CTX_LEN
    nblk = nkeys // kt
    assert nkeys % kt == 0 and tq % 256 == 0 and nq % tq == 0 and (nblk == 1 and nt == 1 or nblk % 2 == 0)
    in_specs = [pl.BlockSpec((nq, 128), lambda b, h, i: (b, h))]
    args = [qd]
    if nk:
        in_specs += [pl.BlockSpec((nk, 128), lambda b, h, i: (b, h)),
                     pl.BlockSpec((None, 128, nk), lambda b, h, i: (b, h, 0))]
        args += [kd, vdt]
    in_specs += [pl.BlockSpec((CTX_LEN, 128), lambda b, h, i: (b, h)),
                 pl.BlockSpec((None, 128, CTX_LEN), lambda b, h, i: (b, h, 0))]
    args += [kdc, vdtc]
    in_specs += [pl.BlockSpec((1, HEAD_DIM), lambda b, h, i: (0, 0))] * 4
    args += [lw["lq1"], lw["lk1"], lw["lq2"], lw["lk2"]]
    in_specs += [pl.BlockSpec((DA_V_DIM, 1), lambda b, h, i: (0, 0))]
    args += [lw["sub_g"]]
    return pl.pallas_call(
        functools.partial(_da_kernel, nk=nk, kt=kt, lam_init=lam_init),
        grid=(batch, DA_HEADS, nt),
        in_specs=in_specs,
        out_specs=pl.BlockSpec((tq, 128), lambda b, h, i: (b * nt + i, h)),
        out_shape=jax.ShapeDtypeStruct((batch * nq, DA_WIDTH), BF16),
        scratch_shapes=[
            pltpu.VMEM((nkeys, 128), BF16),
            pltpu.VMEM((DA_V_DIM + ONES_ROWS, nkeys), BF16),
            pltpu.VMEM((2, 2, kt, tq), F32),
            pltpu.VMEM((2, 2, 1, tq), F32),
            pltpu.VMEM((2, DA_V_DIM + ONES_ROWS, tq), F32),
            pltpu.VMEM((2, 1, tq), F32),
        ],
        compiler_params=pltpu.CompilerParams(
            dimension_semantics=("arbitrary", "arbitrary", "arbitrary"), vmem_limit_bytes=V7X_VMEM_LIMIT),
        name="diff_attn" if nk else "diff_attn_ctx",
    )(*args)


def _ffn_kernel(x_ref, oa_ref, ob_ref, oc_ref, gate1_ref, shift2_ref, scale2_ref, gate2_ref, g2_ref,
                wout_ref, w1_ref, w3_ref, w2_ref, o_ref):
    mixed = (_dot(oa_ref[...], wout_ref[0:NA_WIDTH, :])
             + _dot(ob_ref[...], wout_ref[NA_WIDTH:NA_WIDTH + GM_WIDTH, :])
             + _dot(oc_ref[...], wout_ref[NA_WIDTH + GM_WIDTH:, :]))
    x1 = x_ref[...] + gate1_ref[0] * mixed
    h = x1 * lax.rsqrt(jnp.mean(x1 * x1, axis=-1, keepdims=True) + EPS) * g2_ref[...]
    hb = (h * (1.0 + scale2_ref[0]) + shift2_ref[0]).astype(BF16)
    a = _dot(hb, w1_ref[...])
    b = _dot(hb, w3_ref[...])
    g = (a * jax.nn.sigmoid(a) * b).astype(BF16)
    o_ref[...] = x1 + gate2_ref[0] * _dot(g, w2_ref[...])


def _out_ffn(x2d, oa, ob, oc, mod3, lw, *, batch, n, tm, per_batch_mod):
    nt = n // tm
    row = (lambda b: b) if per_batch_mod else (lambda b: CTX_MOD_ROW)
    const = lambda b, i: (0, 0)
    tok = lambda b, i: (b * nt + i, 0)
    mod_spec = lambda k: pl.BlockSpec((1, 1, D_MODEL), lambda b, i: (row(b), 0, k))
    return pl.pallas_call(
        _ffn_kernel,
        grid=(batch, nt),
        in_specs=[
            pl.BlockSpec((tm, D_MODEL), tok),
            pl.BlockSpec((tm, NA_WIDTH), tok),
            pl.BlockSpec((tm, GM_WIDTH), tok),
            pl.BlockSpec((tm, DA_WIDTH), tok),
            mod_spec(2), mod_spec(3), mod_spec(4), mod_spec(5),
            pl.BlockSpec((1, D_MODEL), const),
            _resident((D_MODEL, D_MODEL), const),
            _resident((D_MODEL, D_FF), const),
            _resident((D_MODEL, D_FF), const),
            _resident((D_FF, D_MODEL), const),
        ],
        out_specs=pl.BlockSpec((tm, D_MODEL), tok),
        out_shape=jax.ShapeDtypeStruct((batch * n, D_MODEL), F32),
        compiler_params=pltpu.CompilerParams(
            dimension_semantics=("arbitrary", "arbitrary"), vmem_limit_bytes=V7X_VMEM_LIMIT),
        name="out_ffn",
    )(x2d, oa, ob, oc, mod3, mod3, mod3, mod3, lw["norm2_g"], lw["w_out"], lw["w1"], lw["w3"], lw["w2"])


def _rope_tables(n):
    t = jnp.arange(n, dtype=jnp.int32)
    row = (t // GRID_W).astype(F32)
    col = (t % GRID_W).astype(F32)
    half = HEAD_DIM // 2
    inv = ROPE_BASE ** (-jnp.arange(0, half, 2, dtype=F32) / half)
    ar = row[:, None] * inv[None, :]
    ac = col[:, None] * inv[None, :]
    ang = jnp.concatenate([ar, ar, ac, ac], axis=-1)
    cos = jnp.cos(ang)
    sin = jnp.sin(ang)
    first = (np.arange(HEAD_DIM) % 32) < 16
    sa = jnp.where(first, -sin, 0.0)
    sb = jnp.where(first, 0.0, sin)
    return tuple(jnp.tile(a, (1, 2)) for a in (cos, sa, sb))


def _identity_rope_tables(n):
    return (jnp.ones((n, 128), F32), jnp.zeros((n, 128), F32), jnp.zeros((n, 128), F32))


def _na_bias_tables(rpb, rows):
    cq = np.arange(GRID_W)
    c0 = np.clip(cq - NA_WIN_W // 2, 0, GRID_W - NA_WIN_W)
    col_ok = (cq[None, :] >= c0[:, None]) & (cq[None, :] < c0[:, None] + NA_WIN_W)
    coff = np.clip(cq[None, :] - cq[:, None], -(NA_WIN_W - 1), NA_WIN_W - 1) + (NA_WIN_W - 1)
    col_sel = (coff[None] == np.arange(2 * NA_WIN_W - 1)[:, None, None]).astype(np.float32)
    row_sel, ok = [], []
    for r0 in (0, 2 * NA_Q_ROWS, rows - NA_Q_ROWS):
        u0 = min(max(r0 - NA_WIN_H // 2, 0), rows - NA_K_ROWS)
        r = r0 + np.arange(NA_Q_ROWS)
        key_row = u0 + np.arange(NA_K_ROWS)
        start = np.clip(r - NA_WIN_H // 2, 0, rows - NA_WIN_H)
        row_ok = (key_row[None, :] >= start[:, None]) & (key_row[None, :] < start[:, None] + NA_WIN_H)
        roff = key_row[None, :] - r[:, None] + (NA_WIN_H - 1)
        row_sel.append((roff[None] == np.arange(2 * NA_WIN_H - 1)[:, None, None]).astype(np.float32))
        ok.append(row_ok[:, None, :, None] & col_ok[None, :, None, :])
    toeplitz = jnp.einsum("hrc,cqk->hrqk", rpb.astype(F32), jnp.asarray(col_sel),
                          precision=lax.Precision.HIGHEST)
    bias = jnp.einsum("traj,hrqk->thaqjk", jnp.asarray(np.stack(row_sel)), toeplitz,
                      precision=lax.Precision.HIGHEST)
    tab = jnp.where(jnp.asarray(np.stack(ok))[:, None], bias * LOG2E, NEG_INF)
    return tab.reshape(3, NA_HEADS, NA_Q_ROWS * GRID_W, NA_K_ROWS * GRID_W)


def _layer_weights(i, p):
    tile4 = lambda g: jnp.tile(g.astype(F32), 256 // HEAD_DIM).reshape(1, 256)
    blk = np.arange(256) // HEAD_DIM
    return {
        "norm1_g": p["norm1_g"][i].reshape(1, D_MODEL),
        "w_in": p["w_in"][i].astype(BF16),
        "gqa": tile4(p["na_q_g"][i]) * (HEAD_DIM ** -0.5 * LOG2E),
        "gka": tile4(p["na_k_g"][i]),
        "gqd": tile4(p["da_q_g"][i]) * (HEAD_DIM ** -0.5 * LOG2E),
        "gkd": tile4(p["da_k_g"][i]),
        "gmat": jnp.asarray(blk[:, None] == blk[None, :], BF16),
        "gv": p["gm_v_g"][i].reshape(1, GM_WIDTH),
        "ws_cat": p["gm_ws"][i].transpose(1, 0, 2).reshape(GM_CHUNK, GM_GROUPS * GM_CHUNK).astype(BF16),
        "bs_tab": jnp.repeat(p["gm_bs"][i].T, GM_WIDTH // GM_GROUPS, axis=1),
        "lq1": p["da_lq1"][i].reshape(1, HEAD_DIM),
        "lk1": p["da_lk1"][i].reshape(1, HEAD_DIM),
        "lq2": p["da_lq2"][i].reshape(1, HEAD_DIM),
        "lk2": p["da_lk2"][i].reshape(1, HEAD_DIM),
        "sub_g": p["da_sub_g"][i].reshape(DA_V_DIM, 1),
        "norm2_g": p["norm2_g"][i].reshape(1, D_MODEL),
        "w_out": p["w_out"][i].astype(BF16),
        "w1": p["ffn_w1"][i].astype(BF16),
        "w3": p["ffn_w3"][i].astype(BF16),
        "w2": p["ffn_w2"][i].astype(BF16),
    }


def kernel(x, c, ctx, c_ctx, w_mod, b_mod, norm1_g, w_in, na_q_g, na_k_g, na_rpb, gm_v_g, gm_ws, gm_bs,
           da_q_g, da_k_g, da_lq1, da_lk1, da_lq2, da_lk2, da_sub_g, w_out, norm2_g, ffn_w1, ffn_w3, ffn_w2):
    batch, n, _ = x.shape
    depth = w_mod.shape[0]
    assert n % (NA_Q_ROWS * GRID_W) == 0 and ctx.shape[1] == CTX_LEN and batch < CTX_MOD_ROW + 1
    params = dict(norm1_g=norm1_g, w_in=w_in, na_q_g=na_q_g, na_k_g=na_k_g, gm_v_g=gm_v_g, gm_ws=gm_ws,
                  gm_bs=gm_bs, da_q_g=da_q_g, da_k_g=da_k_g, da_lq1=da_lq1, da_lk1=da_lk1, da_lq2=da_lq2,
                  da_lk2=da_lk2, da_sub_g=da_sub_g, w_out=w_out, norm2_g=norm2_g, ffn_w1=ffn_w1,
                  ffn_w3=ffn_w3, ffn_w2=ffn_w2)

    c8 = jnp.zeros((MOD_ROWS, D_MODEL), F32).at[:batch].set(c).at[CTX_MOD_ROW].set(c_ctx)
    mod_all = _modulation(c8, w_mod, b_mod)

    rope_lat = _rope_tables(n)
    rope_ctx = _identity_rope_tables(CTX_LEN)
    x2d = x.reshape(batch * n, D_MODEL)
    xc2d = ctx.reshape(batch * CTX_LEN, D_MODEL)

    for i in range(depth):
        lam_init = 0.8 - 0.6 * math.exp(-0.3 * i)
        lw = _layer_weights(i, params)
        mod3 = mod_all[i].reshape(MOD_ROWS, 1, 6 * D_MODEL)
        tab = _na_bias_tables(na_rpb[i], n // GRID_W)

        qa, qd, ka, va, kd, vdt, ob = _inproj(x2d, mod3, lw, rope_lat, batch=batch, n=n, tm=1024,
                                              per_batch_mod=True)
        qac, qdc, kac, vac, kdc, vdtc, obc = _inproj(xc2d, mod3, lw, rope_ctx, batch=batch, n=CTX_LEN,
                                                     tm=CTX_LEN, per_batch_mod=False)
        na_bound = (1.02 * HEAD_DIM * (HEAD_DIM ** -0.5 * LOG2E)
                    * jnp.max(jnp.abs(na_q_g[i])) * jnp.max(jnp.abs(na_k_g[i]))
                    + LOG2E * jnp.max(jnp.abs(na_rpb[i]))).astype(F32)
        oa = lax.cond(
            na_bound <= MAX_CONSTANT_SHIFT,
            lambda *a: _na_window(*a[:5], a[5] - a[6], a[6].reshape(1, 1), batch=batch, n=n),
            lambda *a: _na_window(*a[:6], batch=batch, n=n),
            qa, ka, va, kac, vac, tab, na_bound)
        score_bound = (1.02 * HEAD_DIM * (HEAD_DIM ** -0.5 * LOG2E)
                       * jnp.max(jnp.abs(da_q_g[i])) * jnp.max(jnp.abs(da_k_g[i]))).astype(F32)
        oc = lax.cond(
            score_bound <= MAX_CONSTANT_SHIFT,
            lambda *a: _diff_attention_bounded(*a, lw, batch=batch, nq=n, nk=n, tq=1024, kt=1024,
                                               lam_init=lam_init),
            lambda *a: _diff_attention(*a[:5], lw, batch=batch, nq=n, nk=n, tq=512, kt=1408,
                                       lam_init=lam_init),
            qd, kd, vdt, kdc, vdtc, score_bound.reshape(1, 1))
        x2d = _out_ffn(x2d, oa, ob, oc, mod3, lw, batch=batch, n=n, tm=512, per_batch_mod=True)
        if i < depth - 1:
            oac = _na_dense(qac, kac, vac, batch=batch)
            occ = _diff_attention(qdc, None, None, kdc, vdtc, lw, batch=batch, nq=CTX_LEN, nk=0,
                                  tq=CTX_LEN, kt=CTX_LEN, lam_init=lam_init)
            xc2d = _out_ffn(xc2d, oac, obc, occ, mod3, lw, batch=batch, n=CTX_LEN, tm=CTX_LEN,
                            per_batch_mod=False)
    return x2d.reshape(batch, n, D_MODEL)
```

```python
import functools
import math

import numpy as np
import jax
import jax.numpy as jnp
from jax import lax
from jax.experimental import pallas as pl
from jax.experimental.pallas import tpu as pltpu

D_MODEL = 1024
GRID_W = 64
CTX_LEN = 256
HEAD_DIM = 64
NA_HEADS = 4
NA_WIN_H = 8
NA_WIN_W = 16
NA_WIDTH = NA_HEADS * HEAD_DIM
GM_GROUPS = 4
GM_CHUNK = 128
GM_WIDTH = 256
DA_HEADS = 4
DA_QK_DIM = 2 * HEAD_DIM
DA_V_DIM = 2 * HEAD_DIM
DA_WIDTH = DA_HEADS * DA_V_DIM
QU_WIDTH = NA_WIDTH + DA_HEADS * DA_QK_DIM + 2 * GM_WIDTH
IN_WIDTH = QU_WIDTH + 2 * NA_WIDTH + DA_HEADS * DA_QK_DIM + DA_HEADS * DA_V_DIM
D_FF = -(-8 * D_MODEL // (3 * 256)) * 256
ROPE_BASE = 10000.0
EPS = 1e-6
NEG_INF = -1e30
LOG2E = math.log2(math.e)

COL_QA = 0
COL_QD = NA_WIDTH
COL_UV = COL_QD + DA_HEADS * DA_QK_DIM
COL_KA = QU_WIDTH
COL_VA = COL_KA + NA_WIDTH
COL_KD = COL_VA + NA_WIDTH
COL_VD = COL_KD + DA_HEADS * DA_QK_DIM

MOD_ROWS = 8
CTX_MOD_ROW = 4
NA_Q_ROWS = 4
NA_K_ROWS = 12
ONES_ROWS = 16
MAX_CONSTANT_SHIFT = 48.0

V7X_VMEM_LIMIT = 56 * 2 ** 20
INPROJ_TM = 1024
INPROJ_SUB = 256
FFN_TM = 512
DA_BOUNDED_TQ = 1024
DA_BOUNDED_KT = 1024
DA_ONLINE_TQ = 512
DA_ONLINE_KT = 1408


def _score_bound(gq, gk):
    return 1.02 * HEAD_DIM * (HEAD_DIM ** -0.5 * LOG2E) * jnp.max(jnp.abs(gq)) * jnp.max(jnp.abs(gk))

F32 = jnp.float32
BF16 = jnp.bfloat16


def _dot(a, b):
    return jnp.dot(a, b, preferred_element_type=F32)


def _dot_nt(a, b):
    return lax.dot_general(a, b, (((1,), (1,)), ((), ())), preferred_element_type=F32)


def _resident(shape, index_map):
    return pl.BlockSpec(shape, index_map, pipeline_mode=pl.Buffered(1))


def _mod_kernel(c_ref, w_ref, b_ref, o_ref):
    c = c_ref[...]
    a = c * jax.nn.sigmoid(c)
    a_hi = a.astype(BF16)
    a_lo = (a - a_hi.astype(F32)).astype(BF16)
    w = w_ref[...]
    w_hi = w.astype(BF16)
    w_lo = (w - w_hi.astype(F32)).astype(BF16)
    o_ref[...] = _dot(a_hi, w_hi) + _dot(a_lo, w_hi) + _dot(a_hi, w_lo) + b_ref[...]


def _modulation(c8, w_mod, b_mod):
    depth = w_mod.shape[0]
    tn = 1024
    return pl.pallas_call(
        _mod_kernel,
        grid=(depth, 6 * D_MODEL // tn),
        in_specs=[
            pl.BlockSpec((MOD_ROWS, D_MODEL), lambda l, j: (0, 0)),
            pl.BlockSpec((None, D_MODEL, tn), lambda l, j: (l, 0, j)),
            pl.BlockSpec((None, 1, tn), lambda l, j: (l, 0, j)),
        ],
        out_specs=pl.BlockSpec((None, MOD_ROWS, tn), lambda l, j: (l, 0, j)),
        out_shape=jax.ShapeDtypeStruct((depth, MOD_ROWS, 6 * D_MODEL), F32),
        name="adaln_mod",
    )(c8, w_mod, b_mod.reshape(depth, 1, 6 * D_MODEL))


def _inproj_kernel(x_ref, shift_ref, scale_ref, g1_ref, w_ref, cos_ref, sa_ref, sb_ref,
                   gqa_ref, gka_ref, gqd_ref, gkd_ref, gmat_ref, gv_ref, ws_ref, bs_ref,
                   qa_ref, qd_ref, ka_ref, va_ref, kd_ref, vdt_ref, ob_ref, *, tm, sub):
    gmat = gmat_ref[...]
    group = lax.broadcasted_iota(jnp.int32, (GM_CHUNK, GM_WIDTH), 1) // (GM_WIDTH // GM_GROUPS)
    ws = ws_ref[...]
    bs = bs_ref[...]

    def head_norm(y, g):
        ss = _dot((y * y).astype(BF16), gmat)
        return y * lax.rsqrt(ss * (1.0 / HEAD_DIM) + EPS) * g

    for r0 in range(0, tm, sub):
        rows = slice(r0, r0 + sub)
        x = x_ref[rows, :]
        h = x * lax.rsqrt(jnp.mean(x * x, axis=-1, keepdims=True) + EPS) * g1_ref[...]
        hb = (h * (1.0 + scale_ref[0]) + shift_ref[0]).astype(BF16)
        p = _dot(hb, w_ref[...])
        cos = cos_ref[rows, :]
        sa = sa_ref[rows, :]
        sb = sb_ref[rows, :]

        def rope(z):
            return z * cos + pltpu.roll(z, 128 - 16, 1) * sa + pltpu.roll(z, 16, 1) * sb

        qa_ref[rows, :] = head_norm(p[:, COL_QA:COL_QA + 256], gqa_ref[...]).astype(BF16)
        ka_ref[rows, :] = head_norm(p[:, COL_KA:COL_KA + 256], gka_ref[...]).astype(BF16)
        va_ref[rows, :] = p[:, COL_VA:COL_VA + 256].astype(BF16)
        for c in range(2):
            yq = head_norm(p[:, COL_QD + 256 * c:COL_QD + 256 * c + 256], gqd_ref[...])
            yk = head_norm(p[:, COL_KD + 256 * c:COL_KD + 256 * c + 256], gkd_ref[...])
            for t in range(2):
                lo = 256 * c + 128 * t
                qd_ref[rows, lo:lo + 128] = rope(yq[:, 128 * t:128 * t + 128]).astype(BF16)
                kd_ref[rows, lo:lo + 128] = rope(yk[:, 128 * t:128 * t + 128]).astype(BF16)
        vdt_ref[:, rows] = p[:, COL_VD:COL_VD + 512].T.astype(BF16)

        z = jax.nn.gelu(p[:, COL_UV:COL_UV + 2 * GM_WIDTH])
        u = z[:, :GM_WIDTH]
        v = z[:, GM_WIDTH:]
        v = v * lax.rsqrt(jnp.mean(v * v, axis=-1, keepdims=True) + EPS) * gv_ref[...]
        vb = v.astype(BF16)
        for c in range(sub // GM_CHUNK):
            vc = vb[c * GM_CHUNK:(c + 1) * GM_CHUNK, :]
            vbd = jnp.concatenate([jnp.where(group == g, vc, jnp.zeros_like(vc)) for g in range(GM_GROUPS)],
                                  axis=0)
            s = _dot(ws, vbd) + bs
            lo = r0 + c * GM_CHUNK
            ob_ref[lo:lo + GM_CHUNK, :] = (u[c * GM_CHUNK:(c + 1) * GM_CHUNK, :] * s).astype(BF16)


def _inproj(x2d, mod3, lw, rope_tabs, *, batch, n, tm, per_batch_mod):
    nt = n // tm
    t_tot = batch * n
    row = (lambda b: b) if per_batch_mod else (lambda b: CTX_MOD_ROW)
    const = lambda b, i: (0, 0)
    tok = lambda b, i: (b * nt + i, 0)
    in_specs = [
        pl.BlockSpec((tm, D_MODEL), tok),
        pl.BlockSpec((1, 1, D_MODEL), lambda b, i: (row(b), 0, 0)),
        pl.BlockSpec((1, 1, D_MODEL), lambda b, i: (row(b), 0, 1)),
        pl.BlockSpec((1, D_MODEL), const),
        _resident((D_MODEL, IN_WIDTH), const),
        pl.BlockSpec((tm, 128), lambda b, i: (i, 0)),
        pl.BlockSpec((tm, 128), lambda b, i: (i, 0)),
        pl.BlockSpec((tm, 128), lambda b, i: (i, 0)),
        pl.BlockSpec((1, 256), const),
        pl.BlockSpec((1, 256), const),
        pl.BlockSpec((1, 256), const),
        pl.BlockSpec((1, 256), const),
        pl.BlockSpec((256, 256), const),
        pl.BlockSpec((1, GM_WIDTH), const),
        pl.BlockSpec((GM_CHUNK, GM_GROUPS * GM_CHUNK), const),
        pl.BlockSpec((GM_CHUNK, GM_WIDTH), const),
    ]
    out_specs = [
        pl.BlockSpec((tm, 256), tok),
        pl.BlockSpec((tm, 512), tok),
        pl.BlockSpec((tm, 256), tok),
        pl.BlockSpec((tm, 256), tok),
        pl.BlockSpec((tm, 512), tok),
        pl.BlockSpec((None, 512, tm), lambda b, i: (b, 0, i)),
        pl.BlockSpec((tm, 256), tok),
    ]
    out_shape = [
        jax.ShapeDtypeStruct((t_tot, 256), BF16),
        jax.ShapeDtypeStruct((t_tot, 512), BF16),
        jax.ShapeDtypeStruct((t_tot, 256), BF16),
        jax.ShapeDtypeStruct((t_tot, 256), BF16),
        jax.ShapeDtypeStruct((t_tot, 512), BF16),
        jax.ShapeDtypeStruct((batch, 512, n), BF16),
        jax.ShapeDtypeStruct((t_tot, 256), BF16),
    ]
    return pl.pallas_call(
        functools.partial(_inproj_kernel, tm=tm, sub=min(tm, INPROJ_SUB)),
        grid=(batch, nt),
        in_specs=in_specs,
        out_specs=out_specs,
        out_shape=out_shape,
        compiler_params=pltpu.CompilerParams(
            dimension_semantics=("arbitrary", "arbitrary"), vmem_limit_bytes=V7X_VMEM_LIMIT),
        name="inproj",
    )(x2d, mod3, mod3, lw["norm1_g"], lw["w_in"], *rope_tabs,
      lw["gqa"], lw["gka"], lw["gqd"], lw["gkd"], lw["gmat"], lw["gv"], lw["ws_cat"], lw["bs_tab"])


def _na_kernel(*refs, has_window, bounded=False):
    if bounded:
        q_ref, k_ref, v_ref, kc_ref, vc_ref, tab_ref, bound_ref, o_ref = refs
        refs = refs[:6] + refs[7:]
    if has_window:
        q_ref, k_ref, v_ref, kc_ref, vc_ref, tab_ref, o_ref = refs
        i = pl.program_id(1)
        u0 = jnp.clip(NA_Q_ROWS * i - NA_WIN_H // 2, 0, k_ref.shape[0] // GRID_W - NA_K_ROWS)
        off = pl.multiple_of(u0 * GRID_W, GRID_W)
        kw = k_ref[pl.ds(off, NA_K_ROWS * GRID_W), :]
        vw = v_ref[pl.ds(off, NA_K_ROWS * GRID_W), :]
    else:
        q_ref, kc_ref, vc_ref, o_ref = refs
    lane = lax.broadcasted_iota(jnp.int32, (1, 128), 1)
    lane_head = lane // HEAD_DIM
    for p in range(NA_HEADS // 2):
        cols = slice(128 * p, 128 * p + 128)
        qp = q_ref[:, cols]
        kcp = kc_ref[:, cols]
        vcp = vc_ref[:, cols]
        acc = jnp.zeros((qp.shape[0], 128), F32)
        for j in range(2):
            sel = (lane_head == j).astype(BF16)
            qm = qp * sel
            sc = _dot_nt(qm, kcp)
            if has_window:
                sw = _dot_nt(qm, kw[:, cols]) + tab_ref[2 * p + j]
            if bounded:
                ec = jnp.exp2(sc - bound_ref[...])
                ew = jnp.exp2(sw)
                l = jnp.sum(ec, axis=1, keepdims=True) + jnp.sum(ew, axis=1, keepdims=True)
                o = _dot(ec.astype(BF16), vcp * sel) + _dot(ew.astype(BF16), vw[:, cols] * sel)
                acc = acc + o / l
                continue
            m = jnp.max(sc, axis=1, keepdims=True)
            if has_window:
                m = jnp.maximum(m, jnp.max(sw, axis=1, keepdims=True))
            ec = jnp.exp2(sc - m)
            l = jnp.sum(ec, axis=1, keepdims=True)
            o = _dot(ec.astype(BF16), vcp * sel)
            if has_window:
                ew = jnp.exp2(sw - m)
                l = l + jnp.sum(ew, axis=1, keepdims=True)
                o = o + _dot(ew.astype(BF16), vw[:, cols] * sel)
            acc = acc + o / l
        o_ref[:, cols] = acc.astype(BF16)


def _na_window(qa, ka, va, kac, vac, tab, bound=None, *, batch, n):
    tq = NA_Q_ROWS * GRID_W
    nt = n // tq
    in_specs = [
        pl.BlockSpec((tq, 256), lambda b, i: (b * nt + i, 0)),
        pl.BlockSpec((n, 256), lambda b, i: (b, 0)),
        pl.BlockSpec((n, 256), lambda b, i: (b, 0)),
        pl.BlockSpec((CTX_LEN, 256), lambda b, i: (b, 0)),
        pl.BlockSpec((CTX_LEN, 256), lambda b, i: (b, 0)),
        pl.BlockSpec((None, NA_HEADS, tq, NA_K_ROWS * GRID_W),
                     lambda b, i: (jnp.where(i == 0, 0, jnp.where(i == nt - 1, 2, 1)), 0, 0, 0)),
    ]
    args = [qa, ka, va, kac, vac, tab]
    if bound is not None:
        in_specs.append(pl.BlockSpec((1, 1), lambda b, i: (0, 0)))
        args.append(bound)
    return pl.pallas_call(
        functools.partial(_na_kernel, has_window=True, bounded=bound is not None),
        grid=(batch, nt),
        in_specs=in_specs,
        out_specs=pl.BlockSpec((tq, 256), lambda b, i: (b * nt + i, 0)),
        out_shape=jax.ShapeDtypeStruct((batch * n, 256), BF16),
        compiler_params=pltpu.CompilerParams(
            dimension_semantics=("arbitrary", "arbitrary"), vmem_limit_bytes=V7X_VMEM_LIMIT),
        name="na_window_bounded" if bound is not None else "na_window",
    )(*args)


def _na_dense(qac, kac, vac, *, batch):
    spec = pl.BlockSpec((CTX_LEN, 256), lambda b: (b, 0))
    return pl.pallas_call(
        functools.partial(_na_kernel, has_window=False),
        grid=(batch,),
        in_specs=[spec, spec, spec],
        out_specs=spec,
        out_shape=jax.ShapeDtypeStruct((batch * CTX_LEN, 256), BF16),
        name="na_dense",
    )(qac, kac, vac)


def _da_kernel(*refs, nk, kt, lam_init):
    if nk:
        q_ref, k_ref, vt_ref, kc_ref, vtc_ref, lq1, lk1, lq2, lk2, subg_ref, o_ref = refs[:11]
    else:
        q_ref, kc_ref, vtc_ref, lq1, lk1, lq2, lk2, subg_ref, o_ref = refs[:9]
    kall, vtall, s_buf, bm_buf, acc_ref, m_ref = refs[-6:]
    tq = o_ref.shape[0]
    nt = q_ref.shape[0] // tq
    nblk = (nk + CTX_LEN) // kt
    i = pl.program_id(2)
    lane_map = lax.broadcasted_iota(jnp.int32, (1, 128), 1) // HEAD_DIM
    sel = [(lane_map == mi).astype(BF16) for mi in range(2)]

    def scores(tile, blk, slot):
        q = q_ref[pl.ds(pl.multiple_of(tile * tq, tq), tq), :]
        kblk = kall[pl.ds(pl.multiple_of(blk * kt, kt), kt), :]
        for mi in range(2):
            s = _dot_nt(kblk, q * sel[mi])
            s_buf[slot, mi] = s
            bm_buf[slot, mi] = jnp.max(s, axis=0, keepdims=True)

    @pl.when(i == 0)
    def _():
        if nk:
            kall[0:nk, :] = k_ref[...]
            vtall[0:DA_V_DIM, 0:nk] = vt_ref[...]
        kall[nk:nk + CTX_LEN, :] = kc_ref[...]
        vtall[0:DA_V_DIM, nk:nk + CTX_LEN] = vtc_ref[...]
        vtall[DA_V_DIM:, :] = jnp.ones((ONES_ROWS, nk + CTX_LEN), BF16)
        scores(0, 0, 0)

    acc_ref[...] = jnp.zeros_like(acc_ref)
    m_ref[...] = jnp.full_like(m_ref, NEG_INF)

    def softmax_pv(blk, slot):
        vte = vtall[:, pl.ds(pl.multiple_of(blk * kt, kt), kt)]
        for mi in range(2):
            for g in range(tq // 256):
                cols = slice(256 * g, 256 * g + 256)
                m_old = m_ref[mi, :, cols]
                m_new = jnp.maximum(m_old, bm_buf[slot, mi, :, cols])
                alpha = jnp.exp2(m_old - m_new)
                e = jnp.exp2(s_buf[slot, mi, :, cols] - m_new).astype(BF16)
                acc_ref[mi, :, cols] = acc_ref[mi, :, cols] * alpha + _dot(vte, e)
                m_ref[mi, :, cols] = m_new

    if nblk == 1:
        softmax_pv(0, 0)
    else:
        for blk in range(nblk):
            if blk + 1 < nblk:
                scores(i, blk + 1, (blk + 1) % 2)
            else:
                scores(jnp.minimum(i + 1, nt - 1), 0, 0)
            softmax_pv(blk, blk % 2)

    _da_finalize(acc_ref[0], acc_ref[1], lq1, lk1, lq2, lk2, subg_ref, o_ref, lam_init)


def _da_finalize(a0, a1, lq1, lk1, lq2, lk2, subg_ref, o_ref, lam_init):
    lam = (jnp.exp(jnp.sum(lq1[...] * lk1[...], keepdims=True))
           - jnp.exp(jnp.sum(lq2[...] * lk2[...], keepdims=True)) + lam_init)
    o = a0[:DA_V_DIM] / a0[DA_V_DIM:DA_V_DIM + 1] - lam * (a1[:DA_V_DIM] / a1[DA_V_DIM:DA_V_DIM + 1])
    y = o * lax.rsqrt(jnp.mean(o * o, axis=0, keepdims=True) + EPS) * subg_ref[...] * (1.0 - lam_init)
    o_ref[...] = y.T.astype(BF16)


def _da_bounded_kernel(q_ref, k_ref, vt_ref, kc_ref, vtc_ref, bound_ref, lq1, lk1, lq2, lk2, subg_ref, o_ref,
                       *, kt, lam_init):
    q = q_ref[...]
    lane_map = lax.broadcasted_iota(jnp.int32, (1, 128), 1) // HEAD_DIM
    qms = [q * (lane_map == mi).astype(BF16) for mi in range(2)]
    shift = bound_ref[...]
    acc = [None, None]

    def block(kblk, vtblk):
        vte = jnp.concatenate([vtblk, jnp.ones((ONES_ROWS, vtblk.shape[1]), BF16)], axis=0)
        for mi in range(2):
            e = jnp.exp2(_dot_nt(kblk, qms[mi]) - shift).astype(BF16)
            d = _dot(vte, e)
            acc[mi] = d if acc[mi] is None else acc[mi] + d

    for j in range(k_ref.shape[0] // kt):
        block(k_ref[j * kt:(j + 1) * kt, :], vt_ref[:, j * kt:(j + 1) * kt])
    block(kc_ref[...], vtc_ref[...])
    _da_finalize(acc[0], acc[1], lq1, lk1, lq2, lk2, subg_ref, o_ref, lam_init)


def _diff_attention_bounded(qd, kd, vdt, kdc, vdtc, bound, lw, *, batch, nq, nk, tq, kt, lam_init):
    nt = nq // tq
    assert nk % kt == 0 and nq % tq == 0
    small = lambda shape: pl.BlockSpec(shape, lambda b, h, i: (0, 0))
    return pl.pallas_call(
        functools.partial(_da_bounded_kernel, kt=kt, lam_init=lam_init),
        grid=(batch, DA_HEADS, nt),
        in_specs=[
            pl.BlockSpec((tq, 128), lambda b, h, i: (b * nt + i, h)),
            pl.BlockSpec((nk, 128), lambda b, h, i: (b, h)),
            pl.BlockSpec((None, 128, nk), lambda b, h, i: (b, h, 0)),
            pl.BlockSpec((CTX_LEN, 128), lambda b, h, i: (b, h)),
            pl.BlockSpec((None, 128, CTX_LEN), lambda b, h, i: (b, h, 0)),
            small((1, 1)), small((1, HEAD_DIM)), small((1, HEAD_DIM)), small((1, HEAD_DIM)), small((1, HEAD_DIM)),
            small((DA_V_DIM, 1)),
        ],
        out_specs=pl.BlockSpec((tq, 128), lambda b, h, i: (b * nt + i, h)),
        out_shape=jax.ShapeDtypeStruct((batch * nq, DA_WIDTH), BF16),
        compiler_params=pltpu.CompilerParams(
            dimension_semantics=("arbitrary", "arbitrary", "arbitrary"), vmem_limit_bytes=V7X_VMEM_LIMIT),
        name="diff_attn_bounded",
    )(qd, kd, vdt, kdc, vdtc, bound, lw["lq1"], lw["lk1"], lw["lq2"], lw["lk2"], lw["sub_g"])


def _diff_attention(qd, kd, vdt, kdc, vdtc, lw, *, batch, nq, nk, tq, kt, lam_init):
    nt = nq // tq
    nkeys = nk + CTX_LEN
    nblk = nkeys // kt
    assert nkeys % kt == 0 and tq % 256 == 0 and nq % tq == 0 and (nblk == 1 and nt == 1 or nblk % 2 == 0)
    in_specs = [pl.BlockSpec((nq, 128), lambda b, h, i: (b, h))]
    args = [qd]
    if nk:
        in_specs += [pl.BlockSpec((nk, 128), lambda b, h, i: (b, h)),
                     pl.BlockSpec((None, 128, nk), lambda b, h, i: (b, h, 0))]
        args += [kd, vdt]
    in_specs += [pl.BlockSpec((CTX_LEN, 128), lambda b, h, i: (b, h)),
                 pl.BlockSpec((None, 128, CTX_LEN), lambda b, h, i: (b, h, 0))]
    args += [kdc, vdtc]
    in_specs += [pl.BlockSpec((1, HEAD_DIM), lambda b, h, i: (0, 0))] * 4
    args += [lw["lq1"], lw["lk1"], lw["lq2"], lw["lk2"]]
    in_specs += [pl.BlockSpec((DA_V_DIM, 1), lambda b, h, i: (0, 0))]
    args += [lw["sub_g"]]
    return pl.pallas_call(
        functools.partial(_da_kernel, nk=nk, kt=kt, lam_init=lam_init),
        grid=(batch, DA_HEADS, nt),
        in_specs=in_specs,
        out_specs=pl.BlockSpec((tq, 128), lambda b, h, i: (b * nt + i, h)),
        out_shape=jax.ShapeDtypeStruct((batch * nq, DA_WIDTH), BF16),
        scratch_shapes=[
            pltpu.VMEM((nkeys, 128), BF16),
            pltpu.VMEM((DA_V_DIM + ONES_ROWS, nkeys), BF16),
            pltpu.VMEM((2, 2, kt, tq), F32),
            pltpu.VMEM((2, 2, 1, tq), F32),
            pltpu.VMEM((2, DA_V_DIM + ONES_ROWS, tq), F32),
            pltpu.VMEM((2, 1, tq), F32),
        ],
        compiler_params=pltpu.CompilerParams(
            dimension_semantics=("arbitrary", "arbitrary", "arbitrary"), vmem_limit_bytes=V7X_VMEM_LIMIT),
        name="diff_attn" if nk else "diff_attn_ctx",
    )(*args)


def _ffn_kernel(x_ref, oa_ref, ob_ref, oc_ref, gate1_ref, shift2_ref, scale2_ref, gate2_ref, g2_ref,
                wout_ref, w1_ref, w3_ref, w2_ref, o_ref):
    mixed = (_dot(oa_ref[...], wout_ref[0:NA_WIDTH, :])
             + _dot(ob_ref[...], wout_ref[NA_WIDTH:NA_WIDTH + GM_WIDTH, :])
             + _dot(oc_ref[...], wout_ref[NA_WIDTH + GM_WIDTH:, :]))
    x1 = x_ref[...] + gate1_ref[0] * mixed
    h = x1 * lax.rsqrt(jnp.mean(x1 * x1, axis=-1, keepdims=True) + EPS) * g2_ref[...]
    hb = (h * (1.0 + scale2_ref[0]) + shift2_ref[0]).astype(BF16)
    a = _dot(hb, w1_ref[...])
    b = _dot(hb, w3_ref[...])
    g = (a * jax.nn.sigmoid(a) * b).astype(BF16)
    o_ref[...] = x1 + gate2_ref[0] * _dot(g, w2_ref[...])


def _out_ffn(x2d, oa, ob, oc, mod3, lw, *, batch, n, tm, per_batch_mod):
    nt = n // tm
    row = (lambda b: b) if per_batch_mod else (lambda b: CTX_MOD_ROW)
    const = lambda b, i: (0, 0)
    tok = lambda b, i: (b * nt + i, 0)
    mod_spec = lambda k: pl.BlockSpec((1, 1, D_MODEL), lambda b, i: (row(b), 0, k))
    return pl.pallas_call(
        _ffn_kernel,
        grid=(batch, nt),
        in_specs=[
            pl.BlockSpec((tm, D_MODEL), tok),
            pl.BlockSpec((tm, NA_WIDTH), tok),
            pl.BlockSpec((tm, GM_WIDTH), tok),
            pl.BlockSpec((tm, DA_WIDTH), tok),
            mod_spec(2), mod_spec(3), mod_spec(4), mod_spec(5),
            pl.BlockSpec((1, D_MODEL), const),
            _resident((D_MODEL, D_MODEL), const),
            _resident((D_MODEL, D_FF), const),
            _resident((D_MODEL, D_FF), const),
            _resident((D_FF, D_MODEL), const),
        ],
        out_specs=pl.BlockSpec((tm, D_MODEL), tok),
        out_shape=jax.ShapeDtypeStruct((batch * n, D_MODEL), F32),
        compiler_params=pltpu.CompilerParams(
            dimension_semantics=("arbitrary", "arbitrary"), vmem_limit_bytes=V7X_VMEM_LIMIT),
        name="out_ffn",
    )(x2d, oa, ob, oc, mod3, mod3, mod3, mod3, lw["norm2_g"], lw["w_out"], lw["w1"], lw["w3"], lw["w2"])


def _rope_tables(n):
    t = jnp.arange(n, dtype=jnp.int32)
    row = (t // GRID_W).astype(F32)
    col = (t % GRID_W).astype(F32)
    half = HEAD_DIM // 2
    inv = ROPE_BASE ** (-jnp.arange(0, half, 2, dtype=F32) / half)
    ar = row[:, None] * inv[None, :]
    ac = col[:, None] * inv[None, :]
    ang = jnp.concatenate([ar, ar, ac, ac], axis=-1)
    cos = jnp.cos(ang)
    sin = jnp.sin(ang)
    first = (np.arange(HEAD_DIM) % 32) < 16
    sa = jnp.where(first, -sin, 0.0)
    sb = jnp.where(first, 0.0, sin)
    return tuple(jnp.tile(a, (1, 2)) for a in (cos, sa, sb))


def _identity_rope_tables(n):
    return (jnp.ones((n, 128), F32), jnp.zeros((n, 128), F32), jnp.zeros((n, 128), F32))


def _na_bias_tables(rpb, rows, shift):
    cq = np.arange(GRID_W)
    c0 = np.clip(cq - NA_WIN_W // 2, 0, GRID_W - NA_WIN_W)
    col_ok = (cq[None, :] >= c0[:, None]) & (cq[None, :] < c0[:, None] + NA_WIN_W)
    coff = np.clip(cq[None, :] - cq[:, None], -(NA_WIN_W - 1), NA_WIN_W - 1) + (NA_WIN_W - 1)
    col_sel = (coff[None] == np.arange(2 * NA_WIN_W - 1)[:, None, None]).astype(np.float32)
    row_sel, ok = [], []
    for r0 in (0, 2 * NA_Q_ROWS, rows - NA_Q_ROWS):
        u0 = min(max(r0 - NA_WIN_H // 2, 0), rows - NA_K_ROWS)
        r = r0 + np.arange(NA_Q_ROWS)
        key_row = u0 + np.arange(NA_K_ROWS)
        start = np.clip(r - NA_WIN_H // 2, 0, rows - NA_WIN_H)
        row_ok = (key_row[None, :] >= start[:, None]) & (key_row[None, :] < start[:, None] + NA_WIN_H)
        roff = key_row[None, :] - r[:, None] + (NA_WIN_H - 1)
        row_sel.append((roff[None] == np.arange(2 * NA_WIN_H - 1)[:, None, None]).astype(np.float32))
        ok.append(row_ok[:, None, :, None] & col_ok[None, :, None, :])
    toeplitz = jnp.einsum("hrc,cqk->hrqk", rpb.astype(F32), jnp.asarray(col_sel),
                          precision=lax.Precision.HIGHEST)
    bias = jnp.einsum("traj,hrqk->thaqjk", jnp.asarray(np.stack(row_sel)), toeplitz,
                      precision=lax.Precision.HIGHEST)
    tab = jnp.where(jnp.asarray(np.stack(ok))[:, None], bias * LOG2E - shift, NEG_INF)
    return tab.reshape(3, NA_HEADS, NA_Q_ROWS * GRID_W, NA_K_ROWS * GRID_W)


def _layer_weights(i, p):
    tile4 = lambda g: jnp.tile(g.astype(F32), 256 // HEAD_DIM).reshape(1, 256)
    blk = np.arange(256) // HEAD_DIM
    return {
        "norm1_g": p["norm1_g"][i].reshape(1, D_MODEL),
        "w_in": p["w_in"][i].astype(BF16),
        "gqa": tile4(p["na_q_g"][i]) * (HEAD_DIM ** -0.5 * LOG2E),
        "gka": tile4(p["na_k_g"][i]),
        "gqd": tile4(p["da_q_g"][i]) * (HEAD_DIM ** -0.5 * LOG2E),
        "gkd": tile4(p["da_k_g"][i]),
        "gmat": jnp.asarray(blk[:, None] == blk[None, :], BF16),
        "gv": p["gm_v_g"][i].reshape(1, GM_WIDTH),
        "ws_cat": p["gm_ws"][i].transpose(1, 0, 2).reshape(GM_CHUNK, GM_GROUPS * GM_CHUNK).astype(BF16),
        "bs_tab": jnp.repeat(p["gm_bs"][i].T, GM_WIDTH // GM_GROUPS, axis=1),
        "lq1": p["da_lq1"][i].reshape(1, HEAD_DIM),
        "lk1": p["da_lk1"][i].reshape(1, HEAD_DIM),
        "lq2": p["da_lq2"][i].reshape(1, HEAD_DIM),
        "lk2": p["da_lk2"][i].reshape(1, HEAD_DIM),
        "sub_g": p["da_sub_g"][i].reshape(DA_V_DIM, 1),
        "norm2_g": p["norm2_g"][i].reshape(1, D_MODEL),
        "w_out": p["w_out"][i].astype(BF16),
        "w1": p["ffn_w1"][i].astype(BF16),
        "w3": p["ffn_w3"][i].astype(BF16),
        "w2": p["ffn_w2"][i].astype(BF16),
    }


def kernel(x, c, ctx, c_ctx, w_mod, b_mod, norm1_g, w_in, na_q_g, na_k_g, na_rpb, gm_v_g, gm_ws, gm_bs,
           da_q_g, da_k_g, da_lq1, da_lk1, da_lq2, da_lk2, da_sub_g, w_out, norm2_g, ffn_w1, ffn_w3, ffn_w2):
    batch, n, _ = x.shape
    depth = w_mod.shape[0]
    assert n % (NA_Q_ROWS * GRID_W) == 0 and ctx.shape[1] == CTX_LEN and batch < CTX_MOD_ROW + 1
    params = dict(norm1_g=norm1_g, w_in=w_in, na_q_g=na_q_g, na_k_g=na_k_g, gm_v_g=gm_v_g, gm_ws=gm_ws,
                  gm_bs=gm_bs, da_q_g=da_q_g, da_k_g=da_k_g, da_lq1=da_lq1, da_lk1=da_lk1, da_lq2=da_lq2,
                  da_lk2=da_lk2, da_sub_g=da_sub_g, w_out=w_out, norm2_g=norm2_g, ffn_w1=ffn_w1,
                  ffn_w3=ffn_w3, ffn_w2=ffn_w2)

    c8 = jnp.zeros((MOD_ROWS, D_MODEL), F32).at[:batch].set(c).at[CTX_MOD_ROW].set(c_ctx)
    mod_all = _modulation(c8, w_mod, b_mod)

    rope_lat = _rope_tables(n)
    rope_ctx = _identity_rope_tables(CTX_LEN)
    x2d = x.reshape(batch * n, D_MODEL)
    xc2d = ctx.reshape(batch * CTX_LEN, D_MODEL)

    for i in range(depth):
        lam_init = 0.8 - 0.6 * math.exp(-0.3 * i)
        lw = _layer_weights(i, params)
        mod3 = mod_all[i].reshape(MOD_ROWS, 1, 6 * D_MODEL)

        qa, qd, ka, va, kd, vdt, ob = _inproj(x2d, mod3, lw, rope_lat, batch=batch, n=n, tm=INPROJ_TM,
                                              per_batch_mod=True)
        qac, qdc, kac, vac, kdc, vdtc, obc = _inproj(xc2d, mod3, lw, rope_ctx, batch=batch, n=CTX_LEN,
                                                     tm=CTX_LEN, per_batch_mod=False)
        na_bound = (_score_bound(na_q_g[i], na_k_g[i]) + LOG2E * jnp.max(jnp.abs(na_rpb[i]))).astype(F32)
        na_bounded = na_bound <= MAX_CONSTANT_SHIFT
        tab = _na_bias_tables(na_rpb[i], n // GRID_W, jnp.where(na_bounded, na_bound, 0.0))
        oa = lax.cond(
            na_bounded,
            lambda *a: _na_window(*a[:6], a[6].reshape(1, 1), batch=batch, n=n),
            lambda *a: _na_window(*a[:6], batch=batch, n=n),
            qa, ka, va, kac, vac, tab, na_bound)
        da_bound = _score_bound(da_q_g[i], da_k_g[i]).astype(F32)
        oc = lax.cond(
            da_bound <= MAX_CONSTANT_SHIFT,
            lambda *a: _diff_attention_bounded(*a, lw, batch=batch, nq=n, nk=n, tq=DA_BOUNDED_TQ,
                                               kt=DA_BOUNDED_KT, lam_init=lam_init),
            lambda *a: _diff_attention(*a[:5], lw, batch=batch, nq=n, nk=n, tq=DA_ONLINE_TQ, kt=DA_ONLINE_KT,
                                       lam_init=lam_init),
            qd, kd, vdt, kdc, vdtc, da_bound.reshape(1, 1))
        x2d = _out_ffn(x2d, oa, ob, oc, mod3, lw, batch=batch, n=n, tm=FFN_TM, per_batch_mod=True)
        if i < depth - 1:
            oac = _na_dense(qac, kac, vac, batch=batch)
            occ = _diff_attention(qdc, None, None, kdc, vdtc, lw, batch=batch, nq=CTX_LEN, nk=0,
                                  tq=CTX_LEN, kt=CTX_LEN, lam_init=lam_init)
            xc2d = _out_ffn(xc2d, oac, obc, occ, mod3, lw, batch=batch, n=CTX_LEN, tm=CTX_LEN,
                            per_batch_mod=False)
    return x2d.reshape(batch, n, D_MODEL)
```

```python
import functools
import math

import numpy as np
import jax
import jax.numpy as jnp
from jax import lax
from jax.experimental import pallas as pl
from jax.experimental.pallas import tpu as pltpu

D_MODEL = 1024
GRID_W = 64
CTX_LEN = 256
HEAD_DIM = 64
NA_HEADS = 4
NA_WIN_H = 8
NA_WIN_W = 16
NA_WIDTH = NA_HEADS * HEAD_DIM
GM_GROUPS = 4
GM_CHUNK = 128
GM_WIDTH = 256
DA_HEADS = 4
DA_QK_DIM = 2 * HEAD_DIM
DA_V_DIM = 2 * HEAD_DIM
DA_WIDTH = DA_HEADS * DA_V_DIM
QU_WIDTH = NA_WIDTH + DA_HEADS * DA_QK_DIM + 2 * GM_WIDTH
IN_WIDTH = QU_WIDTH + 2 * NA_WIDTH + DA_HEADS * DA_QK_DIM + DA_HEADS * DA_V_DIM
D_FF = -(-8 * D_MODEL // (3 * 256)) * 256
ROPE_BASE = 10000.0
EPS = 1e-6
NEG_INF = -1e30
LOG2E = math.log2(math.e)

COL_QA = 0
COL_QD = NA_WIDTH
COL_UV = COL_QD + DA_HEADS * DA_QK_DIM
COL_KA = QU_WIDTH
COL_VA = COL_KA + NA_WIDTH
COL_KD = COL_VA + NA_WIDTH
COL_VD = COL_KD + DA_HEADS * DA_QK_DIM

MOD_ROWS = 8
CTX_MOD_ROW = 4
NA_Q_ROWS = 4
NA_K_ROWS = 12
ONES_ROWS = 16
MAX_CONSTANT_SHIFT = 48.0

V7X_VMEM_LIMIT = 56 * 2 ** 20
INPROJ_TM = 1024
INPROJ_SUB = 256
FFN_TM = 512
DA_BOUNDED_TQ = 1024
DA_BOUNDED_KT = 1024
DA_ONLINE_TQ = 512
DA_ONLINE_KT = 1408


def _score_bound(gq, gk):
    return 1.02 * HEAD_DIM * (HEAD_DIM ** -0.5 * LOG2E) * jnp.max(jnp.abs(gq)) * jnp.max(jnp.abs(gk))

F32 = jnp.float32
BF16 = jnp.bfloat16


def _dot(a, b):
    return jnp.dot(a, b, preferred_element_type=F32)


def _dot_nt(a, b):
    return lax.dot_general(a, b, (((1,), (1,)), ((), ())), preferred_element_type=F32)


def _resident(shape, index_map):
    return pl.BlockSpec(shape, index_map, pipeline_mode=pl.Buffered(1))


def _mod_kernel(c_ref, w_ref, b_ref, o_ref):
    c = c_ref[...]
    a = c * jax.nn.sigmoid(c)
    a_hi = a.astype(BF16)
    a_lo = (a - a_hi.astype(F32)).astype(BF16)
    w = w_ref[...]
    w_hi = w.astype(BF16)
    w_lo = (w - w_hi.astype(F32)).astype(BF16)
    o_ref[...] = _dot(a_hi, w_hi) + _dot(a_lo, w_hi) + _dot(a_hi, w_lo) + b_ref[...]


def _modulation(c8, w_mod, b_mod):
    depth = w_mod.shape[0]
    tn = 1024
    return pl.pallas_call(
        _mod_kernel,
        grid=(depth, 6 * D_MODEL // tn),
        in_specs=[
            pl.BlockSpec((MOD_ROWS, D_MODEL), lambda l, j: (0, 0)),
            pl.BlockSpec((None, D_MODEL, tn), lambda l, j: (l, 0, j)),
            pl.BlockSpec((None, 1, tn), lambda l, j: (l, 0, j)),
        ],
        out_specs=pl.BlockSpec((None, MOD_ROWS, tn), lambda l, j: (l, 0, j)),
        out_shape=jax.ShapeDtypeStruct((depth, MOD_ROWS, 6 * D_MODEL), F32),
        name="adaln_mod",
    )(c8, w_mod, b_mod.reshape(depth, 1, 6 * D_MODEL))


def _inproj_kernel(x_ref, shift_ref, scale_ref, g1_ref, w_ref, cos_ref, sa_ref, sb_ref,
                   gqa_ref, gka_ref, gqd_ref, gkd_ref, gmat_ref, gv_ref, ws_ref, bs_ref,
                   qa_ref, qd_ref, ka_ref, va_ref, kd_ref, vdt_ref, ob_ref, *, tm, sub):
    gmat = gmat_ref[...]
    group = lax.broadcasted_iota(jnp.int32, (GM_CHUNK, GM_WIDTH), 1) // (GM_WIDTH // GM_GROUPS)
    ws = ws_ref[...]
    bs = bs_ref[...]

    def head_norm(y, g):
        ss = _dot((y * y).astype(BF16), gmat)
        return y * lax.rsqrt(ss * (1.0 / HEAD_DIM) + EPS) * g

    for r0 in range(0, tm, sub):
        rows = slice(r0, r0 + sub)
        x = x_ref[rows, :]
        h = x * lax.rsqrt(jnp.mean(x * x, axis=-1, keepdims=True) + EPS) * g1_ref[...]
        hb = (h * (1.0 + scale_ref[0]) + shift_ref[0]).astype(BF16)
        p = _dot(hb, w_ref[...])
        cos = cos_ref[rows, :]
        sa = sa_ref[rows, :]
        sb = sb_ref[rows, :]

        def rope(z):
            return z * cos + pltpu.roll(z, 128 - 16, 1) * sa + pltpu.roll(z, 16, 1) * sb

        qa_ref[rows, :] = head_norm(p[:, COL_QA:COL_QA + 256], gqa_ref[...]).astype(BF16)
        ka_ref[rows, :] = head_norm(p[:, COL_KA:COL_KA + 256], gka_ref[...]).astype(BF16)
        va_ref[rows, :] = p[:, COL_VA:COL_VA + 256].astype(BF16)
        for c in range(2):
            yq = head_norm(p[:, COL_QD + 256 * c:COL_QD + 256 * c + 256], gqd_ref[...])
            yk = head_norm(p[:, COL_KD + 256 * c:COL_KD + 256 * c + 256], gkd_ref[...])
            for t in range(2):
                lo = 256 * c + 128 * t
                qd_ref[rows, lo:lo + 128] = rope(yq[:, 128 * t:128 * t + 128]).astype(BF16)
                kd_ref[rows, lo:lo + 128] = rope(yk[:, 128 * t:128 * t + 128]).astype(BF16)
        vdt_ref[:, rows] = p[:, COL_VD:COL_VD + 512].T.astype(BF16)

        z = jax.nn.gelu(p[:, COL_UV:COL_UV + 2 * GM_WIDTH])
        u = z[:, :GM_WIDTH]
        v = z[:, GM_WIDTH:]
        v = v * lax.rsqrt(jnp.mean(v * v, axis=-1, keepdims=True) + EPS) * gv_ref[...]
        vb = v.astype(BF16)
        for c in range(sub // GM_CHUNK):
            vc = vb[c * GM_CHUNK:(c + 1) * GM_CHUNK, :]
            vbd = jnp.concatenate([jnp.where(group == g, vc, jnp.zeros_like(vc)) for g in range(GM_GROUPS)],
                                  axis=0)
            s = _dot(ws, vbd) + bs
            lo = r0 + c * GM_CHUNK
            ob_ref[lo:lo + GM_CHUNK, :] = (u[c * GM_CHUNK:(c + 1) * GM_CHUNK, :] * s).astype(BF16)


def _inproj(x2d, mod3, lw, rope_tabs, *, batch, n, tm, per_batch_mod):
    nt = n // tm
    t_tot = batch * n
    row = (lambda b: b) if per_batch_mod else (lambda b: CTX_MOD_ROW)
    const = lambda b, i: (0, 0)
    tok = lambda b, i: (b * nt + i, 0)
    in_specs = [
        pl.BlockSpec((tm, D_MODEL), tok),
        pl.BlockSpec((1, 1, D_MODEL), lambda b, i: (row(b), 0, 0)),
        pl.BlockSpec((1, 1, D_MODEL), lambda b, i: (row(b), 0, 1)),
        pl.BlockSpec((1, D_MODEL), const),
        _resident((D_MODEL, IN_WIDTH), const),
        pl.BlockSpec((tm, 128), lambda b, i: (i, 0)),
        pl.BlockSpec((tm, 128), lambda b, i: (i, 0)),
        pl.BlockSpec((tm, 128), lambda b, i: (i, 0)),
        pl.BlockSpec((1, 256), const),
        pl.BlockSpec((1, 256), const),
        pl.BlockSpec((1, 256), const),
        pl.BlockSpec((1, 256), const),
        pl.BlockSpec((256, 256), const),
        pl.BlockSpec((1, GM_WIDTH), const),
        pl.BlockSpec((GM_CHUNK, GM_GROUPS * GM_CHUNK), const),
        pl.BlockSpec((GM_CHUNK, GM_WIDTH), const),
    ]
    out_specs = [
        pl.BlockSpec((tm, 256), tok),
        pl.BlockSpec((tm, 512), tok),
        pl.BlockSpec((tm, 256), tok),
        pl.BlockSpec((tm, 256), tok),
        pl.BlockSpec((tm, 512), tok),
        pl.BlockSpec((None, 512, tm), lambda b, i: (b, 0, i)),
        pl.BlockSpec((tm, 256), tok),
    ]
    out_shape = [
        jax.ShapeDtypeStruct((t_tot, 256), BF16),
        jax.ShapeDtypeStruct((t_tot, 512), BF16),
        jax.ShapeDtypeStruct((t_tot, 256), BF16),
        jax.ShapeDtypeStruct((t_tot, 256), BF16),
        jax.ShapeDtypeStruct((t_tot, 512), BF16),
        jax.ShapeDtypeStruct((batch, 512, n), BF16),
        jax.ShapeDtypeStruct((t_tot, 256), BF16),
    ]
    return pl.pallas_call(
        functools.partial(_inproj_kernel, tm=tm, sub=min(tm, INPROJ_SUB)),
        grid=(batch, nt),
        in_specs=in_specs,
        out_specs=out_specs,
        out_shape=out_shape,
        compiler_params=pltpu.CompilerParams(
            dimension_semantics=("arbitrary", "arbitrary"), vmem_limit_bytes=V7X_VMEM_LIMIT),
        name="inproj",
    )(x2d, mod3, mod3, lw["norm1_g"], lw["w_in"], *rope_tabs,
      lw["gqa"], lw["gka"], lw["gqd"], lw["gkd"], lw["gmat"], lw["gv"], lw["ws_cat"], lw["bs_tab"])


def _na_kernel(*refs, has_window, bounded=False):
    if bounded:
        q_ref, k_ref, v_ref, kc_ref, vc_ref, tab_ref, bound_ref, o_ref = refs
        refs = refs[:6] + refs[7:]
    if has_window:
        q_ref, k_ref, v_ref, kc_ref, vc_ref, tab_ref, o_ref = refs
        i = pl.program_id(1)
        u0 = jnp.clip(NA_Q_ROWS * i - NA_WIN_H // 2, 0, k_ref.shape[0] // GRID_W - NA_K_ROWS)
        off = pl.multiple_of(u0 * GRID_W, GRID_W)
        kw = k_ref[pl.ds(off, NA_K_ROWS * GRID_W), :]
        vw = v_ref[pl.ds(off, NA_K_ROWS * GRID_W), :]
    else:
        q_ref, kc_ref, vc_ref, o_ref = refs
    lane = lax.broadcasted_iota(jnp.int32, (1, 128), 1)
    lane_head = lane // HEAD_DIM
    for p in range(NA_HEADS // 2):
        cols = slice(128 * p, 128 * p + 128)
        qp = q_ref[:, cols]
        kcp = kc_ref[:, cols]
        vcp = vc_ref[:, cols]
        acc = jnp.zeros((qp.shape[0], 128), F32)
        for j in range(2):
            sel = (lane_head == j).astype(BF16)
            qm = qp * sel
            sc = _dot_nt(qm, kcp)
            if has_window:
                sw = _dot_nt(qm, kw[:, cols]) + tab_ref[2 * p + j]
            if bounded:
                ec = jnp.exp2(sc - bound_ref[...])
                ew = jnp.exp2(sw)
                l = jnp.sum(ec, axis=1, keepdims=True) + jnp.sum(ew, axis=1, keepdims=True)
                o = _dot(ec.astype(BF16), vcp * sel) + _dot(ew.astype(BF16), vw[:, cols] * sel)
                acc = acc + o / l
                continue
            m = jnp.max(sc, axis=1, keepdims=True)
            if has_window:
                m = jnp.maximum(m, jnp.max(sw, axis=1, keepdims=True))
            ec = jnp.exp2(sc - m)
            l = jnp.sum(ec, axis=1, keepdims=True)
            o = _dot(ec.astype(BF16), vcp * sel)
            if has_window:
                ew = jnp.exp2(sw - m)
                l = l + jnp.sum(ew, axis=1, keepdims=True)
                o = o + _dot(ew.astype(BF16), vw[:, cols] * sel)
            acc = acc + o / l
        o_ref[:, cols] = acc.astype(BF16)


def _na_window(qa, ka, va, kac, vac, tab, bound=None, *, batch, n):
    tq = NA_Q_ROWS * GRID_W
    nt = n // tq
    in_specs = [
        pl.BlockSpec((tq, 256), lambda b, i: (b * nt + i, 0)),
        pl.BlockSpec((n, 256), lambda b, i: (b, 0)),
        pl.BlockSpec((n, 256), lambda b, i: (b, 0)),
        pl.BlockSpec((CTX_LEN, 256), lambda b, i: (b, 0)),
        pl.BlockSpec((CTX_LEN, 256), lambda b, i: (b, 0)),
        pl.BlockSpec((None, NA_HEADS, tq, NA_K_ROWS * GRID_W),
                     lambda b, i: (jnp.where(i == 0, 0, jnp.where(i == nt - 1, 2, 1)), 0, 0, 0)),
    ]
    args = [qa, ka, va, kac, vac, tab]
    if bound is not None:
        in_specs.append(pl.BlockSpec((1, 1), lambda b, i: (0, 0)))
        args.append(bound)
    return pl.pallas_call(
        functools.partial(_na_kernel, has_window=True, bounded=bound is not None),
        grid=(batch, nt),
        in_specs=in_specs,
        out_specs=pl.BlockSpec((tq, 256), lambda b, i: (b * nt + i, 0)),
        out_shape=jax.ShapeDtypeStruct((batch * n, 256), BF16),
        compiler_params=pltpu.CompilerParams(
            dimension_semantics=("arbitrary", "arbitrary"), vmem_limit_bytes=V7X_VMEM_LIMIT),
        name="na_window_bounded" if bound is not None else "na_window",
    )(*args)


def _na_dense(qac, kac, vac, *, batch):
    spec = pl.BlockSpec((CTX_LEN, 256), lambda b: (b, 0))
    return pl.pallas_call(
        functools.partial(_na_kernel, has_window=False),
        grid=(batch,),
        in_specs=[spec, spec, spec],
        out_specs=spec,
        out_shape=jax.ShapeDtypeStruct((batch * CTX_LEN, 256), BF16),
        name="na_dense",
    )(qac, kac, vac)


def _da_kernel(*refs, nk, kt, lam_init):
    if nk:
        q_ref, k_ref, vt_ref, kc_ref, vtc_ref, lq1, lk1, lq2, lk2, subg_ref, o_ref = refs[:11]
    else:
        q_ref, kc_ref, vtc_ref, lq1, lk1, lq2, lk2, subg_ref, o_ref = refs[:9]
    kall, vtall, s_buf, bm_buf, acc_ref, m_ref = refs[-6:]
    tq = o_ref.shape[0]
    nt = q_ref.shape[0] // tq
    nblk = (nk + CTX_LEN) // kt
    i = pl.program_id(2)
    lane_map = lax.broadcasted_iota(jnp.int32, (1, 128), 1) // HEAD_DIM
    sel = [(lane_map == mi).astype(BF16) for mi in range(2)]

    def scores(tile, blk, slot):
        q = q_ref[pl.ds(pl.multiple_of(tile * tq, tq), tq), :]
        kblk = kall[pl.ds(pl.multiple_of(blk * kt, kt), kt), :]
        for mi in range(2):
            s = _dot_nt(kblk, q * sel[mi])
            s_buf[slot, mi] = s
            bm_buf[slot, mi] = jnp.max(s, axis=0, keepdims=True)

    @pl.when(i == 0)
    def _():
        if nk:
            kall[0:nk, :] = k_ref[...]
            vtall[0:DA_V_DIM, 0:nk] = vt_ref[...]
        kall[nk:nk + CTX_LEN, :] = kc_ref[...]
        vtall[0:DA_V_DIM, nk:nk + CTX_LEN] = vtc_ref[...]
        vtall[DA_V_DIM:, :] = jnp.ones((ONES_ROWS, nk + CTX_LEN), BF16)
        scores(0, 0, 0)

    acc_ref[...] = jnp.zeros_like(acc_ref)
    m_ref[...] = jnp.full_like(m_ref, NEG_INF)

    def softmax_pv(blk, slot):
        vte = vtall[:, pl.ds(pl.multiple_of(blk * kt, kt), kt)]
        for mi in range(2):
            for g in range(tq // 256):
                cols = slice(256 * g, 256 * g + 256)
                m_old = m_ref[mi, :, cols]
                m_new = jnp.maximum(m_old, bm_buf[slot, mi, :, cols])
                alpha = jnp.exp2(m_old - m_new)
                e = jnp.exp2(s_buf[slot, mi, :, cols] - m_new).astype(BF16)
                acc_ref[mi, :, cols] = acc_ref[mi, :, cols] * alpha + _dot(vte, e)
                m_ref[mi, :, cols] = m_new

    if nblk == 1:
        softmax_pv(0, 0)
    else:
        for blk in range(nblk):
            if blk + 1 < nblk:
                scores(i, blk + 1, (blk + 1) % 2)
            else:
                scores(jnp.minimum(i + 1, nt - 1), 0, 0)
            softmax_pv(blk, blk % 2)

    _da_finalize(acc_ref[0], acc_ref[1], lq1, lk1, lq2, lk2, subg_ref, o_ref, lam_init)


def _da_finalize(a0, a1, lq1, lk1, lq2, lk2, subg_ref, o_ref, lam_init):
    lam = (jnp.exp(jnp.sum(lq1[...] * lk1[...], keepdims=True))
           - jnp.exp(jnp.sum(lq2[...] * lk2[...], keepdims=True)) + lam_init)
    o = a0[:DA_V_DIM] / a0[DA_V_DIM:DA_V_DIM + 1] - lam * (a1[:DA_V_DIM] / a1[DA_V_DIM:DA_V_DIM + 1])
    y = o * lax.rsqrt(jnp.mean(o * o, axis=0, keepdims=True) + EPS) * subg_ref[...] * (1.0 - lam_init)
    o_ref[...] = y.T.astype(BF16)


def _da_bounded_kernel(q_ref, k_ref, vt_ref, kc_ref, vtc_ref, bound_ref, lq1, lk1, lq2, lk2, subg_ref, o_ref,
                       *, kt, lam_init):
    q = q_ref[...]
    lane_map = lax.broadcasted_iota(jnp.int32, (1, 128), 1) // HEAD_DIM
    qms = [q * (lane_map == mi).astype(BF16) for mi in range(2)]
    shift = bound_ref[...]
    acc = [None, None]

    def block(kblk, vtblk):
        vte = jnp.concatenate([vtblk, jnp.ones((ONES_ROWS, vtblk.shape[1]), BF16)], axis=0)
        for mi in range(2):
            e = jnp.exp2(_dot_nt(kblk, qms[mi]) - shift).astype(BF16)
            d = _dot(vte, e)
            acc[mi] = d if acc[mi] is None else acc[mi] + d

    for j in range(k_ref.shape[0] // kt):
        block(k_ref[j * kt:(j + 1) * kt, :], vt_ref[:, j * kt:(j + 1) * kt])
    block(kc_ref[...], vtc_ref[...])
    _da_finalize(acc[0], acc[1], lq1, lk1, lq2, lk2, subg_ref, o_ref, lam_init)


def _diff_attention_bounded(qd, kd, vdt, kdc, vdtc, bound, lw, *, batch, nq, nk, tq, kt, lam_init):
    nt = nq // tq
    assert nk % kt == 0 and nq % tq == 0
    small = lambda shape: pl.BlockSpec(shape, lambda b, h, i: (0, 0))
    return pl.pallas_call(
        functools.partial(_da_bounded_kernel, kt=kt, lam_init=lam_init),
        grid=(batch, DA_HEADS, nt),
        in_specs=[
            pl.BlockSpec((tq, 128), lambda b, h, i: (b * nt + i, h)),
            pl.BlockSpec((nk, 128), lambda b, h, i: (b, h)),
            pl.BlockSpec((None, 128, nk), lambda b, h, i: (b, h, 0)),
            pl.BlockSpec((CTX_LEN, 128), lambda b, h, i: (b, h)),
            pl.BlockSpec((None, 128, CTX_LEN), lambda b, h, i: (b, h, 0)),
            small((1, 1)), small((1, HEAD_DIM)), small((1, HEAD_DIM)), small((1, HEAD_DIM)), small((1, HEAD_DIM)),
            small((DA_V_DIM, 1)),
        ],
        out_specs=pl.BlockSpec((tq, 128), lambda b, h, i: (b * nt + i, h)),
        out_shape=jax.ShapeDtypeStruct((batch * nq, DA_WIDTH), BF16),
        compiler_params=pltpu.CompilerParams(
            dimension_semantics=("arbitrary", "arbitrary", "arbitrary"), vmem_limit_bytes=V7X_VMEM_LIMIT),
        name="diff_attn_bounded",
    )(qd, kd, vdt, kdc, vdtc, bound, lw["lq1"], lw["lk1"], lw["lq2"], lw["lk2"], lw["sub_g"])


def _diff_attention(qd, kd, vdt, kdc, vdtc, lw, *, batch, nq, nk, tq, kt, lam_init):
    nt = nq // tq
    nkeys = nk + CTX_LEN
    nblk = nkeys // kt
    assert nkeys % kt == 0 and tq % 256 == 0 and nq % tq == 0 and (nblk == 1 and nt == 1 or nblk % 2 == 0)
    in_specs = [pl.BlockSpec((nq, 128), lambda b, h, i: (b, h))]
    args = [qd]
    if nk:
        in_specs += [pl.BlockSpec((nk, 128), lambda b, h, i: (b, h)),
                     pl.BlockSpec((None, 128, nk), lambda b, h, i: (b, h, 0))]
        args += [kd, vdt]
    in_specs += [pl.BlockSpec((CTX_LEN, 128), lambda b, h, i: (b, h)),
                 pl.BlockSpec((None, 128, CTX_LEN), lambda b, h, i: (b, h, 0))]
    args += [kdc, vdtc]
    in_specs += [pl.BlockSpec((1, HEAD_DIM), lambda b, h, i: (0, 0))] * 4
    args += [lw["lq1"], lw["lk1"], lw["lq2"], lw["lk2"]]
    in_specs += [pl.BlockSpec((DA_V_DIM, 1), lambda b, h, i: (0, 0))]
    args += [lw["sub_g"]]
    return pl.pallas_call(
        functools.partial(_da_kernel, nk=nk, kt=kt, lam_init=lam_init),
        grid=(batch, DA_HEADS, nt),
        in_specs=in_specs,
        out_specs=pl.BlockSpec((tq, 128), lambda b, h, i: (b * nt + i, h)),
        out_shape=jax.ShapeDtypeStruct((batch * nq, DA_WIDTH), BF16),
        scratch_shapes=[
            pltpu.VMEM((nkeys, 128), BF16),
            pltpu.VMEM((DA_V_DIM + ONES_ROWS, nkeys), BF16),
            pltpu.VMEM((2, 2, kt, tq), F32),
            pltpu.VMEM((2, 2, 1, tq), F32),
            pltpu.VMEM((2, DA_V_DIM + ONES_ROWS, tq), F32),
            pltpu.VMEM((2, 1, tq), F32),
        ],
        compiler_params=pltpu.CompilerParams(
            dimension_semantics=("arbitrary", "arbitrary", "arbitrary"), vmem_limit_bytes=V7X_VMEM_LIMIT),
        name="diff_attn" if nk else "diff_attn_ctx",
    )(*args)


def _ffn_kernel(x_ref, oa_ref, ob_ref, oc_ref, gate1_ref, shift2_ref, scale2_ref, gate2_ref, g2_ref,
                wout_ref, w1_ref, w3_ref, w2_ref, o_ref):
    mixed = (_dot(oa_ref[...], wout_ref[0:NA_WIDTH, :])
             + _dot(ob_ref[...], wout_ref[NA_WIDTH:NA_WIDTH + GM_WIDTH, :])
             + _dot(oc_ref[...], wout_ref[NA_WIDTH + GM_WIDTH:, :]))
    x1 = x_ref[...] + gate1_ref[0] * mixed
    h = x1 * lax.rsqrt(jnp.mean(x1 * x1, axis=-1, keepdims=True) + EPS) * g2_ref[...]
    hb = (h * (1.0 + scale2_ref[0]) + shift2_ref[0]).astype(BF16)
    a = _dot(hb, w1_ref[...])
    b = _dot(hb, w3_ref[...])
    g = (a * jax.nn.sigmoid(a) * b).astype(BF16)
    o_ref[...] = x1 + gate2_ref[0] * _dot(g, w2_ref[...])


def _out_ffn(x2d, oa, ob, oc, mod3, lw, *, batch, n, tm, per_batch_mod):
    nt = n // tm
    row = (lambda b: b) if per_batch_mod else (lambda b: CTX_MOD_ROW)
    const = lambda b, i: (0, 0)
    tok = lambda b, i: (b * nt + i, 0)
    mod_spec = lambda k: pl.BlockSpec((1, 1, D_MODEL), lambda b, i: (row(b), 0, k))
    return pl.pallas_call(
        _ffn_kernel,
        grid=(batch, nt),
        in_specs=[
            pl.BlockSpec((tm, D_MODEL), tok),
            pl.BlockSpec((tm, NA_WIDTH), tok),
            pl.BlockSpec((tm, GM_WIDTH), tok),
            pl.BlockSpec((tm, DA_WIDTH), tok),
            mod_spec(2), mod_spec(3), mod_spec(4), mod_spec(5),
            pl.BlockSpec((1, D_MODEL), const),
            _resident((D_MODEL, D_MODEL), const),
            _resident((D_MODEL, D_FF), const),
            _resident((D_MODEL, D_FF), const),
            _resident((D_FF, D_MODEL), const),
        ],
        out_specs=pl.BlockSpec((tm, D_MODEL), tok),
        out_shape=jax.ShapeDtypeStruct((batch * n, D_MODEL), F32),
        compiler_params=pltpu.CompilerParams(
            dimension_semantics=("arbitrary", "arbitrary"), vmem_limit_bytes=V7X_VMEM_LIMIT),
        name="out_ffn",
    )(x2d, oa, ob, oc, mod3, mod3, mod3, mod3, lw["norm2_g"], lw["w_out"], lw["w1"], lw["w3"], lw["w2"])


def _rope_tables(n):
    t = jnp.arange(n, dtype=jnp.int32)
    row = (t // GRID_W).astype(F32)
    col = (t % GRID_W).astype(F32)
    half = HEAD_DIM // 2
    inv = ROPE_BASE ** (-jnp.arange(0, half, 2, dtype=F32) / half)
    ar = row[:, None] * inv[None, :]
    ac = col[:, None] * inv[None, :]
    ang = jnp.concatenate([ar, ar, ac, ac], axis=-1)
    cos = jnp.cos(ang)
    sin = jnp.sin(ang)
    first = (np.arange(HEAD_DIM) % 32) < 16
    sa = jnp.where(first, -sin, 0.0)
    sb = jnp.where(first, 0.0, sin)
    return tuple(jnp.tile(a, (1, 2)) for a in (cos, sa, sb))


def _identity_rope_tables(n):
    return (jnp.ones((n, 128), F32), jnp.zeros((n, 128), F32), jnp.zeros((n, 128), F32))


def _split3(x):
    hi = x.astype(BF16)
    rest = x - hi.astype(F32)
    mid = rest.astype(BF16)
    return hi, mid, (rest - mid.astype(F32)).astype(BF16)


def _na_bias_kernel(rsel_ref, rpb_ref, csel_ref, o_ref):
    picked = sum(_dot(rsel_ref[...], part) for part in _split3(rpb_ref[...]))
    o_ref[...] = sum(_dot(part, csel_ref[...]) for part in _split3(picked))


def _na_bias_tables(rpb, rows, shift):
    cq = np.arange(GRID_W)
    c0 = np.clip(cq - NA_WIN_W // 2, 0, GRID_W - NA_WIN_W)
    col_ok = (cq[None, :] >= c0[:, None]) & (cq[None, :] < c0[:, None] + NA_WIN_W)
    coff = np.clip(cq[None, :] - cq[:, None], -(NA_WIN_W - 1), NA_WIN_W - 1) + (NA_WIN_W - 1)
    col_sel = (coff[None] == np.arange(2 * NA_WIN_W - 1)[:, None, None]).astype(np.float32)
    row_sel, ok = [], []
    for r0 in (0, 2 * NA_Q_ROWS, rows - NA_Q_ROWS):
        u0 = min(max(r0 - NA_WIN_H // 2, 0), rows - NA_K_ROWS)
        r = r0 + np.arange(NA_Q_ROWS)
        key_row = u0 + np.arange(NA_K_ROWS)
        start = np.clip(r - NA_WIN_H // 2, 0, rows - NA_WIN_H)
        row_ok = (key_row[None, :] >= start[:, None]) & (key_row[None, :] < start[:, None] + NA_WIN_H)
        roff = key_row[None, :] - r[:, None] + (NA_WIN_H - 1)
        row_sel.append((roff[None] == np.arange(2 * NA_WIN_H - 1)[:, None, None]).astype(np.float32))
        ok.append(row_ok[:, None, :, None] & col_ok[None, :, None, :])
    n_r, n_c = 2 * NA_WIN_H - 1, 2 * NA_WIN_W - 1
    n_aj, n_qk = NA_Q_ROWS * NA_K_ROWS, GRID_W * GRID_W
    row_sel_p = np.zeros((3, n_aj, 128), np.float32)
    row_sel_p[:, :, :n_r] = np.stack(row_sel).reshape(3, n_r, n_aj).transpose(0, 2, 1)
    col_sel_p = np.zeros((128, n_qk), np.float32)
    col_sel_p[:n_c] = col_sel.reshape(n_c, n_qk)
    rpb_p = jnp.zeros((NA_HEADS, 128, 128), F32).at[:, :n_r, :n_c].set(rpb.astype(F32))
    bias = pl.pallas_call(
        _na_bias_kernel,
        grid=(3, NA_HEADS),
        in_specs=[
            pl.BlockSpec((None, n_aj, 128), lambda t, h: (t, 0, 0)),
            pl.BlockSpec((None, 128, 128), lambda t, h: (h, 0, 0)),
            pl.BlockSpec((128, n_qk), lambda t, h: (0, 0)),
        ],
        out_specs=pl.BlockSpec((None, None, n_aj, n_qk), lambda t, h: (t, h, 0, 0)),
        out_shape=jax.ShapeDtypeStruct((3, NA_HEADS, n_aj, n_qk), F32),
        name="na_bias",
    )(jnp.asarray(row_sel_p, BF16), rpb_p, jnp.asarray(col_sel_p, BF16))
    bias = bias.reshape(3, NA_HEADS, NA_Q_ROWS, NA_K_ROWS, GRID_W, GRID_W).transpose(0, 1, 2, 4, 3, 5)
    tab = jnp.where(jnp.asarray(np.stack(ok))[:, None], bias * LOG2E - shift, NEG_INF)
    return tab.reshape(3, NA_HEADS, NA_Q_ROWS * GRID_W, NA_K_ROWS * GRID_W)


def _layer_weights(i, p):
    tile4 = lambda g: jnp.tile(g.astype(F32), 256 // HEAD_DIM).reshape(1, 256)
    blk = np.arange(256) // HEAD_DIM
    return {
        "norm1_g": p["norm1_g"][i].reshape(1, D_MODEL),
        "w_in": p["w_in"][i].astype(BF16),
        "gqa": tile4(p["na_q_g"][i]) * (HEAD_DIM ** -0.5 * LOG2E),
        "gka": tile4(p["na_k_g"][i]),
        "gqd": tile4(p["da_q_g"][i]) * (HEAD_DIM ** -0.5 * LOG2E),
        "gkd": tile4(p["da_k_g"][i]),
        "gmat": jnp.asarray(blk[:, None] == blk[None, :], BF16),
        "gv": p["gm_v_g"][i].reshape(1, GM_WIDTH),
        "ws_cat": p["gm_ws"][i].transpose(1, 0, 2).reshape(GM_CHUNK, GM_GROUPS * GM_CHUNK).astype(BF16),
        "bs_tab": jnp.repeat(p["gm_bs"][i].T, GM_WIDTH // GM_GROUPS, axis=1),
        "lq1": p["da_lq1"][i].reshape(1, HEAD_DIM),
        "lk1": p["da_lk1"][i].reshape(1, HEAD_DIM),
        "lq2": p["da_lq2"][i].reshape(1, HEAD_DIM),
        "lk2": p["da_lk2"][i].reshape(1, HEAD_DIM),
        "sub_g": p["da_sub_g"][i].reshape(DA_V_DIM, 1),
        "norm2_g": p["norm2_g"][i].reshape(1, D_MODEL),
        "w_out": p["w_out"][i].astype(BF16),
        "w1": p["ffn_w1"][i].astype(BF16),
        "w3": p["ffn_w3"][i].astype(BF16),
        "w2": p["ffn_w2"][i].astype(BF16),
    }


def kernel(x, c, ctx, c_ctx, w_mod, b_mod, norm1_g, w_in, na_q_g, na_k_g, na_rpb, gm_v_g, gm_ws, gm_bs,
           da_q_g, da_k_g, da_lq1, da_lk1, da_lq2, da_lk2, da_sub_g, w_out, norm2_g, ffn_w1, ffn_w3, ffn_w2):
    batch, n, _ = x.shape
    depth = w_mod.shape[0]
    assert n % (NA_Q_ROWS * GRID_W) == 0 and ctx.shape[1] == CTX_LEN and batch < CTX_MOD_ROW + 1
    params = dict(norm1_g=norm1_g, w_in=w_in, na_q_g=na_q_g, na_k_g=na_k_g, gm_v_g=gm_v_g, gm_ws=gm_ws,
                  gm_bs=gm_bs, da_q_g=da_q_g, da_k_g=da_k_g, da_lq1=da_lq1, da_lk1=da_lk1, da_lq2=da_lq2,
                  da_lk2=da_lk2, da_sub_g=da_sub_g, w_out=w_out, norm2_g=norm2_g, ffn_w1=ffn_w1,
                  ffn_w3=ffn_w3, ffn_w2=ffn_w2)

    c8 = jnp.zeros((MOD_ROWS, D_MODEL), F32).at[:batch].set(c).at[CTX_MOD_ROW].set(c_ctx)
    mod_all = _modulation(c8, w_mod, b_mod)

    rope_lat = _rope_tables(n)
    rope_ctx = _identity_rope_tables(CTX_LEN)
    x2d = x.reshape(batch * n, D_MODEL)
    xc2d = ctx.reshape(batch * CTX_LEN, D_MODEL)

    for i in range(depth):
        lam_init = 0.8 - 0.6 * math.exp(-0.3 * i)
        lw = _layer_weights(i, params)
        mod3 = mod_all[i].reshape(MOD_ROWS, 1, 6 * D_MODEL)

        qa, qd, ka, va, kd, vdt, ob = _inproj(x2d, mod3, lw, rope_lat, batch=batch, n=n, tm=INPROJ_TM,
                                              per_batch_mod=True)
        qac, qdc, kac, vac, kdc, vdtc, obc = _inproj(xc2d, mod3, lw, rope_ctx, batch=batch, n=CTX_LEN,
                                                     tm=CTX_LEN, per_batch_mod=False)
        na_bound = (_score_bound(na_q_g[i], na_k_g[i]) + LOG2E * jnp.max(jnp.abs(na_rpb[i]))).astype(F32)
        na_bounded = na_bound <= MAX_CONSTANT_SHIFT
        tab = _na_bias_tables(na_rpb[i], n // GRID_W, jnp.where(na_bounded, na_bound, 0.0))
        oa = lax.cond(
            na_bounded,
            lambda *a: _na_window(*a[:6], a[6].reshape(1, 1), batch=batch, n=n),
            lambda *a: _na_window(*a[:6], batch=batch, n=n),
            qa, ka, va, kac, vac, tab, na_bound)
        da_bound = _score_bound(da_q_g[i], da_k_g[i]).astype(F32)
        oc = lax.cond(
            da_bound <= MAX_CONSTANT_SHIFT,
            lambda *a: _diff_attention_bounded(*a, lw, batch=batch, nq=n, nk=n, tq=DA_BOUNDED_TQ,
                                               kt=DA_BOUNDED_KT, lam_init=lam_init),
            lambda *a: _diff_attention(*a[:5], lw, batch=batch, nq=n, nk=n, tq=DA_ONLINE_TQ, kt=DA_ONLINE_KT,
                                       lam_init=lam_init),
            qd, kd, vdt, kdc, vdtc, da_bound.reshape(1, 1))
        x2d = _out_ffn(x2d, oa, ob, oc, mod3, lw, batch=batch, n=n, tm=FFN_TM, per_batch_mod=True)
        if i < depth - 1:
            oac = _na_dense(qac, kac, vac, batch=batch)
            occ = _diff_attention(qdc, None, None, kdc, vdtc, lw, batch=batch, nq=CTX_LEN, nk=0,
                                  tq=CTX_LEN, kt=CTX_LEN, lam_init=lam_init)
            xc2d = _out_ffn(xc2d, oac, obc, occ, mod3, lw, batch=batch, n=CTX_LEN, tm=CTX_LEN,
                            per_batch_mod=False)
    return x2d.reshape(batch, n, D_MODEL)
```

```python
import functools
import math

import numpy as np
import jax
import jax.numpy as jnp
from jax import lax
from jax.experimental import pallas as pl
from jax.experimental.pallas import tpu as pltpu

D_MODEL = 1024
GRID_W = 64
CTX_LEN = 256
HEAD_DIM = 64
NA_HEADS = 4
NA_WIN_H = 8
NA_WIN_W = 16
NA_WIDTH = NA_HEADS * HEAD_DIM
GM_GROUPS = 4
GM_CHUNK = 128
GM_WIDTH = 256
DA_HEADS = 4
DA_QK_DIM = 2 * HEAD_DIM
DA_V_DIM = 2 * HEAD_DIM
DA_WIDTH = DA_HEADS * DA_V_DIM
QU_WIDTH = NA_WIDTH + DA_HEADS * DA_QK_DIM + 2 * GM_WIDTH
IN_WIDTH = QU_WIDTH + 2 * NA_WIDTH + DA_HEADS * DA_QK_DIM + DA_HEADS * DA_V_DIM
D_FF = -(-8 * D_MODEL // (3 * 256)) * 256
ROPE_BASE = 10000.0
EPS = 1e-6
NEG_INF = -1e30
LOG2E = math.log2(math.e)

COL_QA = 0
COL_QD = NA_WIDTH
COL_UV = COL_QD + DA_HEADS * DA_QK_DIM
COL_KA = QU_WIDTH
COL_VA = COL_KA + NA_WIDTH
COL_KD = COL_VA + NA_WIDTH
COL_VD = COL_KD + DA_HEADS * DA_QK_DIM

MOD_ROWS = 8
CTX_MOD_ROW = 4
NA_Q_ROWS = 4
NA_K_ROWS = 12
ONES_ROWS = 16
MAX_CONSTANT_SHIFT = 48.0

V7X_VMEM_LIMIT = 56 * 2 ** 20
INPROJ_TM = 1024
INPROJ_SUB = 256
FFN_TM = 512
DA_BOUNDED_TQ = 1024
DA_BOUNDED_KT = 1024
DA_ONLINE_TQ = 512
DA_ONLINE_KT = 1408


def _score_bound(gq, gk):
    return 1.02 * HEAD_DIM * (HEAD_DIM ** -0.5 * LOG2E) * jnp.max(jnp.abs(gq)) * jnp.max(jnp.abs(gk))

F32 = jnp.float32
BF16 = jnp.bfloat16


def _dot(a, b):
    return jnp.dot(a, b, preferred_element_type=F32)


def _dot_nt(a, b):
    return lax.dot_general(a, b, (((1,), (1,)), ((), ())), preferred_element_type=F32)


def _resident(shape, index_map):
    return pl.BlockSpec(shape, index_map, pipeline_mode=pl.Buffered(1))


def _mod_kernel(c_ref, w_ref, b_ref, o_ref):
    c = c_ref[...]
    a = c * jax.nn.sigmoid(c)
    a_hi = a.astype(BF16)
    a_lo = (a - a_hi.astype(F32)).astype(BF16)
    w = w_ref[...]
    w_hi = w.astype(BF16)
    w_lo = (w - w_hi.astype(F32)).astype(BF16)
    o_ref[...] = _dot(a_hi, w_hi) + _dot(a_lo, w_hi) + _dot(a_hi, w_lo) + b_ref[...]


def _modulation(c8, w_mod, b_mod):
    depth = w_mod.shape[0]
    tn = 1024
    return pl.pallas_call(
        _mod_kernel,
        grid=(depth, 6 * D_MODEL // tn),
        in_specs=[
            pl.BlockSpec((MOD_ROWS, D_MODEL), lambda l, j: (0, 0)),
            pl.BlockSpec((None, D_MODEL, tn), lambda l, j: (l, 0, j)),
            pl.BlockSpec((None, 1, tn), lambda l, j: (l, 0, j)),
        ],
        out_specs=pl.BlockSpec((None, MOD_ROWS, tn), lambda l, j: (l, 0, j)),
        out_shape=jax.ShapeDtypeStruct((depth, MOD_ROWS, 6 * D_MODEL), F32),
        name="adaln_mod",
    )(c8, w_mod, b_mod.reshape(depth, 1, 6 * D_MODEL))


def _inproj_kernel(x_ref, shift_ref, scale_ref, g1_ref, w_ref, cos_ref, sa_ref, sb_ref,
                   gqa_ref, gka_ref, gqd_ref, gkd_ref, gmat_ref, gv_ref, ws_ref, bs_ref,
                   qa_ref, qd_ref, ka_ref, va_ref, kd_ref, vdt_ref, ob_ref, *, tm, sub):
    gmat = gmat_ref[...]
    group = lax.broadcasted_iota(jnp.int32, (GM_CHUNK, GM_WIDTH), 1) // (GM_WIDTH // GM_GROUPS)
    ws = ws_ref[...]
    bs = bs_ref[...]

    def head_norm(y, g):
        ss = _dot((y * y).astype(BF16), gmat)
        return y * lax.rsqrt(ss * (1.0 / HEAD_DIM) + EPS) * g

    for r0 in range(0, tm, sub):
        rows = slice(r0, r0 + sub)
        x = x_ref[rows, :]
        h = x * lax.rsqrt(jnp.mean(x * x, axis=-1, keepdims=True) + EPS) * g1_ref[...]
        hb = (h * (1.0 + scale_ref[0]) + shift_ref[0]).astype(BF16)
        p = _dot(hb, w_ref[...])
        cos = cos_ref[rows, :]
        sa = sa_ref[rows, :]
        sb = sb_ref[rows, :]

        def rope(z):
            return z * cos + pltpu.roll(z, 128 - 16, 1) * sa + pltpu.roll(z, 16, 1) * sb

        qa_ref[rows, :] = head_norm(p[:, COL_QA:COL_QA + 256], gqa_ref[...]).astype(BF16)
        ka_ref[rows, :] = head_norm(p[:, COL_KA:COL_KA + 256], gka_ref[...]).astype(BF16)
        va_ref[rows, :] = p[:, COL_VA:COL_VA + 256].astype(BF16)
        for c in range(2):
            yq = head_norm(p[:, COL_QD + 256 * c:COL_QD + 256 * c + 256], gqd_ref[...])
            yk = head_norm(p[:, COL_KD + 256 * c:COL_KD + 256 * c + 256], gkd_ref[...])
            for t in range(2):
                lo = 256 * c + 128 * t
                qd_ref[rows, lo:lo + 128] = rope(yq[:, 128 * t:128 * t + 128]).astype(BF16)
                kd_ref[rows, lo:lo + 128] = rope(yk[:, 128 * t:128 * t + 128]).astype(BF16)
        vdt_ref[:, rows] = p[:, COL_VD:COL_VD + 512].T.astype(BF16)

        z = jax.nn.gelu(p[:, COL_UV:COL_UV + 2 * GM_WIDTH])
        u = z[:, :GM_WIDTH]
        v = z[:, GM_WIDTH:]
        v = v * lax.rsqrt(jnp.mean(v * v, axis=-1, keepdims=True) + EPS) * gv_ref[...]
        vb = v.astype(BF16)
        for c in range(sub // GM_CHUNK):
            vc = vb[c * GM_CHUNK:(c + 1) * GM_CHUNK, :]
            vbd = jnp.concatenate([jnp.where(group == g, vc, jnp.zeros_like(vc)) for g in range(GM_GROUPS)],
                                  axis=0)
            s = _dot(ws, vbd) + bs
            lo = r0 + c * GM_CHUNK
            ob_ref[lo:lo + GM_CHUNK, :] = (u[c * GM_CHUNK:(c + 1) * GM_CHUNK, :] * s).astype(BF16)


def _inproj(x2d, mod3, lw, rope_tabs, *, batch, n, tm, per_batch_mod):
    nt = n // tm
    t_tot = batch * n
    row = (lambda b: b) if per_batch_mod else (lambda b: CTX_MOD_ROW)
    const = lambda b, i: (0, 0)
    tok = lambda b, i: (b * nt + i, 0)
    in_specs = [
        pl.BlockSpec((tm, D_MODEL), tok),
        pl.BlockSpec((1, 1, D_MODEL), lambda b, i: (row(b), 0, 0)),
        pl.BlockSpec((1, 1, D_MODEL), lambda b, i: (row(b), 0, 1)),
        pl.BlockSpec((1, D_MODEL), const),
        _resident((D_MODEL, IN_WIDTH), const),
        pl.BlockSpec((tm, 128), lambda b, i: (i, 0)),
        pl.BlockSpec((tm, 128), lambda b, i: (i, 0)),
        pl.BlockSpec((tm, 128), lambda b, i: (i, 0)),
        pl.BlockSpec((1, 256), const),
        pl.BlockSpec((1, 256), const),
        pl.BlockSpec((1, 256), const),
        pl.BlockSpec((1, 256), const),
        pl.BlockSpec((256, 256), const),
        pl.BlockSpec((1, GM_WIDTH), const),
        pl.BlockSpec((GM_CHUNK, GM_GROUPS * GM_CHUNK), const),
        pl.BlockSpec((GM_CHUNK, GM_WIDTH), const),
    ]
    out_specs = [
        pl.BlockSpec((tm, 256), tok),
        pl.BlockSpec((tm, 512), tok),
        pl.BlockSpec((tm, 256), tok),
        pl.BlockSpec((tm, 256), tok),
        pl.BlockSpec((tm, 512), tok),
        pl.BlockSpec((None, 512, tm), lambda b, i: (b, 0, i)),
        pl.BlockSpec((tm, 256), tok),
    ]
    out_shape = [
        jax.ShapeDtypeStruct((t_tot, 256), BF16),
        jax.ShapeDtypeStruct((t_tot, 512), BF16),
        jax.ShapeDtypeStruct((t_tot, 256), BF16),
        jax.ShapeDtypeStruct((t_tot, 256), BF16),
        jax.ShapeDtypeStruct((t_tot, 512), BF16),
        jax.ShapeDtypeStruct((batch, 512, n), BF16),
        jax.ShapeDtypeStruct((t_tot, 256), BF16),
    ]
    return pl.pallas_call(
        functools.partial(_inproj_kernel, tm=tm, sub=min(tm, INPROJ_SUB)),
        grid=(batch, nt),
        in_specs=in_specs,
        out_specs=out_specs,
        out_shape=out_shape,
        compiler_params=pltpu.CompilerParams(
            dimension_semantics=("arbitrary", "arbitrary"), vmem_limit_bytes=V7X_VMEM_LIMIT),
        name="inproj",
    )(x2d, mod3, mod3, lw["norm1_g"], lw["w_in"], *rope_tabs,
      lw["gqa"], lw["gka"], lw["gqd"], lw["gkd"], lw["gmat"], lw["gv"], lw["ws_cat"], lw["bs_tab"])


def _na_kernel(*refs, has_window, bounded=False):
    if bounded:
        q_ref, k_ref, v_ref, kc_ref, vc_ref, tab_ref, bound_ref, o_ref = refs
        refs = refs[:6] + refs[7:]
    if has_window:
        q_ref, k_ref, v_ref, kc_ref, vc_ref, tab_ref, o_ref = refs
        i = pl.program_id(1)
        u0 = jnp.clip(NA_Q_ROWS * i - NA_WIN_H // 2, 0, k_ref.shape[0] // GRID_W - NA_K_ROWS)
        off = pl.multiple_of(u0 * GRID_W, GRID_W)
        kw = k_ref[pl.ds(off, NA_K_ROWS * GRID_W), :]
        vw = v_ref[pl.ds(off, NA_K_ROWS * GRID_W), :]
    else:
        q_ref, kc_ref, vc_ref, o_ref = refs
    lane = lax.broadcasted_iota(jnp.int32, (1, 128), 1)
    lane_head = lane // HEAD_DIM
    for p in range(NA_HEADS // 2):
        cols = slice(128 * p, 128 * p + 128)
        qp = q_ref[:, cols]
        kcp = kc_ref[:, cols]
        vcp = vc_ref[:, cols]
        acc = jnp.zeros((qp.shape[0], 128), F32)
        for j in range(2):
            sel = (lane_head == j).astype(BF16)
            qm = qp * sel
            sc = _dot_nt(qm, kcp)
            if has_window:
                sw = _dot_nt(qm, kw[:, cols]) + tab_ref[2 * p + j]
            if bounded:
                ec = jnp.exp2(sc - bound_ref[...])
                ew = jnp.exp2(sw)
                l = jnp.sum(ec, axis=1, keepdims=True) + jnp.sum(ew, axis=1, keepdims=True)
                o = _dot(ec.astype(BF16), vcp * sel) + _dot(ew.astype(BF16), vw[:, cols] * sel)
                acc = acc + o / l
                continue
            m = jnp.max(sc, axis=1, keepdims=True)
            if has_window:
                m = jnp.maximum(m, jnp.max(sw, axis=1, keepdims=True))
            ec = jnp.exp2(sc - m)
            l = jnp.sum(ec, axis=1, keepdims=True)
            o = _dot(ec.astype(BF16), vcp * sel)
            if has_window:
                ew = jnp.exp2(sw - m)
                l = l + jnp.sum(ew, axis=1, keepdims=True)
                o = o + _dot(ew.astype(BF16), vw[:, cols] * sel)
            acc = acc + o / l
        o_ref[:, cols] = acc.astype(BF16)


def _na_window(qa, ka, va, kac, vac, tab, bound=None, *, batch, n):
    tq = NA_Q_ROWS * GRID_W
    nt = n // tq
    in_specs = [
        pl.BlockSpec((tq, 256), lambda b, i: (b * nt + i, 0)),
        pl.BlockSpec((n, 256), lambda b, i: (b, 0)),
        pl.BlockSpec((n, 256), lambda b, i: (b, 0)),
        pl.BlockSpec((CTX_LEN, 256), lambda b, i: (b, 0)),
        pl.BlockSpec((CTX_LEN, 256), lambda b, i: (b, 0)),
        pl.BlockSpec((None, NA_HEADS, tq, NA_K_ROWS * GRID_W),
                     lambda b, i: (jnp.where(i == 0, 0, jnp.where(i == nt - 1, 2, 1)), 0, 0, 0)),
    ]
    args = [qa, ka, va, kac, vac, tab]
    if bound is not None:
        in_specs.append(pl.BlockSpec((1, 1), lambda b, i: (0, 0)))
        args.append(bound)
    return pl.pallas_call(
        functools.partial(_na_kernel, has_window=True, bounded=bound is not None),
        grid=(batch, nt),
        in_specs=in_specs,
        out_specs=pl.BlockSpec((tq, 256), lambda b, i: (b * nt + i, 0)),
        out_shape=jax.ShapeDtypeStruct((batch * n, 256), BF16),
        compiler_params=pltpu.CompilerParams(
            dimension_semantics=("arbitrary", "arbitrary"), vmem_limit_bytes=V7X_VMEM_LIMIT),
        name="na_window_bounded" if bound is not None else "na_window",
    )(*args)


def _na_dense(qac, kac, vac, *, batch):
    spec = pl.BlockSpec((CTX_LEN, 256), lambda b: (b, 0))
    return pl.pallas_call(
        functools.partial(_na_kernel, has_window=False),
        grid=(batch,),
        in_specs=[spec, spec, spec],
        out_specs=spec,
        out_shape=jax.ShapeDtypeStruct((batch * CTX_LEN, 256), BF16),
        name="na_dense",
    )(qac, kac, vac)


def _da_kernel(*refs, nk, kt, lam_init):
    if nk:
        q_ref, k_ref, vt_ref, kc_ref, vtc_ref, lq1, lk1, lq2, lk2, subg_ref, o_ref = refs[:11]
    else:
        q_ref, kc_ref, vtc_ref, lq1, lk1, lq2, lk2, subg_ref, o_ref = refs[:9]
    kall, vtall, s_buf, bm_buf, acc_ref, m_ref = refs[-6:]
    tq = o_ref.shape[0]
    nt = q_ref.shape[0] // tq
    nblk = (nk + CTX_LEN) // kt
    i = pl.program_id(2)
    lane_map = lax.broadcasted_iota(jnp.int32, (1, 128), 1) // HEAD_DIM
    sel = [(lane_map == mi).astype(BF16) for mi in range(2)]

    def scores(tile, blk, slot):
        q = q_ref[pl.ds(pl.multiple_of(tile * tq, tq), tq), :]
        kblk = kall[pl.ds(pl.multiple_of(blk * kt, kt), kt), :]
        for mi in range(2):
            s = _dot_nt(kblk, q * sel[mi])
            s_buf[slot, mi] = s
            bm_buf[slot, mi] = jnp.max(s, axis=0, keepdims=True)

    @pl.when(i == 0)
    def _():
        if nk:
            kall[0:nk, :] = k_ref[...]
            vtall[0:DA_V_DIM, 0:nk] = vt_ref[...]
        kall[nk:nk + CTX_LEN, :] = kc_ref[...]
        vtall[0:DA_V_DIM, nk:nk + CTX_LEN] = vtc_ref[...]
        vtall[DA_V_DIM:, :] = jnp.ones((ONES_ROWS, nk + CTX_LEN), BF16)
        scores(0, 0, 0)

    acc_ref[...] = jnp.zeros_like(acc_ref)
    m_ref[...] = jnp.full_like(m_ref, NEG_INF)

    def softmax_pv(blk, slot):
        vte = vtall[:, pl.ds(pl.multiple_of(blk * kt, kt), kt)]
        for mi in range(2):
            for g in range(tq // 256):
                cols = slice(256 * g, 256 * g + 256)
                m_old = m_ref[mi, :, cols]
                m_new = jnp.maximum(m_old, bm_buf[slot, mi, :, cols])
                alpha = jnp.exp2(m_old - m_new)
                e = jnp.exp2(s_buf[slot, mi, :, cols] - m_new).astype(BF16)
                acc_ref[mi, :, cols] = acc_ref[mi, :, cols] * alpha + _dot(vte, e)
                m_ref[mi, :, cols] = m_new

    if nblk == 1:
        softmax_pv(0, 0)
    else:
        for blk in range(nblk):
            if blk + 1 < nblk:
                scores(i, blk + 1, (blk + 1) % 2)
            else:
                scores(jnp.minimum(i + 1, nt - 1), 0, 0)
            softmax_pv(blk, blk % 2)

    _da_finalize(acc_ref[0], acc_ref[1], lq1, lk1, lq2, lk2, subg_ref, o_ref, lam_init)


def _da_finalize(a0, a1, lq1, lk1, lq2, lk2, subg_ref, o_ref, lam_init):
    lam = (jnp.exp(jnp.sum(lq1[...] * lk1[...], keepdims=True))
           - jnp.exp(jnp.sum(lq2[...] * lk2[...], keepdims=True)) + lam_init)
    o = a0[:DA_V_DIM] / a0[DA_V_DIM:DA_V_DIM + 1] - lam * (a1[:DA_V_DIM] / a1[DA_V_DIM:DA_V_DIM + 1])
    y = o * lax.rsqrt(jnp.mean(o * o, axis=0, keepdims=True) + EPS) * subg_ref[...] * (1.0 - lam_init)
    o_ref[...] = y.T.astype(BF16)


def _da_bounded_kernel(q_ref, k_ref, vt_ref, kc_ref, vtc_ref, bound_ref, lq1, lk1, lq2, lk2, subg_ref, o_ref,
                       *, kt, lam_init):
    q = q_ref[...]
    lane_map = lax.broadcasted_iota(jnp.int32, (1, 128), 1) // HEAD_DIM
    qms = [q * (lane_map == mi).astype(BF16) for mi in range(2)]
    shift = bound_ref[...]
    acc = [None, None]

    def block(kblk, vtblk):
        vte = jnp.concatenate([vtblk, jnp.ones((ONES_ROWS, vtblk.shape[1]), BF16)], axis=0)
        for mi in range(2):
            e = jnp.exp2(_dot_nt(kblk, qms[mi]) - shift).astype(BF16)
            d = _dot(vte, e)
            acc[mi] = d if acc[mi] is None else acc[mi] + d

    for j in range(k_ref.shape[0] // kt):
        block(k_ref[j * kt:(j + 1) * kt, :], vt_ref[:, j * kt:(j + 1) * kt])
    block(kc_ref[...], vtc_ref[...])
    _da_finalize(acc[0], acc[1], lq1, lk1, lq2, lk2, subg_ref, o_ref, lam_init)


def _diff_attention_bounded(qd, kd, vdt, kdc, vdtc, bound, lw, *, batch, nq, nk, tq, kt, lam_init):
    nt = nq // tq
    assert nk % kt == 0 and nq % tq == 0
    small = lambda shape: pl.BlockSpec(shape, lambda b, h, i: (0, 0))
    return pl.pallas_call(
        functools.partial(_da_bounded_kernel, kt=kt, lam_init=lam_init),
        grid=(batch, DA_HEADS, nt),
        in_specs=[
            pl.BlockSpec((tq, 128), lambda b, h, i: (b * nt + i, h)),
            pl.BlockSpec((nk, 128), lambda b, h, i: (b, h)),
            pl.BlockSpec((None, 128, nk), lambda b, h, i: (b, h, 0)),
            pl.BlockSpec((CTX_LEN, 128), lambda b, h, i: (b, h)),
            pl.BlockSpec((None, 128, CTX_LEN), lambda b, h, i: (b, h, 0)),
            small((1, 1)), small((1, HEAD_DIM)), small((1, HEAD_DIM)), small((1, HEAD_DIM)), small((1, HEAD_DIM)),
            small((DA_V_DIM, 1)),
        ],
        out_specs=pl.BlockSpec((tq, 128), lambda b, h, i: (b * nt + i, h)),
        out_shape=jax.ShapeDtypeStruct((batch * nq, DA_WIDTH), BF16),
        compiler_params=pltpu.CompilerParams(
            dimension_semantics=("arbitrary", "arbitrary", "arbitrary"), vmem_limit_bytes=V7X_VMEM_LIMIT),
        name="diff_attn_bounded",
    )(qd, kd, vdt, kdc, vdtc, bound, lw["lq1"], lw["lk1"], lw["lq2"], lw["lk2"], lw["sub_g"])


def _diff_attention(qd, kd, vdt, kdc, vdtc, lw, *, batch, nq, nk, tq, kt, lam_init):
    nt = nq // tq
    nkeys = nk + CTX_LEN
    nblk = nkeys // kt
    assert nkeys % kt == 0 and tq % 256 == 0 and nq % tq == 0 and (nblk == 1 and nt == 1 or nblk % 2 == 0)
    in_specs = [pl.BlockSpec((nq, 128), lambda b, h, i: (b, h))]
    args = [qd]
    if nk:
        in_specs += [pl.BlockSpec((nk, 128), lambda b, h, i: (b, h)),
                     pl.BlockSpec((None, 128, nk), lambda b, h, i: (b, h, 0))]
        args += [kd, vdt]
    in_specs += [pl.BlockSpec((CTX_LEN, 128), lambda b, h, i: (b, h)),
                 pl.BlockSpec((None, 128, CTX_LEN), lambda b, h, i: (b, h, 0))]
    args += [kdc, vdtc]
    in_specs += [pl.BlockSpec((1, HEAD_DIM), lambda b, h, i: (0, 0))] * 4
    args += [lw["lq1"], lw["lk1"], lw["lq2"], lw["lk2"]]
    in_specs += [pl.BlockSpec((DA_V_DIM, 1), lambda b, h, i: (0, 0))]
    args += [lw["sub_g"]]
    return pl.pallas_call(
        functools.partial(_da_kernel, nk=nk, kt=kt, lam_init=lam_init),
        grid=(batch, DA_HEADS, nt),
        in_specs=in_specs,
        out_specs=pl.BlockSpec((tq, 128), lambda b, h, i: (b * nt + i, h)),
        out_shape=jax.ShapeDtypeStruct((batch * nq, DA_WIDTH), BF16),
        scratch_shapes=[
            pltpu.VMEM((nkeys, 128), BF16),
            pltpu.VMEM((DA_V_DIM + ONES_ROWS, nkeys), BF16),
            pltpu.VMEM((2, 2, kt, tq), F32),
            pltpu.VMEM((2, 2, 1, tq), F32),
            pltpu.VMEM((2, DA_V_DIM + ONES_ROWS, tq), F32),
            pltpu.VMEM((2, 1, tq), F32),
        ],
        compiler_params=pltpu.CompilerParams(
            dimension_semantics=("arbitrary", "arbitrary", "arbitrary"), vmem_limit_bytes=V7X_VMEM_LIMIT),
        name="diff_attn" if nk else "diff_attn_ctx",
    )(*args)


def _ffn_kernel(x_ref, oa_ref, ob_ref, oc_ref, gate1_ref, shift2_ref, scale2_ref, gate2_ref, g2_ref,
                wout_ref, w1_ref, w3_ref, w2_ref, o_ref):
    mixed = (_dot(oa_ref[...], wout_ref[0:NA_WIDTH, :])
             + _dot(ob_ref[...], wout_ref[NA_WIDTH:NA_WIDTH + GM_WIDTH, :])
             + _dot(oc_ref[...], wout_ref[NA_WIDTH + GM_WIDTH:, :]))
    x1 = x_ref[...] + gate1_ref[0] * mixed
    h = x1 * lax.rsqrt(jnp.mean(x1 * x1, axis=-1, keepdims=True) + EPS) * g2_ref[...]
    hb = (h * (1.0 + scale2_ref[0]) + shift2_ref[0]).astype(BF16)
    a = _dot(hb, w1_ref[...])
    b = _dot(hb, w3_ref[...])
    g = (a * jax.nn.sigmoid(a) * b).astype(BF16)
    o_ref[...] = x1 + gate2_ref[0] * _dot(g, w2_ref[...])


def _out_ffn(x2d, oa, ob, oc, mod3, lw, *, batch, n, tm, per_batch_mod):
    nt = n // tm
    row = (lambda b: b) if per_batch_mod else (lambda b: CTX_MOD_ROW)
    const = lambda b, i: (0, 0)
    tok = lambda b, i: (b * nt + i, 0)
    mod_spec = lambda k: pl.BlockSpec((1, 1, D_MODEL), lambda b, i: (row(b), 0, k))
    return pl.pallas_call(
        _ffn_kernel,
        grid=(batch, nt),
        in_specs=[
            pl.BlockSpec((tm, D_MODEL), tok),
            pl.BlockSpec((tm, NA_WIDTH), tok),
            pl.BlockSpec((tm, GM_WIDTH), tok),
            pl.BlockSpec((tm, DA_WIDTH), tok),
            mod_spec(2), mod_spec(3), mod_spec(4), mod_spec(5),
            pl.BlockSpec((1, D_MODEL), const),
            _resident((D_MODEL, D_MODEL), const),
            _resident((D_MODEL, D_FF), const),
            _resident((D_MODEL, D_FF), const),
            _resident((D_FF, D_MODEL), const),
        ],
        out_specs=pl.BlockSpec((tm, D_MODEL), tok),
        out_shape=jax.ShapeDtypeStruct((batch * n, D_MODEL), F32),
        compiler_params=pltpu.CompilerParams(
            dimension_semantics=("arbitrary", "arbitrary"), vmem_limit_bytes=V7X_VMEM_LIMIT),
        name="out_ffn",
    )(x2d, oa, ob, oc, mod3, mod3, mod3, mod3, lw["norm2_g"], lw["w_out"], lw["w1"], lw["w3"], lw["w2"])


def _rope_tables(n):
    t = jnp.arange(n, dtype=jnp.int32)
    row = (t // GRID_W).astype(F32)
    col = (t % GRID_W).astype(F32)
    half = HEAD_DIM // 2
    inv = ROPE_BASE ** (-jnp.arange(0, half, 2, dtype=F32) / half)
    ar = row[:, None] * inv[None, :]
    ac = col[:, None] * inv[None, :]
    ang = jnp.concatenate([ar, ar, ac, ac], axis=-1)
    cos = jnp.cos(ang)
    sin = jnp.sin(ang)
    first = (np.arange(HEAD_DIM) % 32) < 16
    sa = jnp.where(first, -sin, 0.0)
    sb = jnp.where(first, 0.0, sin)
    return tuple(jnp.tile(a, (1, 2)) for a in (cos, sa, sb))


def _identity_rope_tables(n):
    return (jnp.ones((n, 128), F32), jnp.zeros((n, 128), F32), jnp.zeros((n, 128), F32))


def _na_window_rows(rows):
    roff, row_ok = [], []
    for r0 in (0, 2 * NA_Q_ROWS, rows - NA_Q_ROWS):
        u0 = min(max(r0 - NA_WIN_H // 2, 0), rows - NA_K_ROWS)
        r = r0 + np.arange(NA_Q_ROWS)
        key_row = u0 + np.arange(NA_K_ROWS)
        start = np.clip(r - NA_WIN_H // 2, 0, rows - NA_WIN_H)
        row_ok.append((key_row[None, :] >= start[:, None]) & (key_row[None, :] < start[:, None] + NA_WIN_H))
        roff.append(key_row[None, :] - r[:, None] + (NA_WIN_H - 1))
    return np.stack(roff), np.stack(row_ok)


def _na_bias_kernel(shift_ref, rpb_ref, o_ref, *, roff, row_ok):
    lane = lax.broadcasted_iota(jnp.int32, (GRID_W, 128), 1)
    q = lax.broadcasted_iota(jnp.int32, (GRID_W, 128), 0)
    k = lane % GRID_W
    c0 = jnp.clip(q - NA_WIN_W // 2, 0, GRID_W - NA_WIN_W)
    col_ok = jnp.abs(2 * (k - c0) - (NA_WIN_W - 1)) < NA_WIN_W
    low = lane < GRID_W
    shift = shift_ref[...]

    def toeplitz(t, a, j):
        r = int(roff[t, a, j])
        row = jnp.broadcast_to(rpb_ref[r:r + 1, :], (GRID_W, 128))
        return pltpu.roll(row, 128 - (NA_WIN_W - 1), 1, stride=1, stride_axis=0)

    for t in range(roff.shape[0]):
        for a in range(NA_Q_ROWS):
            for jp in range(NA_K_ROWS // 2):
                ok = [bool(row_ok[t, a, 2 * jp]), bool(row_ok[t, a, 2 * jp + 1])]
                tile = jnp.full((GRID_W, 128), NEG_INF, F32)
                if ok[0] or ok[1]:
                    zero = jnp.zeros((GRID_W, 128), F32)
                    even = toeplitz(t, a, 2 * jp) if ok[0] else zero
                    odd = pltpu.roll(toeplitz(t, a, 2 * jp + 1), GRID_W, 1) if ok[1] else zero
                    tile = jnp.where(col_ok, jnp.where(low, even, odd) * LOG2E - shift, NEG_INF)
                    if not ok[0]:
                        tile = jnp.where(low, NEG_INF, tile)
                    if not ok[1]:
                        tile = jnp.where(low, tile, NEG_INF)
                o_ref[t, a * GRID_W:(a + 1) * GRID_W, jp * 128:(jp + 1) * 128] = tile


def _na_bias_tables(rpb, rows, shift):
    roff, row_ok = _na_window_rows(rows)
    n_r, n_c = 2 * NA_WIN_H - 1, 2 * NA_WIN_W - 1
    rpb_p = jnp.zeros((NA_HEADS, 16, 128), F32).at[:, :n_r, :n_c].set(rpb.astype(F32))
    tq, tk = NA_Q_ROWS * GRID_W, NA_K_ROWS * GRID_W
    return pl.pallas_call(
        functools.partial(_na_bias_kernel, roff=roff, row_ok=row_ok),
        grid=(NA_HEADS,),
        in_specs=[pl.BlockSpec((1, 1), lambda h: (0, 0)), pl.BlockSpec((None, 16, 128), lambda h: (h, 0, 0))],
        out_specs=pl.BlockSpec((3, None, tq, tk), lambda h: (0, h, 0, 0)),
        out_shape=jax.ShapeDtypeStruct((3, NA_HEADS, tq, tk), F32),
        name="na_bias",
    )(jnp.reshape(shift, (1, 1)).astype(F32), rpb_p)


def _layer_weights(i, p):
    tile4 = lambda g: jnp.tile(g.astype(F32), 256 // HEAD_DIM).reshape(1, 256)
    blk = np.arange(256) // HEAD_DIM
    return {
        "norm1_g": p["norm1_g"][i].reshape(1, D_MODEL),
        "w_in": p["w_in"][i].astype(BF16),
        "gqa": tile4(p["na_q_g"][i]) * (HEAD_DIM ** -0.5 * LOG2E),
        "gka": tile4(p["na_k_g"][i]),
        "gqd": tile4(p["da_q_g"][i]) * (HEAD_DIM ** -0.5 * LOG2E),
        "gkd": tile4(p["da_k_g"][i]),
        "gmat": jnp.asarray(blk[:, None] == blk[None, :], BF16),
        "gv": p["gm_v_g"][i].reshape(1, GM_WIDTH),
        "ws_cat": p["gm_ws"][i].transpose(1, 0, 2).reshape(GM_CHUNK, GM_GROUPS * GM_CHUNK).astype(BF16),
        "bs_tab": jnp.repeat(p["gm_bs"][i].T, GM_WIDTH // GM_GROUPS, axis=1),
        "lq1": p["da_lq1"][i].reshape(1, HEAD_DIM),
        "lk1": p["da_lk1"][i].reshape(1, HEAD_DIM),
        "lq2": p["da_lq2"][i].reshape(1, HEAD_DIM),
        "lk2": p["da_lk2"][i].reshape(1, HEAD_DIM),
        "sub_g": p["da_sub_g"][i].reshape(DA_V_DIM, 1),
        "norm2_g": p["norm2_g"][i].reshape(1, D_MODEL),
        "w_out": p["w_out"][i].astype(BF16),
        "w1": p["ffn_w1"][i].astype(BF16),
        "w3": p["ffn_w3"][i].astype(BF16),
        "w2": p["ffn_w2"][i].astype(BF16),
    }


def kernel(x, c, ctx, c_ctx, w_mod, b_mod, norm1_g, w_in, na_q_g, na_k_g, na_rpb, gm_v_g, gm_ws, gm_bs,
           da_q_g, da_k_g, da_lq1, da_lk1, da_lq2, da_lk2, da_sub_g, w_out, norm2_g, ffn_w1, ffn_w3, ffn_w2):
    batch, n, _ = x.shape
    depth = w_mod.shape[0]
    assert n % (NA_Q_ROWS * GRID_W) == 0 and ctx.shape[1] == CTX_LEN and batch < CTX_MOD_ROW + 1
    params = dict(norm1_g=norm1_g, w_in=w_in, na_q_g=na_q_g, na_k_g=na_k_g, gm_v_g=gm_v_g, gm_ws=gm_ws,
                  gm_bs=gm_bs, da_q_g=da_q_g, da_k_g=da_k_g, da_lq1=da_lq1, da_lk1=da_lk1, da_lq2=da_lq2,
                  da_lk2=da_lk2, da_sub_g=da_sub_g, w_out=w_out, norm2_g=norm2_g, ffn_w1=ffn_w1,
                  ffn_w3=ffn_w3, ffn_w2=ffn_w2)

    c8 = jnp.zeros((MOD_ROWS, D_MODEL), F32).at[:batch].set(c).at[CTX_MOD_ROW].set(c_ctx)
    mod_all = _modulation(c8, w_mod, b_mod)

    rope_lat = _rope_tables(n)
    rope_ctx = _identity_rope_tables(CTX_LEN)
    x2d = x.reshape(batch * n, D_MODEL)
    xc2d = ctx.reshape(batch * CTX_LEN, D_MODEL)

    for i in range(depth):
        lam_init = 0.8 - 0.6 * math.exp(-0.3 * i)
        lw = _layer_weights(i, params)
        mod3 = mod_all[i].reshape(MOD_ROWS, 1, 6 * D_MODEL)

        qa, qd, ka, va, kd, vdt, ob = _inproj(x2d, mod3, lw, rope_lat, batch=batch, n=n, tm=INPROJ_TM,
                                              per_batch_mod=True)
        qac, qdc, kac, vac, kdc, vdtc, obc = _inproj(xc2d, mod3, lw, rope_ctx, batch=batch, n=CTX_LEN,
                                                     tm=CTX_LEN, per_batch_mod=False)
        na_bound = (_score_bound(na_q_g[i], na_k_g[i]) + LOG2E * jnp.max(jnp.abs(na_rpb[i]))).astype(F32)
        na_bounded = na_bound <= MAX_CONSTANT_SHIFT
        tab = _na_bias_tables(na_rpb[i], n // GRID_W, jnp.where(na_bounded, na_bound, 0.0))
        oa = lax.cond(
            na_bounded,
            lambda *a: _na_window(*a[:6], a[6].reshape(1, 1), batch=batch, n=n),
            lambda *a: _na_window(*a[:6], batch=batch, n=n),
            qa, ka, va, kac, vac, tab, na_bound)
        da_bound = _score_bound(da_q_g[i], da_k_g[i]).astype(F32)
        oc = lax.cond(
            da_bound <= MAX_CONSTANT_SHIFT,
            lambda *a: _diff_attention_bounded(*a, lw, batch=batch, nq=n, nk=n, tq=DA_BOUNDED_TQ,
                                               kt=DA_BOUNDED_KT, lam_init=lam_init),
            lambda *a: _diff_attention(*a[:5], lw, batch=batch, nq=n, nk=n, tq=DA_ONLINE_TQ, kt=DA_ONLINE_KT,
                                       lam_init=lam_init),
            qd, kd, vdt, kdc, vdtc, da_bound.reshape(1, 1))
        x2d = _out_ffn(x2d, oa, ob, oc, mod3, lw, batch=batch, n=n, tm=FFN_TM, per_batch_mod=True)
        if i < depth - 1:
            oac = _na_dense(qac, kac, vac, batch=batch)
            occ = _diff_attention(qdc, None, None, kdc, vdtc, lw, batch=batch, nq=CTX_LEN, nk=0,
                                  tq=CTX_LEN, kt=CTX_LEN, lam_init=lam_init)
            xc2d = _out_ffn(xc2d, oac, obc, occ, mod3, lw, batch=batch, n=CTX_LEN, tm=CTX_LEN,
                            per_batch_mod=False)
    return x2d.reshape(batch, n, D_MODEL)
```

```python
import functools
import math

import numpy as np
import jax
import jax.numpy as jnp
from jax import lax
from jax.experimental import pallas as pl
from jax.experimental.pallas import tpu as pltpu

D_MODEL = 1024
GRID_W = 64
CTX_LEN = 256
HEAD_DIM = 64
NA_HEADS = 4
NA_WIN_H = 8
NA_WIN_W = 16
NA_WIDTH = NA_HEADS * HEAD_DIM
GM_GROUPS = 4
GM_CHUNK = 128
GM_WIDTH = 256
DA_HEADS = 4
DA_QK_DIM = 2 * HEAD_DIM
DA_V_DIM = 2 * HEAD_DIM
DA_WIDTH = DA_HEADS * DA_V_DIM
QU_WIDTH = NA_WIDTH + DA_HEADS * DA_QK_DIM + 2 * GM_WIDTH
IN_WIDTH = QU_WIDTH + 2 * NA_WIDTH + DA_HEADS * DA_QK_DIM + DA_HEADS * DA_V_DIM
D_FF = -(-8 * D_MODEL // (3 * 256)) * 256
ROPE_BASE = 10000.0
EPS = 1e-6
NEG_INF = -1e30
LOG2E = math.log2(math.e)

COL_QA = 0
COL_QD = NA_WIDTH
COL_UV = COL_QD + DA_HEADS * DA_QK_DIM
COL_KA = QU_WIDTH
COL_VA = COL_KA + NA_WIDTH
COL_KD = COL_VA + NA_WIDTH
COL_VD = COL_KD + DA_HEADS * DA_QK_DIM

MOD_ROWS = 8
CTX_MOD_ROW = 4
NA_Q_ROWS = 4
NA_K_ROWS = 12
ONES_ROWS = 16
MAX_CONSTANT_SHIFT = 48.0

V7X_VMEM_LIMIT = 56 * 2 ** 20
INPROJ_TM = 1024
INPROJ_SUB = 256
FFN_TM = 512
DA_BOUNDED_TQ = 2048
DA_BOUNDED_KT = 1024
DA_ONLINE_TQ = 512
DA_ONLINE_KT = 1408


def _score_bound(gq, gk):
    return 1.02 * HEAD_DIM * (HEAD_DIM ** -0.5 * LOG2E) * jnp.max(jnp.abs(gq)) * jnp.max(jnp.abs(gk))

F32 = jnp.float32
BF16 = jnp.bfloat16


def _dot(a, b):
    return jnp.dot(a, b, preferred_element_type=F32)


def _dot_nt(a, b):
    return lax.dot_general(a, b, (((1,), (1,)), ((), ())), preferred_element_type=F32)


def _resident(shape, index_map):
    return pl.BlockSpec(shape, index_map, pipeline_mode=pl.Buffered(1))


def _mod_kernel(c_ref, w_ref, b_ref, o_ref):
    c = c_ref[...]
    a = c * jax.nn.sigmoid(c)
    a_hi = a.astype(BF16)
    a_lo = (a - a_hi.astype(F32)).astype(BF16)
    w = w_ref[...]
    w_hi = w.astype(BF16)
    w_lo = (w - w_hi.astype(F32)).astype(BF16)
    o_ref[...] = _dot(a_hi, w_hi) + _dot(a_lo, w_hi) + _dot(a_hi, w_lo) + b_ref[...]


def _modulation(c8, w_mod, b_mod):
    depth = w_mod.shape[0]
    tn = 1024
    return pl.pallas_call(
        _mod_kernel,
        grid=(depth, 6 * D_MODEL // tn),
        in_specs=[
            pl.BlockSpec((MOD_ROWS, D_MODEL), lambda l, j: (0, 0)),
            pl.BlockSpec((None, D_MODEL, tn), lambda l, j: (l, 0, j)),
            pl.BlockSpec((None, 1, tn), lambda l, j: (l, 0, j)),
        ],
        out_specs=pl.BlockSpec((None, MOD_ROWS, tn), lambda l, j: (l, 0, j)),
        out_shape=jax.ShapeDtypeStruct((depth, MOD_ROWS, 6 * D_MODEL), F32),
        name="adaln_mod",
    )(c8, w_mod, b_mod.reshape(depth, 1, 6 * D_MODEL))


def _inproj_kernel(x_ref, shift_ref, scale_ref, g1_ref, w_ref, cos_ref, sa_ref, sb_ref,
                   gqa_ref, gka_ref, gqd_ref, gkd_ref, gmat_ref, gv_ref, ws_ref, bs_ref,
                   qa_ref, qd_ref, ka_ref, va_ref, kd_ref, vdt_ref, ob_ref, *, tm, sub):
    gmat = gmat_ref[...]
    group = lax.broadcasted_iota(jnp.int32, (GM_CHUNK, GM_WIDTH), 1) // (GM_WIDTH // GM_GROUPS)
    ws = ws_ref[...]
    bs = bs_ref[...]

    def head_norm(y, g):
        ss = _dot((y * y).astype(BF16), gmat)
        return y * lax.rsqrt(ss * (1.0 / HEAD_DIM) + EPS) * g

    for r0 in range(0, tm, sub):
        rows = slice(r0, r0 + sub)
        x = x_ref[rows, :]
        h = x * lax.rsqrt(jnp.mean(x * x, axis=-1, keepdims=True) + EPS) * g1_ref[...]
        hb = (h * (1.0 + scale_ref[0]) + shift_ref[0]).astype(BF16)
        p = _dot(hb, w_ref[...])
        cos = cos_ref[rows, :]
        sa = sa_ref[rows, :]
        sb = sb_ref[rows, :]

        def rope(z):
            return z * cos + pltpu.roll(z, 128 - 16, 1) * sa + pltpu.roll(z, 16, 1) * sb

        qa_ref[rows, :] = head_norm(p[:, COL_QA:COL_QA + 256], gqa_ref[...]).astype(BF16)
        ka_ref[rows, :] = head_norm(p[:, COL_KA:COL_KA + 256], gka_ref[...]).astype(BF16)
        va_ref[rows, :] = p[:, COL_VA:COL_VA + 256].astype(BF16)
        for c in range(2):
            yq = head_norm(p[:, COL_QD + 256 * c:COL_QD + 256 * c + 256], gqd_ref[...])
            yk = head_norm(p[:, COL_KD + 256 * c:COL_KD + 256 * c + 256], gkd_ref[...])
            for t in range(2):
                lo = 256 * c + 128 * t
                qd_ref[rows, lo:lo + 128] = rope(yq[:, 128 * t:128 * t + 128]).astype(BF16)
                kd_ref[rows, lo:lo + 128] = rope(yk[:, 128 * t:128 * t + 128]).astype(BF16)
        vdt_ref[:, rows] = p[:, COL_VD:COL_VD + 512].T.astype(BF16)

        z = jax.nn.gelu(p[:, COL_UV:COL_UV + 2 * GM_WIDTH])
        u = z[:, :GM_WIDTH]
        v = z[:, GM_WIDTH:]
        v = v * lax.rsqrt(jnp.mean(v * v, axis=-1, keepdims=True) + EPS) * gv_ref[...]
        vb = v.astype(BF16)
        for c in range(sub // GM_CHUNK):
            vc = vb[c * GM_CHUNK:(c + 1) * GM_CHUNK, :]
            vbd = jnp.concatenate([jnp.where(group == g, vc, jnp.zeros_like(vc)) for g in range(GM_GROUPS)],
                                  axis=0)
            s = _dot(ws, vbd) + bs
            lo = r0 + c * GM_CHUNK
            ob_ref[lo:lo + GM_CHUNK, :] = (u[c * GM_CHUNK:(c + 1) * GM_CHUNK, :] * s).astype(BF16)


def _inproj(x2d, mod3, lw, rope_tabs, *, batch, n, tm, per_batch_mod):
    nt = n // tm
    t_tot = batch * n
    row = (lambda b: b) if per_batch_mod else (lambda b: CTX_MOD_ROW)
    const = lambda b, i: (0, 0)
    tok = lambda b, i: (b * nt + i, 0)
    in_specs = [
        pl.BlockSpec((tm, D_MODEL), tok),
        pl.BlockSpec((1, 1, D_MODEL), lambda b, i: (row(b), 0, 0)),
        pl.BlockSpec((1, 1, D_MODEL), lambda b, i: (row(b), 0, 1)),
        pl.BlockSpec((1, D_MODEL), const),
        _resident((D_MODEL, IN_WIDTH), const),
        pl.BlockSpec((tm, 128), lambda b, i: (i, 0)),
        pl.BlockSpec((tm, 128), lambda b, i: (i, 0)),
        pl.BlockSpec((tm, 128), lambda b, i: (i, 0)),
        pl.BlockSpec((1, 256), const),
        pl.BlockSpec((1, 256), const),
        pl.BlockSpec((1, 256), const),
        pl.BlockSpec((1, 256), const),
        pl.BlockSpec((256, 256), const),
        pl.BlockSpec((1, GM_WIDTH), const),
        pl.BlockSpec((GM_CHUNK, GM_GROUPS * GM_CHUNK), const),
        pl.BlockSpec((GM_CHUNK, GM_WIDTH), const),
    ]
    out_specs = [
        pl.BlockSpec((tm, 256), tok),
        pl.BlockSpec((tm, 512), tok),
        pl.BlockSpec((tm, 256), tok),
        pl.BlockSpec((tm, 256), tok),
        pl.BlockSpec((tm, 512), tok),
        pl.BlockSpec((None, 512, tm), lambda b, i: (b, 0, i)),
        pl.BlockSpec((tm, 256), tok),
    ]
    out_shape = [
        jax.ShapeDtypeStruct((t_tot, 256), BF16),
        jax.ShapeDtypeStruct((t_tot, 512), BF16),
        jax.ShapeDtypeStruct((t_tot, 256), BF16),
        jax.ShapeDtypeStruct((t_tot, 256), BF16),
        jax.ShapeDtypeStruct((t_tot, 512), BF16),
        jax.ShapeDtypeStruct((batch, 512, n), BF16),
        jax.ShapeDtypeStruct((t_tot, 256), BF16),
    ]
    return pl.pallas_call(
        functools.partial(_inproj_kernel, tm=tm, sub=min(tm, INPROJ_SUB)),
        grid=(batch, nt),
        in_specs=in_specs,
        out_specs=out_specs,
        out_shape=out_shape,
        compiler_params=pltpu.CompilerParams(
            dimension_semantics=("arbitrary", "arbitrary"), vmem_limit_bytes=V7X_VMEM_LIMIT),
        name="inproj",
    )(x2d, mod3, mod3, lw["norm1_g"], lw["w_in"], *rope_tabs,
      lw["gqa"], lw["gka"], lw["gqd"], lw["gkd"], lw["gmat"], lw["gv"], lw["ws_cat"], lw["bs_tab"])


def _na_kernel(*refs, has_window, bounded=False):
    if bounded:
        q_ref, k_ref, v_ref, kc_ref, vc_ref, tab_ref, bound_ref, o_ref = refs
        refs = refs[:6] + refs[7:]
    if has_window:
        q_ref, k_ref, v_ref, kc_ref, vc_ref, tab_ref, o_ref = refs
        i = pl.program_id(1)
        u0 = jnp.clip(NA_Q_ROWS * i - NA_WIN_H // 2, 0, k_ref.shape[0] // GRID_W - NA_K_ROWS)
        off = pl.multiple_of(u0 * GRID_W, GRID_W)
        kw = k_ref[pl.ds(off, NA_K_ROWS * GRID_W), :]
        vw = v_ref[pl.ds(off, NA_K_ROWS * GRID_W), :]
    else:
        q_ref, kc_ref, vc_ref, o_ref = refs
    lane = lax.broadcasted_iota(jnp.int32, (1, 128), 1)
    lane_head = lane // HEAD_DIM
    for p in range(NA_HEADS // 2):
        cols = slice(128 * p, 128 * p + 128)
        qp = q_ref[:, cols]
        kcp = kc_ref[:, cols]
        vcp = vc_ref[:, cols]
        acc = jnp.zeros((qp.shape[0], 128), F32)
        for j in range(2):
            sel = (lane_head == j).astype(BF16)
            qm = qp * sel
            sc = _dot_nt(qm, kcp)
            if has_window:
                sw = _dot_nt(qm, kw[:, cols]) + tab_ref[2 * p + j]
            if bounded:
                ec = jnp.exp2(sc - bound_ref[...])
                ew = jnp.exp2(sw)
                l = jnp.sum(ec, axis=1, keepdims=True) + jnp.sum(ew, axis=1, keepdims=True)
                o = _dot(ec.astype(BF16), vcp * sel) + _dot(ew.astype(BF16), vw[:, cols] * sel)
                acc = acc + o / l
                continue
            m = jnp.max(sc, axis=1, keepdims=True)
            if has_window:
                m = jnp.maximum(m, jnp.max(sw, axis=1, keepdims=True))
            ec = jnp.exp2(sc - m)
            l = jnp.sum(ec, axis=1, keepdims=True)
            o = _dot(ec.astype(BF16), vcp * sel)
            if has_window:
                ew = jnp.exp2(sw - m)
                l = l + jnp.sum(ew, axis=1, keepdims=True)
                o = o + _dot(ew.astype(BF16), vw[:, cols] * sel)
            acc = acc + o / l
        o_ref[:, cols] = acc.astype(BF16)


def _na_window(qa, ka, va, kac, vac, tab, bound=None, *, batch, n):
    tq = NA_Q_ROWS * GRID_W
    nt = n // tq
    in_specs = [
        pl.BlockSpec((tq, 256), lambda b, i: (b * nt + i, 0)),
        pl.BlockSpec((n, 256), lambda b, i: (b, 0)),
        pl.BlockSpec((n, 256), lambda b, i: (b, 0)),
        pl.BlockSpec((CTX_LEN, 256), lambda b, i: (b, 0)),
        pl.BlockSpec((CTX_LEN, 256), lambda b, i: (b, 0)),
        pl.BlockSpec((None, NA_HEADS, tq, NA_K_ROWS * GRID_W),
                     lambda b, i: (jnp.where(i == 0, 0, jnp.where(i == nt - 1, 2, 1)), 0, 0, 0)),
    ]
    args = [qa, ka, va, kac, vac, tab]
    if bound is not None:
        in_specs.append(pl.BlockSpec((1, 1), lambda b, i: (0, 0)))
        args.append(bound)
    return pl.pallas_call(
        functools.partial(_na_kernel, has_window=True, bounded=bound is not None),
        grid=(batch, nt),
        in_specs=in_specs,
        out_specs=pl.BlockSpec((tq, 256), lambda b, i: (b * nt + i, 0)),
        out_shape=jax.ShapeDtypeStruct((batch * n, 256), BF16),
        compiler_params=pltpu.CompilerParams(
            dimension_semantics=("arbitrary", "arbitrary"), vmem_limit_bytes=V7X_VMEM_LIMIT),
        name="na_window_bounded" if bound is not None else "na_window",
    )(*args)


def _na_dense(qac, kac, vac, *, batch):
    spec = pl.BlockSpec((CTX_LEN, 256), lambda b: (b, 0))
    return pl.pallas_call(
        functools.partial(_na_kernel, has_window=False),
        grid=(batch,),
        in_specs=[spec, spec, spec],
        out_specs=spec,
        out_shape=jax.ShapeDtypeStruct((batch * CTX_LEN, 256), BF16),
        name="na_dense",
    )(qac, kac, vac)


def _da_kernel(*refs, nk, kt, lam_init):
    if nk:
        q_ref, k_ref, vt_ref, kc_ref, vtc_ref, lq1, lk1, lq2, lk2, subg_ref, o_ref = refs[:11]
    else:
        q_ref, kc_ref, vtc_ref, lq1, lk1, lq2, lk2, subg_ref, o_ref = refs[:9]
    kall, vtall, s_buf, bm_buf, acc_ref, m_ref = refs[-6:]
    tq = o_ref.shape[0]
    nt = q_ref.shape[0] // tq
    nblk = (nk + CTX_LEN) // kt
    i = pl.program_id(2)
    lane_map = lax.broadcasted_iota(jnp.int32, (1, 128), 1) // HEAD_DIM
    sel = [(lane_map == mi).astype(BF16) for mi in range(2)]

    def scores(tile, blk, slot):
        q = q_ref[pl.ds(pl.multiple_of(tile * tq, tq), tq), :]
        kblk = kall[pl.ds(pl.multiple_of(blk * kt, kt), kt), :]
        for mi in range(2):
            s = _dot_nt(kblk, q * sel[mi])
            s_buf[slot, mi] = s
            bm_buf[slot, mi] = jnp.max(s, axis=0, keepdims=True)

    @pl.when(i == 0)
    def _():
        if nk:
            kall[0:nk, :] = k_ref[...]
            vtall[0:DA_V_DIM, 0:nk] = vt_ref[...]
        kall[nk:nk + CTX_LEN, :] = kc_ref[...]
        vtall[0:DA_V_DIM, nk:nk + CTX_LEN] = vtc_ref[...]
        vtall[DA_V_DIM:, :] = jnp.ones((ONES_ROWS, nk + CTX_LEN), BF16)
        scores(0, 0, 0)

    acc_ref[...] = jnp.zeros_like(acc_ref)
    m_ref[...] = jnp.full_like(m_ref, NEG_INF)

    def softmax_pv(blk, slot):
        vte = vtall[:, pl.ds(pl.multiple_of(blk * kt, kt), kt)]
        for mi in range(2):
            for g in range(tq // 256):
                cols = slice(256 * g, 256 * g + 256)
                m_old = m_ref[mi, :, cols]
                m_new = jnp.maximum(m_old, bm_buf[slot, mi, :, cols])
                alpha = jnp.exp2(m_old - m_new)
                e = jnp.exp2(s_buf[slot, mi, :, cols] - m_new).astype(BF16)
                acc_ref[mi, :, cols] = acc_ref[mi, :, cols] * alpha + _dot(vte, e)
                m_ref[mi, :, cols] = m_new

    if nblk == 1:
        softmax_pv(0, 0)
    else:
        for blk in range(nblk):
            if blk + 1 < nblk:
                scores(i, blk + 1, (blk + 1) % 2)
            else:
                scores(jnp.minimum(i + 1, nt - 1), 0, 0)
            softmax_pv(blk, blk % 2)

    _da_finalize(acc_ref[0], acc_ref[1], lq1, lk1, lq2, lk2, subg_ref, o_ref, lam_init)


def _da_finalize(a0, a1, lq1, lk1, lq2, lk2, subg_ref, o_ref, lam_init):
    lam = (jnp.exp(jnp.sum(lq1[...] * lk1[...], keepdims=True))
           - jnp.exp(jnp.sum(lq2[...] * lk2[...], keepdims=True)) + lam_init)
    o = a0[:DA_V_DIM] / a0[DA_V_DIM:DA_V_DIM + 1] - lam * (a1[:DA_V_DIM] / a1[DA_V_DIM:DA_V_DIM + 1])
    y = o * lax.rsqrt(jnp.mean(o * o, axis=0, keepdims=True) + EPS) * subg_ref[...] * (1.0 - lam_init)
    o_ref[...] = y.T.astype(BF16)


def _da_bounded_kernel(q_ref, k_ref, vt_ref, kc_ref, vtc_ref, bound_ref, lq1, lk1, lq2, lk2, subg_ref, o_ref,
                       *, kt, lam_init):
    q = q_ref[...]
    lane_map = lax.broadcasted_iota(jnp.int32, (1, 128), 1) // HEAD_DIM
    qms = [q * (lane_map == mi).astype(BF16) for mi in range(2)]
    shift = bound_ref[...]
    acc = [None, None]

    def block(kblk, vtblk):
        vte = jnp.concatenate([vtblk, jnp.ones((ONES_ROWS, vtblk.shape[1]), BF16)], axis=0)
        for mi in range(2):
            e = jnp.exp2(_dot_nt(kblk, qms[mi]) - shift).astype(BF16)
            d = _dot(vte, e)
            acc[mi] = d if acc[mi] is None else acc[mi] + d

    for j in range(k_ref.shape[0] // kt):
        block(k_ref[j * kt:(j + 1) * kt, :], vt_ref[:, j * kt:(j + 1) * kt])
    block(kc_ref[...], vtc_ref[...])
    _da_finalize(acc[0], acc[1], lq1, lk1, lq2, lk2, subg_ref, o_ref, lam_init)


def _diff_attention_bounded(qd, kd, vdt, kdc, vdtc, bound, lw, *, batch, nq, nk, tq, kt, lam_init):
    nt = nq // tq
    assert nk % kt == 0 and nq % tq == 0
    small = lambda shape: pl.BlockSpec(shape, lambda b, h, i: (0, 0))
    return pl.pallas_call(
        functools.partial(_da_bounded_kernel, kt=kt, lam_init=lam_init),
        grid=(batch, DA_HEADS, nt),
        in_specs=[
            pl.BlockSpec((tq, 128), lambda b, h, i: (b * nt + i, h)),
            pl.BlockSpec((nk, 128), lambda b, h, i: (b, h)),
            pl.BlockSpec((None, 128, nk), lambda b, h, i: (b, h, 0)),
            pl.BlockSpec((CTX_LEN, 128), lambda b, h, i: (b, h)),
            pl.BlockSpec((None, 128, CTX_LEN), lambda b, h, i: (b, h, 0)),
            small((1, 1)), small((1, HEAD_DIM)), small((1, HEAD_DIM)), small((1, HEAD_DIM)), small((1, HEAD_DIM)),
            small((DA_V_DIM, 1)),
        ],
        out_specs=pl.BlockSpec((tq, 128), lambda b, h, i: (b * nt + i, h)),
        out_shape=jax.ShapeDtypeStruct((batch * nq, DA_WIDTH), BF16),
        compiler_params=pltpu.CompilerParams(
            dimension_semantics=("arbitrary", "arbitrary", "arbitrary"), vmem_limit_bytes=V7X_VMEM_LIMIT),
        name="diff_attn_bounded",
    )(qd, kd, vdt, kdc, vdtc, bound, lw["lq1"], lw["lk1"], lw["lq2"], lw["lk2"], lw["sub_g"])


def _diff_attention(qd, kd, vdt, kdc, vdtc, lw, *, batch, nq, nk, tq, kt, lam_init):
    nt = nq // tq
    nkeys = nk + CTX_LEN
    nblk = nkeys // kt
    assert nkeys % kt == 0 and tq % 256 == 0 and nq % tq == 0 and (nblk == 1 and nt == 1 or nblk % 2 == 0)
    in_specs = [pl.BlockSpec((nq, 128), lambda b, h, i: (b, h))]
    args = [qd]
    if nk:
        in_specs += [pl.BlockSpec((nk, 128), lambda b, h, i: (b, h)),
                     pl.BlockSpec((None, 128, nk), lambda b, h, i: (b, h, 0))]
        args += [kd, vdt]
    in_specs += [pl.BlockSpec((CTX_LEN, 128), lambda b, h, i: (b, h)),
                 pl.BlockSpec((None, 128, CTX_LEN), lambda b, h, i: (b, h, 0))]
    args += [kdc, vdtc]
    in_specs += [pl.BlockSpec((1, HEAD_DIM), lambda b, h, i: (0, 0))] * 4
    args += [lw["lq1"], lw["lk1"], lw["lq2"], lw["lk2"]]
    in_specs += [pl.BlockSpec((DA_V_DIM, 1), lambda b, h, i: (0, 0))]
    args += [lw["sub_g"]]
    return pl.pallas_call(
        functools.partial(_da_kernel, nk=nk, kt=kt, lam_init=lam_init),
        grid=(batch, DA_HEADS, nt),
        in_specs=in_specs,
        out_specs=pl.BlockSpec((tq, 128), lambda b, h, i: (b * nt + i, h)),
        out_shape=jax.ShapeDtypeStruct((batch * nq, DA_WIDTH), BF16),
        scratch_shapes=[
            pltpu.VMEM((nkeys, 128), BF16),
            pltpu.VMEM((DA_V_DIM + ONES_ROWS, nkeys), BF16),
            pltpu.VMEM((2, 2, kt, tq), F32),
            pltpu.VMEM((2, 2, 1, tq), F32),
            pltpu.VMEM((2, DA_V_DIM + ONES_ROWS, tq), F32),
            pltpu.VMEM((2, 1, tq), F32),
        ],
        compiler_params=pltpu.CompilerParams(
            dimension_semantics=("arbitrary", "arbitrary", "arbitrary"), vmem_limit_bytes=V7X_VMEM_LIMIT),
        name="diff_attn" if nk else "diff_attn_ctx",
    )(*args)


def _ffn_kernel(x_ref, oa_ref, ob_ref, oc_ref, gate1_ref, shift2_ref, scale2_ref, gate2_ref, g2_ref,
                wout_ref, w1_ref, w3_ref, w2_ref, o_ref):
    mixed = (_dot(oa_ref[...], wout_ref[0:NA_WIDTH, :])
             + _dot(ob_ref[...], wout_ref[NA_WIDTH:NA_WIDTH + GM_WIDTH, :])
             + _dot(oc_ref[...], wout_ref[NA_WIDTH + GM_WIDTH:, :]))
    x1 = x_ref[...] + gate1_ref[0] * mixed
    h = x1 * lax.rsqrt(jnp.mean(x1 * x1, axis=-1, keepdims=True) + EPS) * g2_ref[...]
    hb = (h * (1.0 + scale2_ref[0]) + shift2_ref[0]).astype(BF16)
    a = _dot(hb, w1_ref[...])
    b = _dot(hb, w3_ref[...])
    g = (a * jax.nn.sigmoid(a) * b).astype(BF16)
    o_ref[...] = x1 + gate2_ref[0] * _dot(g, w2_ref[...])


def _out_ffn(x2d, oa, ob, oc, mod3, lw, *, batch, n, tm, per_batch_mod):
    nt = n // tm
    row = (lambda b: b) if per_batch_mod else (lambda b: CTX_MOD_ROW)
    const = lambda b, i: (0, 0)
    tok = lambda b, i: (b * nt + i, 0)
    mod_spec = lambda k: pl.BlockSpec((1, 1, D_MODEL), lambda b, i: (row(b), 0, k))
    return pl.pallas_call(
        _ffn_kernel,
        grid=(batch, nt),
        in_specs=[
            pl.BlockSpec((tm, D_MODEL), tok),
            pl.BlockSpec((tm, NA_WIDTH), tok),
            pl.BlockSpec((tm, GM_WIDTH), tok),
            pl.BlockSpec((tm, DA_WIDTH), tok),
            mod_spec(2), mod_spec(3), mod_spec(4), mod_spec(5),
            pl.BlockSpec((1, D_MODEL), const),
            _resident((D_MODEL, D_MODEL), const),
            _resident((D_MODEL, D_FF), const),
            _resident((D_MODEL, D_FF), const),
            _resident((D_FF, D_MODEL), const),
        ],
        out_specs=pl.BlockSpec((tm, D_MODEL), tok),
        out_shape=jax.ShapeDtypeStruct((batch * n, D_MODEL), F32),
        compiler_params=pltpu.CompilerParams(
            dimension_semantics=("arbitrary", "arbitrary"), vmem_limit_bytes=V7X_VMEM_LIMIT),
        name="out_ffn",
    )(x2d, oa, ob, oc, mod3, mod3, mod3, mod3, lw["norm2_g"], lw["w_out"], lw["w1"], lw["w3"], lw["w2"])


def _rope_tables(n):
    t = jnp.arange(n, dtype=jnp.int32)
    row = (t // GRID_W).astype(F32)
    col = (t % GRID_W).astype(F32)
    half = HEAD_DIM // 2
    inv = ROPE_BASE ** (-jnp.arange(0, half, 2, dtype=F32) / half)
    ar = row[:, None] * inv[None, :]
    ac = col[:, None] * inv[None, :]
    ang = jnp.concatenate([ar, ar, ac, ac], axis=-1)
    cos = jnp.cos(ang)
    sin = jnp.sin(ang)
    first = (np.arange(HEAD_DIM) % 32) < 16
    sa = jnp.where(first, -sin, 0.0)
    sb = jnp.where(first, 0.0, sin)
    return tuple(jnp.tile(a, (1, 2)) for a in (cos, sa, sb))


def _identity_rope_tables(n):
    return (jnp.ones((n, 128), F32), jnp.zeros((n, 128), F32), jnp.zeros((n, 128), F32))


def _na_window_rows(rows):
    roff, row_ok = [], []
    for r0 in (0, 2 * NA_Q_ROWS, rows - NA_Q_ROWS):
        u0 = min(max(r0 - NA_WIN_H // 2, 0), rows - NA_K_ROWS)
        r = r0 + np.arange(NA_Q_ROWS)
        key_row = u0 + np.arange(NA_K_ROWS)
        start = np.clip(r - NA_WIN_H // 2, 0, rows - NA_WIN_H)
        row_ok.append((key_row[None, :] >= start[:, None]) & (key_row[None, :] < start[:, None] + NA_WIN_H))
        roff.append(key_row[None, :] - r[:, None] + (NA_WIN_H - 1))
    return np.stack(roff), np.stack(row_ok)


def _na_bias_kernel(shift_ref, rpb_ref, o_ref, *, roff, row_ok):
    lane = lax.broadcasted_iota(jnp.int32, (GRID_W, 128), 1)
    q = lax.broadcasted_iota(jnp.int32, (GRID_W, 128), 0)
    k = lane % GRID_W
    c0 = jnp.clip(q - NA_WIN_W // 2, 0, GRID_W - NA_WIN_W)
    col_ok = jnp.abs(2 * (k - c0) - (NA_WIN_W - 1)) < NA_WIN_W
    low = lane < GRID_W
    shift = shift_ref[...]

    def toeplitz(t, a, j):
        r = int(roff[t, a, j])
        row = jnp.broadcast_to(rpb_ref[r:r + 1, :], (GRID_W, 128))
        return pltpu.roll(row, 128 - (NA_WIN_W - 1), 1, stride=1, stride_axis=0)

    for t in range(roff.shape[0]):
        for a in range(NA_Q_ROWS):
            for jp in range(NA_K_ROWS // 2):
                ok = [bool(row_ok[t, a, 2 * jp]), bool(row_ok[t, a, 2 * jp + 1])]
                tile = jnp.full((GRID_W, 128), NEG_INF, F32)
                if ok[0] or ok[1]:
                    zero = jnp.zeros((GRID_W, 128), F32)
                    even = toeplitz(t, a, 2 * jp) if ok[0] else zero
                    odd = pltpu.roll(toeplitz(t, a, 2 * jp + 1), GRID_W, 1) if ok[1] else zero
                    tile = jnp.where(col_ok, jnp.where(low, even, odd) * LOG2E - shift, NEG_INF)
                    if not ok[0]:
                        tile = jnp.where(low, NEG_INF, tile)
                    if not ok[1]:
                        tile = jnp.where(low, tile, NEG_INF)
                o_ref[t, a * GRID_W:(a + 1) * GRID_W, jp * 128:(jp + 1) * 128] = tile


def _na_bias_tables(rpb, rows, shift):
    roff, row_ok = _na_window_rows(rows)
    n_r, n_c = 2 * NA_WIN_H - 1, 2 * NA_WIN_W - 1
    rpb_p = jnp.zeros((NA_HEADS, 16, 128), F32).at[:, :n_r, :n_c].set(rpb.astype(F32))
    tq, tk = NA_Q_ROWS * GRID_W, NA_K_ROWS * GRID_W
    return pl.pallas_call(
        functools.partial(_na_bias_kernel, roff=roff, row_ok=row_ok),
        grid=(NA_HEADS,),
        in_specs=[pl.BlockSpec((1, 1), lambda h: (0, 0)), pl.BlockSpec((None, 16, 128), lambda h: (h, 0, 0))],
        out_specs=pl.BlockSpec((3, None, tq, tk), lambda h: (0, h, 0, 0)),
        out_shape=jax.ShapeDtypeStruct((3, NA_HEADS, tq, tk), F32),
        name="na_bias",
    )(jnp.reshape(shift, (1, 1)).astype(F32), rpb_p)


def _layer_weights(i, p):
    tile4 = lambda g: jnp.tile(g.astype(F32), 256 // HEAD_DIM).reshape(1, 256)
    blk = np.arange(256) // HEAD_DIM
    return {
        "norm1_g": p["norm1_g"][i].reshape(1, D_MODEL),
        "w_in": p["w_in"][i].astype(BF16),
        "gqa": tile4(p["na_q_g"][i]) * (HEAD_DIM ** -0.5 * LOG2E),
        "gka": tile4(p["na_k_g"][i]),
        "gqd": tile4(p["da_q_g"][i]) * (HEAD_DIM ** -0.5 * LOG2E),
        "gkd": tile4(p["da_k_g"][i]),
        "gmat": jnp.asarray(blk[:, None] == blk[None, :], BF16),
        "gv": p["gm_v_g"][i].reshape(1, GM_WIDTH),
        "ws_cat": p["gm_ws"][i].transpose(1, 0, 2).reshape(GM_CHUNK, GM_GROUPS * GM_CHUNK).astype(BF16),
        "bs_tab": jnp.repeat(p["gm_bs"][i].T, GM_WIDTH // GM_GROUPS, axis=1),
        "lq1": p["da_lq1"][i].reshape(1, HEAD_DIM),
        "lk1": p["da_lk1"][i].reshape(1, HEAD_DIM),
        "lq2": p["da_lq2"][i].reshape(1, HEAD_DIM),
        "lk2": p["da_lk2"][i].reshape(1, HEAD_DIM),
        "sub_g": p["da_sub_g"][i].reshape(DA_V_DIM, 1),
        "norm2_g": p["norm2_g"][i].reshape(1, D_MODEL),
        "w_out": p["w_out"][i].astype(BF16),
        "w1": p["ffn_w1"][i].astype(BF16),
        "w3": p["ffn_w3"][i].astype(BF16),
        "w2": p["ffn_w2"][i].astype(BF16),
    }


def kernel(x, c, ctx, c_ctx, w_mod, b_mod, norm1_g, w_in, na_q_g, na_k_g, na_rpb, gm_v_g, gm_ws, gm_bs,
           da_q_g, da_k_g, da_lq1, da_lk1, da_lq2, da_lk2, da_sub_g, w_out, norm2_g, ffn_w1, ffn_w3, ffn_w2):
    batch, n, _ = x.shape
    depth = w_mod.shape[0]
    assert n % (NA_Q_ROWS * GRID_W) == 0 and ctx.shape[1] == CTX_LEN and batch < CTX_MOD_ROW + 1
    params = dict(norm1_g=norm1_g, w_in=w_in, na_q_g=na_q_g, na_k_g=na_k_g, gm_v_g=gm_v_g, gm_ws=gm_ws,
                  gm_bs=gm_bs, da_q_g=da_q_g, da_k_g=da_k_g, da_lq1=da_lq1, da_lk1=da_lk1, da_lq2=da_lq2,
                  da_lk2=da_lk2, da_sub_g=da_sub_g, w_out=w_out, norm2_g=norm2_g, ffn_w1=ffn_w1,
                  ffn_w3=ffn_w3, ffn_w2=ffn_w2)

    c8 = jnp.zeros((MOD_ROWS, D_MODEL), F32).at[:batch].set(c).at[CTX_MOD_ROW].set(c_ctx)
    mod_all = _modulation(c8, w_mod, b_mod)

    rope_lat = _rope_tables(n)
    rope_ctx = _identity_rope_tables(CTX_LEN)
    x2d = x.reshape(batch * n, D_MODEL)
    xc2d = ctx.reshape(batch * CTX_LEN, D_MODEL)

    for i in range(depth):
        lam_init = 0.8 - 0.6 * math.exp(-0.3 * i)
        lw = _layer_weights(i, params)
        mod3 = mod_all[i].reshape(MOD_ROWS, 1, 6 * D_MODEL)

        qa, qd, ka, va, kd, vdt, ob = _inproj(x2d, mod3, lw, rope_lat, batch=batch, n=n, tm=INPROJ_TM,
                                              per_batch_mod=True)
        qac, qdc, kac, vac, kdc, vdtc, obc = _inproj(xc2d, mod3, lw, rope_ctx, batch=batch, n=CTX_LEN,
                                                     tm=CTX_LEN, per_batch_mod=False)
        na_bound = (_score_bound(na_q_g[i], na_k_g[i]) + LOG2E * jnp.max(jnp.abs(na_rpb[i]))).astype(F32)
        na_bounded = na_bound <= MAX_CONSTANT_SHIFT
        tab = _na_bias_tables(na_rpb[i], n // GRID_W, jnp.where(na_bounded, na_bound, 0.0))
        oa = lax.cond(
            na_bounded,
            lambda *a: _na_window(*a[:6], a[6].reshape(1, 1), batch=batch, n=n),
            lambda *a: _na_window(*a[:6], batch=batch, n=n),
            qa, ka, va, kac, vac, tab, na_bound)
        da_bound = _score_bound(da_q_g[i], da_k_g[i]).astype(F32)
        oc = lax.cond(
            da_bound <= MAX_CONSTANT_SHIFT,
            lambda *a: _diff_attention_bounded(*a, lw, batch=batch, nq=n, nk=n, tq=DA_BOUNDED_TQ,
                                               kt=DA_BOUNDED_KT, lam_init=lam_init),
            lambda *a: _diff_attention(*a[:5], lw, batch=batch, nq=n, nk=n, tq=DA_ONLINE_TQ, kt=DA_ONLINE_KT,
                                       lam_init=lam_init),
            qd, kd, vdt, kdc, vdtc, da_bound.reshape(1, 1))
        x2d = _out_ffn(x2d, oa, ob, oc, mod3, lw, batch=batch, n=n, tm=FFN_TM, per_batch_mod=True)
        if i < depth - 1:
            oac = _na_dense(qac, kac, vac, batch=batch)
            occ = _diff_attention(qdc, None, None, kdc, vdtc, lw, batch=batch, nq=CTX_LEN, nk=0,
                                  tq=CTX_LEN, kt=CTX_LEN, lam_init=lam_init)
            xc2d = _out_ffn(xc2d, oac, obc, occ, mod3, lw, batch=batch, n=CTX_LEN, tm=CTX_LEN,
                            per_batch_mod=False)
    return x2d.reshape(batch, n, D_MODEL)
```

```python
import functools
import math

import numpy as np
import jax
import jax.numpy as jnp
from jax import lax
from jax.experimental import pallas as pl
from jax.experimental.pallas import tpu as pltpu

D_MODEL = 1024
GRID_W = 64
CTX_LEN = 256
HEAD_DIM = 64
NA_HEADS = 4
NA_WIN_H = 8
NA_WIN_W = 16
NA_WIDTH = NA_HEADS * HEAD_DIM
GM_GROUPS = 4
GM_CHUNK = 128
GM_WIDTH = 256
DA_HEADS = 4
DA_QK_DIM = 2 * HEAD_DIM
DA_V_DIM = 2 * HEAD_DIM
DA_WIDTH = DA_HEADS * DA_V_DIM
QU_WIDTH = NA_WIDTH + DA_HEADS * DA_QK_DIM + 2 * GM_WIDTH
IN_WIDTH = QU_WIDTH + 2 * NA_WIDTH + DA_HEADS * DA_QK_DIM + DA_HEADS * DA_V_DIM
D_FF = -(-8 * D_MODEL // (3 * 256)) * 256
ROPE_BASE = 10000.0
EPS = 1e-6
NEG_INF = -1e30
LOG2E = math.log2(math.e)

COL_QA = 0
COL_QD = NA_WIDTH
COL_UV = COL_QD + DA_HEADS * DA_QK_DIM
COL_KA = QU_WIDTH
COL_VA = COL_KA + NA_WIDTH
COL_KD = COL_VA + NA_WIDTH
COL_VD = COL_KD + DA_HEADS * DA_QK_DIM

MOD_ROWS = 8
CTX_MOD_ROW = 4
NA_Q_ROWS = 4
NA_K_ROWS = 12
ONES_ROWS = 16
MAX_CONSTANT_SHIFT = 48.0

V7X_VMEM_LIMIT = 56 * 2 ** 20
INPROJ_TM = 1024
INPROJ_SUB = 256
FFN_TM = 512
DA_BOUNDED_TQ = 2048
DA_BOUNDED_KT = 1024
DA_ONLINE_TQ = 512
DA_ONLINE_KT = 1408


def _score_bound(gq, gk):
    return 1.02 * HEAD_DIM * (HEAD_DIM ** -0.5 * LOG2E) * jnp.max(jnp.abs(gq)) * jnp.max(jnp.abs(gk))

F32 = jnp.float32
BF16 = jnp.bfloat16


def _dot(a, b):
    return jnp.dot(a, b, preferred_element_type=F32)


def _dot_nt(a, b):
    return lax.dot_general(a, b, (((1,), (1,)), ((), ())), preferred_element_type=F32)


def _resident(shape, index_map):
    return pl.BlockSpec(shape, index_map, pipeline_mode=pl.Buffered(1))


def _mod_kernel(c_ref, w_ref, b_ref, o_ref):
    c = c_ref[...]
    a = c * jax.nn.sigmoid(c)
    a_hi = a.astype(BF16)
    a_lo = (a - a_hi.astype(F32)).astype(BF16)
    w = w_ref[...]
    w_hi = w.astype(BF16)
    w_lo = (w - w_hi.astype(F32)).astype(BF16)
    o_ref[...] = _dot(a_hi, w_hi) + _dot(a_lo, w_hi) + _dot(a_hi, w_lo) + b_ref[...]


def _modulation(c8, w_mod, b_mod):
    depth = w_mod.shape[0]
    tn = 1024
    return pl.pallas_call(
        _mod_kernel,
        grid=(depth, 6 * D_MODEL // tn),
        in_specs=[
            pl.BlockSpec((MOD_ROWS, D_MODEL), lambda l, j: (0, 0)),
            pl.BlockSpec((None, D_MODEL, tn), lambda l, j: (l, 0, j)),
            pl.BlockSpec((None, 1, tn), lambda l, j: (l, 0, j)),
        ],
        out_specs=pl.BlockSpec((None, MOD_ROWS, tn), lambda l, j: (l, 0, j)),
        out_shape=jax.ShapeDtypeStruct((depth, MOD_ROWS, 6 * D_MODEL), F32),
        name="adaln_mod",
    )(c8, w_mod, b_mod.reshape(depth, 1, 6 * D_MODEL))


def _inproj_kernel(x_ref, shift_ref, scale_ref, g1_ref, w_ref, cos_ref, sa_ref, sb_ref,
                   gqa_ref, gka_ref, gqd_ref, gkd_ref, gmat_ref, gv_ref, ws_ref, bs_ref,
                   qa_ref, qd_ref, ka_ref, va_ref, kd_ref, vdt_ref, ob_ref, *, tm, sub):
    gmat = gmat_ref[...]
    group = lax.broadcasted_iota(jnp.int32, (GM_CHUNK, GM_WIDTH), 1) // (GM_WIDTH // GM_GROUPS)
    ws = ws_ref[...]
    bs = bs_ref[...]

    def head_norm(y, g):
        ss = _dot((y * y).astype(BF16), gmat)
        return y * lax.rsqrt(ss * (1.0 / HEAD_DIM) + EPS) * g

    for r0 in range(0, tm, sub):
        rows = slice(r0, r0 + sub)
        x = x_ref[rows, :]
        h = x * lax.rsqrt(jnp.mean(x * x, axis=-1, keepdims=True) + EPS) * g1_ref[...]
        hb = (h * (1.0 + scale_ref[0]) + shift_ref[0]).astype(BF16)
        p = _dot(hb, w_ref[...])
        cos = cos_ref[rows, :]
        sa = sa_ref[rows, :]
        sb = sb_ref[rows, :]

        def rope(z):
            return z * cos + pltpu.roll(z, 128 - 16, 1) * sa + pltpu.roll(z, 16, 1) * sb

        qa_ref[rows, :] = head_norm(p[:, COL_QA:COL_QA + 256], gqa_ref[...]).astype(BF16)
        ka_ref[rows, :] = head_norm(p[:, COL_KA:COL_KA + 256], gka_ref[...]).astype(BF16)
        va_ref[rows, :] = p[:, COL_VA:COL_VA + 256].astype(BF16)
        for c in range(2):
            yq = head_norm(p[:, COL_QD + 256 * c:COL_QD + 256 * c + 256], gqd_ref[...])
            yk = head_norm(p[:, COL_KD + 256 * c:COL_KD + 256 * c + 256], gkd_ref[...])
            for t in range(2):
                lo = 256 * c + 128 * t
                qd_ref[rows, lo:lo + 128] = rope(yq[:, 128 * t:128 * t + 128]).astype(BF16)
                kd_ref[rows, lo:lo + 128] = rope(yk[:, 128 * t:128 * t + 128]).astype(BF16)
        vdt_ref[:, rows] = p[:, COL_VD:COL_VD + 512].T.astype(BF16)

        z = jax.nn.gelu(p[:, COL_UV:COL_UV + 2 * GM_WIDTH])
        u = z[:, :GM_WIDTH]
        v = z[:, GM_WIDTH:]
        v = v * lax.rsqrt(jnp.mean(v * v, axis=-1, keepdims=True) + EPS) * gv_ref[...]
        vb = v.astype(BF16)
        for c in range(sub // GM_CHUNK):
            vc = vb[c * GM_CHUNK:(c + 1) * GM_CHUNK, :]
            vbd = jnp.concatenate([jnp.where(group == g, vc, jnp.zeros_like(vc)) for g in range(GM_GROUPS)],
                                  axis=0)
            s = _dot(ws, vbd) + bs
            lo = r0 + c * GM_CHUNK
            ob_ref[lo:lo + GM_CHUNK, :] = (u[c * GM_CHUNK:(c + 1) * GM_CHUNK, :] * s).astype(BF16)


def _inproj(x2d, mod3, lw, rope_tabs, *, batch, n, tm, per_batch_mod):
    nt = n // tm
    t_tot = batch * n
    row = (lambda b: b) if per_batch_mod else (lambda b: CTX_MOD_ROW)
    const = lambda b, i: (0, 0)
    tok = lambda b, i: (b * nt + i, 0)
    in_specs = [
        pl.BlockSpec((tm, D_MODEL), tok),
        pl.BlockSpec((1, 1, D_MODEL), lambda b, i: (row(b), 0, 0)),
        pl.BlockSpec((1, 1, D_MODEL), lambda b, i: (row(b), 0, 1)),
        pl.BlockSpec((1, D_MODEL), const),
        _resident((D_MODEL, IN_WIDTH), const),
        pl.BlockSpec((tm, 128), lambda b, i: (i, 0)),
        pl.BlockSpec((tm, 128), lambda b, i: (i, 0)),
        pl.BlockSpec((tm, 128), lambda b, i: (i, 0)),
        pl.BlockSpec((1, 256), const),
        pl.BlockSpec((1, 256), const),
        pl.BlockSpec((1, 256), const),
        pl.BlockSpec((1, 256), const),
        pl.BlockSpec((256, 256), const),
        pl.BlockSpec((1, GM_WIDTH), const),
        pl.BlockSpec((GM_CHUNK, GM_GROUPS * GM_CHUNK), const),
        pl.BlockSpec((GM_CHUNK, GM_WIDTH), const),
    ]
    out_specs = [
        pl.BlockSpec((tm, 256), tok),
        pl.BlockSpec((tm, 512), tok),
        pl.BlockSpec((tm, 256), tok),
        pl.BlockSpec((tm, 256), tok),
        pl.BlockSpec((tm, 512), tok),
        pl.BlockSpec((None, 512, tm), lambda b, i: (b, 0, i)),
        pl.BlockSpec((tm, 256), tok),
    ]
    out_shape = [
        jax.ShapeDtypeStruct((t_tot, 256), BF16),
        jax.ShapeDtypeStruct((t_tot, 512), BF16),
        jax.ShapeDtypeStruct((t_tot, 256), BF16),
        jax.ShapeDtypeStruct((t_tot, 256), BF16),
        jax.ShapeDtypeStruct((t_tot, 512), BF16),
        jax.ShapeDtypeStruct((batch, 512, n), BF16),
        jax.ShapeDtypeStruct((t_tot, 256), BF16),
    ]
    return pl.pallas_call(
        functools.partial(_inproj_kernel, tm=tm, sub=min(tm, INPROJ_SUB)),
        grid=(batch, nt),
        in_specs=in_specs,
        out_specs=out_specs,
        out_shape=out_shape,
        compiler_params=pltpu.CompilerParams(
            dimension_semantics=("arbitrary", "arbitrary"), vmem_limit_bytes=V7X_VMEM_LIMIT),
        name="inproj",
    )(x2d, mod3, mod3, lw["norm1_g"], lw["w_in"], *rope_tabs,
      lw["gqa"], lw["gka"], lw["gqd"], lw["gkd"], lw["gmat"], lw["gv"], lw["ws_cat"], lw["bs_tab"])


def _na_kernel(*refs, has_window, bounded=False):
    if bounded:
        q_ref, k_ref, v_ref, kc_ref, vc_ref, tab_ref, bound_ref, o_ref = refs
        refs = refs[:6] + refs[7:]
    if has_window:
        q_ref, k_ref, v_ref, kc_ref, vc_ref, tab_ref, o_ref = refs
        i = pl.program_id(1)
        u0 = jnp.clip(NA_Q_ROWS * i - NA_WIN_H // 2, 0, k_ref.shape[0] // GRID_W - NA_K_ROWS)
        off = pl.multiple_of(u0 * GRID_W, GRID_W)
        kw = k_ref[pl.ds(off, NA_K_ROWS * GRID_W), :]
        vw = v_ref[pl.ds(off, NA_K_ROWS * GRID_W), :]
    else:
        q_ref, kc_ref, vc_ref, o_ref = refs
    lane = lax.broadcasted_iota(jnp.int32, (1, 128), 1)
    lane_head = lane // HEAD_DIM
    for p in range(NA_HEADS // 2):
        cols = slice(128 * p, 128 * p + 128)
        qp = q_ref[:, cols]
        kcp = kc_ref[:, cols]
        vcp = vc_ref[:, cols]
        acc = jnp.zeros((qp.shape[0], 128), F32)
        for j in range(2):
            sel = (lane_head == j).astype(BF16)
            qm = qp * sel
            sc = _dot_nt(qm, kcp)
            if has_window:
                sw = _dot_nt(qm, kw[:, cols]) + tab_ref[2 * p + j]
            if bounded:
                ec = jnp.exp2(sc - bound_ref[...])
                ew = jnp.exp2(sw)
                l = jnp.sum(ec, axis=1, keepdims=True) + jnp.sum(ew, axis=1, keepdims=True)
                o = _dot(ec.astype(BF16), vcp * sel) + _dot(ew.astype(BF16), vw[:, cols] * sel)
                acc = acc + o / l
                continue
            m = jnp.max(sc, axis=1, keepdims=True)
            if has_window:
                m = jnp.maximum(m, jnp.max(sw, axis=1, keepdims=True))
            ec = jnp.exp2(sc - m)
            l = jnp.sum(ec, axis=1, keepdims=True)
            o = _dot(ec.astype(BF16), vcp * sel)
            if has_window:
                ew = jnp.exp2(sw - m)
                l = l + jnp.sum(ew, axis=1, keepdims=True)
                o = o + _dot(ew.astype(BF16), vw[:, cols] * sel)
            acc = acc + o / l
        o_ref[:, cols] = acc.astype(BF16)


def _na_window(qa, ka, va, kac, vac, tab, bound=None, *, batch, n):
    tq = NA_Q_ROWS * GRID_W
    nt = n // tq
    in_specs = [
        pl.BlockSpec((tq, 256), lambda b, i: (b * nt + i, 0)),
        pl.BlockSpec((n, 256), lambda b, i: (b, 0)),
        pl.BlockSpec((n, 256), lambda b, i: (b, 0)),
        pl.BlockSpec((CTX_LEN, 256), lambda b, i: (b, 0)),
        pl.BlockSpec((CTX_LEN, 256), lambda b, i: (b, 0)),
        pl.BlockSpec((None, NA_HEADS, tq, NA_K_ROWS * GRID_W),
                     lambda b, i: (jnp.where(i == 0, 0, jnp.where(i == nt - 1, 2, 1)), 0, 0, 0)),
    ]
    args = [qa, ka, va, kac, vac, tab]
    if bound is not None:
        in_specs.append(pl.BlockSpec((1, 1), lambda b, i: (0, 0)))
        args.append(bound)
    return pl.pallas_call(
        functools.partial(_na_kernel, has_window=True, bounded=bound is not None),
        grid=(batch, nt),
        in_specs=in_specs,
        out_specs=pl.BlockSpec((tq, 256), lambda b, i: (b * nt + i, 0)),
        out_shape=jax.ShapeDtypeStruct((batch * n, 256), BF16),
        compiler_params=pltpu.CompilerParams(
            dimension_semantics=("arbitrary", "arbitrary"), vmem_limit_bytes=V7X_VMEM_LIMIT),
        name="na_window_bounded" if bound is not None else "na_window",
    )(*args)


def _na_dense(qac, kac, vac, *, batch):
    spec = pl.BlockSpec((CTX_LEN, 256), lambda b: (b, 0))
    return pl.pallas_call(
        functools.partial(_na_kernel, has_window=False),
        grid=(batch,),
        in_specs=[spec, spec, spec],
        out_specs=spec,
        out_shape=jax.ShapeDtypeStruct((batch * CTX_LEN, 256), BF16),
        name="na_dense",
    )(qac, kac, vac)


def _da_kernel(*refs, nk, kt, lam_init):
    if nk:
        q_ref, k_ref, vt_ref, kc_ref, vtc_ref, lq1, lk1, lq2, lk2, subg_ref, o_ref = refs[:11]
    else:
        q_ref, kc_ref, vtc_ref, lq1, lk1, lq2, lk2, subg_ref, o_ref = refs[:9]
    kall, vtall, s_buf, bm_buf, acc_ref, m_ref = refs[-6:]
    tq = o_ref.shape[0]
    nt = q_ref.shape[0] // tq
    nblk = (nk + CTX_LEN) // kt
    i = pl.program_id(2)
    lane_map = lax.broadcasted_iota(jnp.int32, (1, 128), 1) // HEAD_DIM
    sel = [(lane_map == mi).astype(BF16) for mi in range(2)]

    def scores(tile, blk, slot):
        q = q_ref[pl.ds(pl.multiple_of(tile * tq, tq), tq), :]
        kblk = kall[pl.ds(pl.multiple_of(blk * kt, kt), kt), :]
        for mi in range(2):
            s = _dot_nt(kblk, q * sel[mi])
            s_buf[slot, mi] = s
            bm_buf[slot, mi] = jnp.max(s, axis=0, keepdims=True)

    @pl.when(i == 0)
    def _():
        if nk:
            kall[0:nk, :] = k_ref[...]
            vtall[0:DA_V_DIM, 0:nk] = vt_ref[...]
        kall[nk:nk + CTX_LEN, :] = kc_ref[...]
        vtall[0:DA_V_DIM, nk:nk + CTX_LEN] = vtc_ref[...]
        vtall[DA_V_DIM:, :] = jnp.ones((ONES_ROWS, nk + CTX_LEN), BF16)
        scores(0, 0, 0)

    acc_ref[...] = jnp.zeros_like(acc_ref)
    m_ref[...] = jnp.full_like(m_ref, NEG_INF)

    def softmax_pv(blk, slot):
        vte = vtall[:, pl.ds(pl.multiple_of(blk * kt, kt), kt)]
        for mi in range(2):
            for g in range(tq // 256):
                cols = slice(256 * g, 256 * g + 256)
                m_old = m_ref[mi, :, cols]
                m_new = jnp.maximum(m_old, bm_buf[slot, mi, :, cols])
                alpha = jnp.exp2(m_old - m_new)
                e = jnp.exp2(s_buf[slot, mi, :, cols] - m_new).astype(BF16)
                acc_ref[mi, :, cols] = acc_ref[mi, :, cols] * alpha + _dot(vte, e)
                m_ref[mi, :, cols] = m_new

    if nblk == 1:
        softmax_pv(0, 0)
    else:
        for blk in range(nblk):
            if blk + 1 < nblk:
                scores(i, blk + 1, (blk + 1) % 2)
            else:
                scores(jnp.minimum(i + 1, nt - 1), 0, 0)
            softmax_pv(blk, blk % 2)

    _da_finalize(acc_ref[0], acc_ref[1], lq1, lk1, lq2, lk2, subg_ref, o_ref, lam_init)


def _da_finalize(a0, a1, lq1, lk1, lq2, lk2, subg_ref, o_ref, lam_init):
    lam = (jnp.exp(jnp.sum(lq1[...] * lk1[...], keepdims=True))
           - jnp.exp(jnp.sum(lq2[...] * lk2[...], keepdims=True)) + lam_init)
    o = a0[:DA_V_DIM] / a0[DA_V_DIM:DA_V_DIM + 1] - lam * (a1[:DA_V_DIM] / a1[DA_V_DIM:DA_V_DIM + 1])
    y = o * lax.rsqrt(jnp.mean(o * o, axis=0, keepdims=True) + EPS) * subg_ref[...] * (1.0 - lam_init)
    o_ref[...] = y.T.astype(BF16)


def _da_bounded_kernel(q_ref, k_ref, vt_ref, kc_ref, vtc_ref, bound_ref, lq1, lk1, lq2, lk2, subg_ref, o_ref,
                       *, kt, lam_init):
    q = q_ref[...]
    lane_map = lax.broadcasted_iota(jnp.int32, (1, 128), 1) // HEAD_DIM
    qms = [q * (lane_map == mi).astype(BF16) for mi in range(2)]
    shift = bound_ref[...]
    acc = [None, None]

    den = [None, None]

    def block(kblk, vtblk):
        for mi in range(2):
            ef = jnp.exp2(_dot_nt(kblk, qms[mi]) - shift)
            ls = jnp.sum(ef, axis=0, keepdims=True)
            d = _dot(vtblk, ef.astype(BF16))
            acc[mi] = d if acc[mi] is None else acc[mi] + d
            den[mi] = ls if den[mi] is None else den[mi] + ls

    for j in range(k_ref.shape[0] // kt):
        block(k_ref[j * kt:(j + 1) * kt, :], vt_ref[:, j * kt:(j + 1) * kt])
    block(kc_ref[...], vtc_ref[...])
    pad = jnp.ones((ONES_ROWS - 1, q.shape[0]), F32)
    _da_finalize(jnp.concatenate([acc[0], den[0], pad], axis=0), jnp.concatenate([acc[1], den[1], pad], axis=0),
                 lq1, lk1, lq2, lk2, subg_ref, o_ref, lam_init)


def _diff_attention_bounded(qd, kd, vdt, kdc, vdtc, bound, lw, *, batch, nq, nk, tq, kt, lam_init):
    nt = nq // tq
    assert nk % kt == 0 and nq % tq == 0
    small = lambda shape: pl.BlockSpec(shape, lambda b, h, i: (0, 0))
    return pl.pallas_call(
        functools.partial(_da_bounded_kernel, kt=kt, lam_init=lam_init),
        grid=(batch, DA_HEADS, nt),
        in_specs=[
            pl.BlockSpec((tq, 128), lambda b, h, i: (b * nt + i, h)),
            pl.BlockSpec((nk, 128), lambda b, h, i: (b, h)),
            pl.BlockSpec((None, 128, nk), lambda b, h, i: (b, h, 0)),
            pl.BlockSpec((CTX_LEN, 128), lambda b, h, i: (b, h)),
            pl.BlockSpec((None, 128, CTX_LEN), lambda b, h, i: (b, h, 0)),
            small((1, 1)), small((1, HEAD_DIM)), small((1, HEAD_DIM)), small((1, HEAD_DIM)), small((1, HEAD_DIM)),
            small((DA_V_DIM, 1)),
        ],
        out_specs=pl.BlockSpec((tq, 128), lambda b, h, i: (b * nt + i, h)),
        out_shape=jax.ShapeDtypeStruct((batch * nq, DA_WIDTH), BF16),
        compiler_params=pltpu.CompilerParams(
            dimension_semantics=("arbitrary", "arbitrary", "arbitrary"), vmem_limit_bytes=V7X_VMEM_LIMIT),
        name="diff_attn_bounded",
    )(qd, kd, vdt, kdc, vdtc, bound, lw["lq1"], lw["lk1"], lw["lq2"], lw["lk2"], lw["sub_g"])


def _diff_attention(qd, kd, vdt, kdc, vdtc, lw, *, batch, nq, nk, tq, kt, lam_init):
    nt = nq // tq
    nkeys = nk + CTX_LEN
    nblk = nkeys // kt
    assert nkeys % kt == 0 and tq % 256 == 0 and nq % tq == 0 and (nblk == 1 and nt == 1 or nblk % 2 == 0)
    in_specs = [pl.BlockSpec((nq, 128), lambda b, h, i: (b, h))]
    args = [qd]
    if nk:
        in_specs += [pl.BlockSpec((nk, 128), lambda b, h, i: (b, h)),
                     pl.BlockSpec((None, 128, nk), lambda b, h, i: (b, h, 0))]
        args += [kd, vdt]
    in_specs += [pl.BlockSpec((CTX_LEN, 128), lambda b, h, i: (b, h)),
                 pl.BlockSpec((None, 128, CTX_LEN), lambda b, h, i: (b, h, 0))]
    args += [kdc, vdtc]
    in_specs += [pl.BlockSpec((1, HEAD_DIM), lambda b, h, i: (0, 0))] * 4
    args += [lw["lq1"], lw["lk1"], lw["lq2"], lw["lk2"]]
    in_specs += [pl.BlockSpec((DA_V_DIM, 1), lambda b, h, i: (0, 0))]
    args += [lw["sub_g"]]
    return pl.pallas_call(
        functools.partial(_da_kernel, nk=nk, kt=kt, lam_init=lam_init),
        grid=(batch, DA_HEADS, nt),
        in_specs=in_specs,
        out_specs=pl.BlockSpec((tq, 128), lambda b, h, i: (b * nt + i, h)),
        out_shape=jax.ShapeDtypeStruct((batch * nq, DA_WIDTH), BF16),
        scratch_shapes=[
            pltpu.VMEM((nkeys, 128), BF16),
            pltpu.VMEM((DA_V_DIM + ONES_ROWS, nkeys), BF16),
            pltpu.VMEM((2, 2, kt, tq), F32),
            pltpu.VMEM((2, 2, 1, tq), F32),
            pltpu.VMEM((2, DA_V_DIM + ONES_ROWS, tq), F32),
            pltpu.VMEM((2, 1, tq), F32),
        ],
        compiler_params=pltpu.CompilerParams(
            dimension_semantics=("arbitrary", "arbitrary", "arbitrary"), vmem_limit_bytes=V7X_VMEM_LIMIT),
        name="diff_attn" if nk else "diff_attn_ctx",
    )(*args)


def _ffn_kernel(x_ref, oa_ref, ob_ref, oc_ref, gate1_ref, shift2_ref, scale2_ref, gate2_ref, g2_ref,
                wout_ref, w1_ref, w3_ref, w2_ref, o_ref):
    mixed = (_dot(oa_ref[...], wout_ref[0:NA_WIDTH, :])
             + _dot(ob_ref[...], wout_ref[NA_WIDTH:NA_WIDTH + GM_WIDTH, :])
             + _dot(oc_ref[...], wout_ref[NA_WIDTH + GM_WIDTH:, :]))
    x1 = x_ref[...] + gate1_ref[0] * mixed
    h = x1 * lax.rsqrt(jnp.mean(x1 * x1, axis=-1, keepdims=True) + EPS) * g2_ref[...]
    hb = (h * (1.0 + scale2_ref[0]) + shift2_ref[0]).astype(BF16)
    a = _dot(hb, w1_ref[...])
    b = _dot(hb, w3_ref[...])
    g = (a * jax.nn.sigmoid(a) * b).astype(BF16)
    o_ref[...] = x1 + gate2_ref[0] * _dot(g, w2_ref[...])


def _out_ffn(x2d, oa, ob, oc, mod3, lw, *, batch, n, tm, per_batch_mod):
    nt = n // tm
    row = (lambda b: b) if per_batch_mod else (lambda b: CTX_MOD_ROW)
    const = lambda b, i: (0, 0)
    tok = lambda b, i: (b * nt + i, 0)
    mod_spec = lambda k: pl.BlockSpec((1, 1, D_MODEL), lambda b, i: (row(b), 0, k))
    return pl.pallas_call(
        _ffn_kernel,
        grid=(batch, nt),
        in_specs=[
            pl.BlockSpec((tm, D_MODEL), tok),
            pl.BlockSpec((tm, NA_WIDTH), tok),
            pl.BlockSpec((tm, GM_WIDTH), tok),
            pl.BlockSpec((tm, DA_WIDTH), tok),
            mod_spec(2), mod_spec(3), mod_spec(4), mod_spec(5),
            pl.BlockSpec((1, D_MODEL), const),
            _resident((D_MODEL, D_MODEL), const),
            _resident((D_MODEL, D_FF), const),
            _resident((D_MODEL, D_FF), const),
            _resident((D_FF, D_MODEL), const),
        ],
        out_specs=pl.BlockSpec((tm, D_MODEL), tok),
        out_shape=jax.ShapeDtypeStruct((batch * n, D_MODEL), F32),
        compiler_params=pltpu.CompilerParams(
            dimension_semantics=("arbitrary", "arbitrary"), vmem_limit_bytes=V7X_VMEM_LIMIT),
        name="out_ffn",
    )(x2d, oa, ob, oc, mod3, mod3, mod3, mod3, lw["norm2_g"], lw["w_out"], lw["w1"], lw["w3"], lw["w2"])


def _rope_tables(n):
    t = jnp.arange(n, dtype=jnp.int32)
    row = (t // GRID_W).astype(F32)
    col = (t % GRID_W).astype(F32)
    half = HEAD_DIM // 2
    inv = ROPE_BASE ** (-jnp.arange(0, half, 2, dtype=F32) / half)
    ar = row[:, None] * inv[None, :]
    ac = col[:, None] * inv[None, :]
    ang = jnp.concatenate([ar, ar, ac, ac], axis=-1)
    cos = jnp.cos(ang)
    sin = jnp.sin(ang)
    first = (np.arange(HEAD_DIM) % 32) < 16
    sa = jnp.where(first, -sin, 0.0)
    sb = jnp.where(first, 0.0, sin)
    return tuple(jnp.tile(a, (1, 2)) for a in (cos, sa, sb))


def _identity_rope_tables(n):
    return (jnp.ones((n, 128), F32), jnp.zeros((n, 128), F32), jnp.zeros((n, 128), F32))


def _na_window_rows(rows):
    roff, row_ok = [], []
    for r0 in (0, 2 * NA_Q_ROWS, rows - NA_Q_ROWS):
        u0 = min(max(r0 - NA_WIN_H // 2, 0), rows - NA_K_ROWS)
        r = r0 + np.arange(NA_Q_ROWS)
        key_row = u0 + np.arange(NA_K_ROWS)
        start = np.clip(r - NA_WIN_H // 2, 0, rows - NA_WIN_H)
        row_ok.append((key_row[None, :] >= start[:, None]) & (key_row[None, :] < start[:, None] + NA_WIN_H))
        roff.append(key_row[None, :] - r[:, None] + (NA_WIN_H - 1))
    return np.stack(roff), np.stack(row_ok)


def _na_bias_kernel(shift_ref, rpb_ref, o_ref, *, roff, row_ok):
    lane = lax.broadcasted_iota(jnp.int32, (GRID_W, 128), 1)
    q = lax.broadcasted_iota(jnp.int32, (GRID_W, 128), 0)
    k = lane % GRID_W
    c0 = jnp.clip(q - NA_WIN_W // 2, 0, GRID_W - NA_WIN_W)
    col_ok = jnp.abs(2 * (k - c0) - (NA_WIN_W - 1)) < NA_WIN_W
    low = lane < GRID_W
    shift = shift_ref[...]

    def toeplitz(t, a, j):
        r = int(roff[t, a, j])
        row = jnp.broadcast_to(rpb_ref[r:r + 1, :], (GRID_W, 128))
        return pltpu.roll(row, 128 - (NA_WIN_W - 1), 1, stride=1, stride_axis=0)

    for t in range(roff.shape[0]):
        for a in range(NA_Q_ROWS):
            for jp in range(NA_K_ROWS // 2):
                ok = [bool(row_ok[t, a, 2 * jp]), bool(row_ok[t, a, 2 * jp + 1])]
                tile = jnp.full((GRID_W, 128), NEG_INF, F32)
                if ok[0] or ok[1]:
                    zero = jnp.zeros((GRID_W, 128), F32)
                    even = toeplitz(t, a, 2 * jp) if ok[0] else zero
                    odd = pltpu.roll(toeplitz(t, a, 2 * jp + 1), GRID_W, 1) if ok[1] else zero
                    tile = jnp.where(col_ok, jnp.where(low, even, odd) * LOG2E - shift, NEG_INF)
                    if not ok[0]:
                        tile = jnp.where(low, NEG_INF, tile)
                    if not ok[1]:
                        tile = jnp.where(low, tile, NEG_INF)
                o_ref[t, a * GRID_W:(a + 1) * GRID_W, jp * 128:(jp + 1) * 128] = tile


def _na_bias_tables(rpb, rows, shift):
    roff, row_ok = _na_window_rows(rows)
    n_r, n_c = 2 * NA_WIN_H - 1, 2 * NA_WIN_W - 1
    rpb_p = jnp.zeros((NA_HEADS, 16, 128), F32).at[:, :n_r, :n_c].set(rpb.astype(F32))
    tq, tk = NA_Q_ROWS * GRID_W, NA_K_ROWS * GRID_W
    return pl.pallas_call(
        functools.partial(_na_bias_kernel, roff=roff, row_ok=row_ok),
        grid=(NA_HEADS,),
        in_specs=[pl.BlockSpec((1, 1), lambda h: (0, 0)), pl.BlockSpec((None, 16, 128), lambda h: (h, 0, 0))],
        out_specs=pl.BlockSpec((3, None, tq, tk), lambda h: (0, h, 0, 0)),
        out_shape=jax.ShapeDtypeStruct((3, NA_HEADS, tq, tk), F32),
        name="na_bias",
    )(jnp.reshape(shift, (1, 1)).astype(F32), rpb_p)


def _layer_weights(i, p):
    tile4 = lambda g: jnp.tile(g.astype(F32), 256 // HEAD_DIM).reshape(1, 256)
    blk = np.arange(256) // HEAD_DIM
    return {
        "norm1_g": p["norm1_g"][i].reshape(1, D_MODEL),
        "w_in": p["w_in"][i].astype(BF16),
        "gqa": tile4(p["na_q_g"][i]) * (HEAD_DIM ** -0.5 * LOG2E),
        "gka": tile4(p["na_k_g"][i]),
        "gqd": tile4(p["da_q_g"][i]) * (HEAD_DIM ** -0.5 * LOG2E),
        "gkd": tile4(p["da_k_g"][i]),
        "gmat": jnp.asarray(blk[:, None] == blk[None, :], BF16),
        "gv": p["gm_v_g"][i].reshape(1, GM_WIDTH),
        "ws_cat": p["gm_ws"][i].transpose(1, 0, 2).reshape(GM_CHUNK, GM_GROUPS * GM_CHUNK).astype(BF16),
        "bs_tab": jnp.repeat(p["gm_bs"][i].T, GM_WIDTH // GM_GROUPS, axis=1),
        "lq1": p["da_lq1"][i].reshape(1, HEAD_DIM),
        "lk1": p["da_lk1"][i].reshape(1, HEAD_DIM),
        "lq2": p["da_lq2"][i].reshape(1, HEAD_DIM),
        "lk2": p["da_lk2"][i].reshape(1, HEAD_DIM),
        "sub_g": p["da_sub_g"][i].reshape(DA_V_DIM, 1),
        "norm2_g": p["norm2_g"][i].reshape(1, D_MODEL),
        "w_out": p["w_out"][i].astype(BF16),
        "w1": p["ffn_w1"][i].astype(BF16),
        "w3": p["ffn_w3"][i].astype(BF16),
        "w2": p["ffn_w2"][i].astype(BF16),
    }


def kernel(x, c, ctx, c_ctx, w_mod, b_mod, norm1_g, w_in, na_q_g, na_k_g, na_rpb, gm_v_g, gm_ws, gm_bs,
           da_q_g, da_k_g, da_lq1, da_lk1, da_lq2, da_lk2, da_sub_g, w_out, norm2_g, ffn_w1, ffn_w3, ffn_w2):
    batch, n, _ = x.shape
    depth = w_mod.shape[0]
    assert n % (NA_Q_ROWS * GRID_W) == 0 and ctx.shape[1] == CTX_LEN and batch < CTX_MOD_ROW + 1
    params = dict(norm1_g=norm1_g, w_in=w_in, na_q_g=na_q_g, na_k_g=na_k_g, gm_v_g=gm_v_g, gm_ws=gm_ws,
                  gm_bs=gm_bs, da_q_g=da_q_g, da_k_g=da_k_g, da_lq1=da_lq1, da_lk1=da_lk1, da_lq2=da_lq2,
                  da_lk2=da_lk2, da_sub_g=da_sub_g, w_out=w_out, norm2_g=norm2_g, ffn_w1=ffn_w1,
                  ffn_w3=ffn_w3, ffn_w2=ffn_w2)

    c8 = jnp.zeros((MOD_ROWS, D_MODEL), F32).at[:batch].set(c).at[CTX_MOD_ROW].set(c_ctx)
    mod_all = _modulation(c8, w_mod, b_mod)

    rope_lat = _rope_tables(n)
    rope_ctx = _identity_rope_tables(CTX_LEN)
    x2d = x.reshape(batch * n, D_MODEL)
    xc2d = ctx.reshape(batch * CTX_LEN, D_MODEL)

    for i in range(depth):
        lam_init = 0.8 - 0.6 * math.exp(-0.3 * i)
        lw = _layer_weights(i, params)
        mod3 = mod_all[i].reshape(MOD_ROWS, 1, 6 * D_MODEL)

        qa, qd, ka, va, kd, vdt, ob = _inproj(x2d, mod3, lw, rope_lat, batch=batch, n=n, tm=INPROJ_TM,
                                              per_batch_mod=True)
        qac, qdc, kac, vac, kdc, vdtc, obc = _inproj(xc2d, mod3, lw, rope_ctx, batch=batch, n=CTX_LEN,
                                                     tm=CTX_LEN, per_batch_mod=False)
        na_bound = (_score_bound(na_q_g[i], na_k_g[i]) + LOG2E * jnp.max(jnp.abs(na_rpb[i]))).astype(F32)
        na_bounded = na_bound <= MAX_CONSTANT_SHIFT
        tab = _na_bias_tables(na_rpb[i], n // GRID_W, jnp.where(na_bounded, na_bound, 0.0))
        oa = lax.cond(
            na_bounded,
            lambda *a: _na_window(*a[:6], a[6].reshape(1, 1), batch=batch, n=n),
            lambda *a: _na_window(*a[:6], batch=batch, n=n),
            qa, ka, va, kac, vac, tab, na_bound)
        da_bound = _score_bound(da_q_g[i], da_k_g[i]).astype(F32)
        oc = lax.cond(
            da_bound <= MAX_CONSTANT_SHIFT,
            lambda *a: _diff_attention_bounded(*a, lw, batch=batch, nq=n, nk=n, tq=DA_BOUNDED_TQ,
                                               kt=DA_BOUNDED_KT, lam_init=lam_init),
            lambda *a: _diff_attention(*a[:5], lw, batch=batch, nq=n, nk=n, tq=DA_ONLINE_TQ, kt=DA_ONLINE_KT,
                                       lam_init=lam_init),
            qd, kd, vdt, kdc, vdtc, da_bound.reshape(1, 1))
        x2d = _out_ffn(x2d, oa, ob, oc, mod3, lw, batch=batch, n=n, tm=FFN_TM, per_batch_mod=True)
        if i < depth - 1:
            oac = _na_dense(qac, kac, vac, batch=batch)
            occ = _diff_attention(qdc, None, None, kdc, vdtc, lw, batch=batch, nq=CTX_LEN, nk=0,
                                  tq=CTX_LEN, kt=CTX_LEN, lam_init=lam_init)
            xc2d = _out_ffn(xc2d, oac, obc, occ, mod3, lw, batch=batch, n=CTX_LEN, tm=CTX_LEN,
                            per_batch_mod=False)
    return x2d.reshape(batch, n, D_MODEL)
```
